```python
import jax
import jax.numpy as jnp
from jax import lax
import numpy as np

D_MODEL = 4096
BATCH = 4
SEQ = 4096
DEPTH = 4

GRID_W = 64
CTX_LEN = 256
N_EVEN = (DEPTH + 1) // 2
N_ODD = DEPTH // 2

W_A = D_MODEL // 2
HEAD_A = 64
H_A = W_A // HEAD_A
R_DECAY = max(32, int(round(1.8 * W_A ** 0.5 / 32)) * 32)
R_AAA = max(32, int(round(1.8 * W_A ** 0.5 / 32)) * 32)
R_GATE = max(32, int(round(0.6 * W_A ** 0.8 / 32)) * 32)
IN_A = 3 * W_A + 2 * R_DECAY + 2 * R_AAA + R_GATE
W_B = D_MODEL // 2
H_B = 8
CHUNK = 128
W_C = D_MODEL // 2
POOL_WINDOWS = (2, 4, 8, 16)
G_C = W_C // len(POOL_WINDOWS)
W_D = D_MODEL // 2
CONV_K = 3

IN_EVEN = IN_A + 2 * W_B
IN_ODD = W_C + 3 * W_D
D_FF = 5 * D_MODEL // 4
ALPHA = (2 * DEPTH) ** 0.25
BETA = (8 * DEPTH) ** -0.25
LN_EPS = 1e-6
GN_EPS = 64e-5

kernel_name = "hybrid_rwkv7_sgu_pool_shortconv_dit"


def _layernorm(x, g, b, eps):
    xf = x.astype(jnp.float32)
    mu = jnp.mean(xf, -1, keepdims=True)
    var = jnp.mean(jnp.square(xf - mu), -1, keepdims=True)
    y = (xf - mu) * lax.rsqrt(var + eps) * g.astype(jnp.float32) + b.astype(jnp.float32)
    return y.astype(x.dtype)


def _dwconv_seq(x, w):
    C = x.shape[-1]
    return lax.conv_general_dilated(x, w[:, None, :].astype(x.dtype), (1,), ((1, 1),),
                                    dimension_numbers=('NWC', 'WIO', 'NWC'), feature_group_count=C)


def _dwconv_grid(x, w, axis):
    B, T, C = x.shape
    rows = T // GRID_W
    xg = x.reshape(B, rows, GRID_W, C)
    if axis == 'row':
        k, pad = w[None, :, None, :], ((0, 0), (1, 1))
    else:
        k, pad = w[:, None, None, :], ((1, 1), (0, 0))
    y = lax.conv_general_dilated(xg, k.astype(x.dtype), (1, 1), pad,
                                 dimension_numbers=('NHWC', 'HWIO', 'NHWC'), feature_group_count=C)
    return y.reshape(B, T, C)


def _modulation(cvec, ada_w, ada_b):
    return jax.nn.silu(cvec) @ ada_w + ada_b


def _rwkv_streams(pa, shift_w, w0, w2, a0, a2, g2, k_k, k_a):
    B, T, _ = pa.shape
    f32 = lambda z: z.astype(jnp.float32)
    heads = lambda z: z.reshape(z.shape[:-1] + (H_A, HEAD_A))
    p = f32(_dwconv_seq(pa, shift_w))
    r, k, v = p[..., :W_A], p[..., W_A:2 * W_A], p[..., 2 * W_A:3 * W_A]
    o = 3 * W_A
    wd = p[..., o:o + 2 * R_DECAY].reshape(B, T, 2, R_DECAY)
    o += 2 * R_DECAY
    ad = p[..., o:o + 2 * R_AAA].reshape(B, T, 2, R_AAA)
    o += 2 * R_AAA
    gd = p[..., o:o + R_GATE]
    w_log = -jax.nn.softplus(-(f32(w0) + jnp.einsum('btdr,drc->btdc', jnp.tanh(wd), f32(w2)))) - 0.5
    decay = jnp.exp(-jnp.exp(w_log))
    a_rate = jax.nn.sigmoid(f32(a0) + jnp.einsum('btdr,drc->btdc', ad, f32(a2)))
    g = jax.nn.sigmoid(gd) @ f32(g2)
    kk = heads(k * f32(k_k))
    kk = kk / jnp.maximum(jnp.sqrt(jnp.sum(kk * kk, -1, keepdims=True)), 1e-12)
    k_dir = k[:, :, None, :] * (1.0 + (a_rate - 1.0) * f32(k_a))
    return (heads(r), heads(v), kk, heads(k_dir), heads(decay), heads(a_rate), g)


def _wkv_scan(s0, streams, d, reverse):
    r, v, kk, k_dir, decay, a_rate, _ = streams
    b = kk * a_rate[:, :, d]
    xs = tuple(jnp.swapaxes(z, 0, 1) for z in (r, decay[:, :, d], k_dir[:, :, d], v, -kk, b))

    def step(S, inp):
        r_t, w_t, k_t, v_t, a_t, b_t = inp
        sa = jnp.einsum('bhvk,bhk->bhv', S, a_t)
        S = S * w_t[:, :, None, :] + sa[..., None] * b_t[:, :, None, :] + v_t[..., None] * k_t[:, :, None, :]
        return S, jnp.einsum('bhvk,bhk->bhv', S, r_t)

    s_last, ys = lax.scan(step, s0, xs, reverse=reverse)
    return s_last, jnp.swapaxes(ys, 0, 1)


def _rwkv_out(y, streams, r_k, lnx_g, lnx_b):
    r, v, _, k_dir, _, _, g = streams
    B, T = r.shape[:2]
    y = _layernorm(y, lnx_g.reshape(H_A, HEAD_A), lnx_b.reshape(H_A, HEAD_A), GN_EPS)
    k_bonus = jnp.mean(k_dir, axis=2)
    y = y + jnp.sum(r * k_bonus * r_k.astype(jnp.float32), -1, keepdims=True) * v
    return y.reshape(B, T, W_A) * g


def _rwkv_mixer(pa_lat, pa_ctx, shift_w, rw, ctx_out):
    w0, w2, a0, a2, g2, k_k, k_a, r_k, lnx_g, lnx_b = rw
    lat = _rwkv_streams(pa_lat, shift_w, w0, w2, a0, a2, g2, k_k, k_a)
    ctx = _rwkv_streams(pa_ctx, shift_w, w0, w2, a0, a2, g2, k_k, k_a)
    s0 = jnp.zeros((pa_lat.shape[0], H_A, HEAD_A, HEAD_A), jnp.float32)
    sc_f, yc_f = _wkv_scan(s0, ctx, 0, False)
    _, yl_f = _wkv_scan(sc_f, lat, 0, False)
    sc_b, yc_b = _wkv_scan(s0, ctx, 1, True)
    _, yl_b = _wkv_scan(sc_b, lat, 1, True)
    out_lat = _rwkv_out(yl_f + yl_b, lat, r_k, lnx_g, lnx_b).astype(pa_lat.dtype)
    if not ctx_out:
        return out_lat, None
    out_ctx = _rwkv_out(yc_f + yc_b, ctx, r_k, lnx_g, lnx_b).astype(pa_ctx.dtype)
    return out_lat, out_ctx


def _sgu(pb, ln_g, ln_b, w_s, b_s):
    B, T, _ = pb.shape
    z = jax.nn.gelu(pb, approximate=False)
    u, v = z[..., :W_B], z[..., W_B:]
    v = _layernorm(v, ln_g, ln_b, LN_EPS)
    v = v.reshape(B, T // CHUNK, CHUNK, H_B, W_B // H_B)
    s = jnp.einsum('hpq,bnqhc->bnphc', w_s, v) + b_s.T[None, None, :, :, None]
    return u * s.reshape(B, T, W_B)


def _pool_mixer(xc, pool_w, pool_scale, on_grid):
    B, T, _ = xc.shape
    xs = xc.reshape(B * (T // GRID_W), GRID_W, W_C) if on_grid else xc
    L = xs.shape[1]
    xf = xs.astype(jnp.float32)
    cs = jnp.concatenate([jnp.zeros_like(xf[:, :1]), jnp.cumsum(xf, axis=1)], axis=1)
    t = jnp.arange(L)
    outs = []
    for gi, win in enumerate(POOL_WINDOWS):
        lo = win // 2
        hi = win - 1 - lo
        end = jnp.clip(t + hi + 1, 0, L)
        start = jnp.clip(t - lo, 0, L)
        sl = slice(gi * G_C, (gi + 1) * G_C)
        mean = (cs[:, end, sl] - cs[:, start, sl]) / (end - start).astype(jnp.float32)[None, :, None]
        outs.append(mean - xf[:, :, sl])
    p = jnp.stack(outs, axis=-2).reshape(B, T, len(POOL_WINDOWS), G_C).astype(xc.dtype)
    y = jnp.einsum('btgc,gcd->btgd', p, pool_w).reshape(B, T, W_C)
    return y * pool_scale


def _short_conv_mixer(pd, conv_w, on_grid):
    b_gate, c_gate, xd = jnp.split(pd, 3, axis=-1)
    z = c_gate * xd
    z = _dwconv_grid(z, conv_w, 'row') if on_grid else _dwconv_seq(z, conv_w)
    return b_gate * z


def _even_mixer(a_lat, a_ctx, w_in, w_out, shift_w, rw, sgu_p, ctx_out):
    p_lat = a_lat @ w_in
    p_ctx = a_ctx @ (w_in if ctx_out else w_in[:, :IN_A])
    ya_lat, ya_ctx = _rwkv_mixer(p_lat[..., :IN_A], p_ctx[..., :IN_A], shift_w, rw, ctx_out)
    y_lat = jnp.concatenate([ya_lat, _sgu(p_lat[..., IN_A:], *sgu_p)], axis=-1) @ w_out
    if not ctx_out:
        return y_lat, None
    y_ctx = jnp.concatenate([ya_ctx, _sgu(p_ctx[..., IN_A:], *sgu_p)], axis=-1) @ w_out
    return y_lat, y_ctx


def _odd_mixer(a, w_in, w_out, pool_w, pool_scale, sconv_w, on_grid):
    p = a @ w_in
    yc = _pool_mixer(p[..., :W_C], pool_w, pool_scale, on_grid)
    yd = _short_conv_mixer(p[..., W_C:], sconv_w, on_grid)
    return jnp.concatenate([yc, yd], axis=-1) @ w_out


def _conv_ffn(a, w_up, conv_w, w_down, on_grid):
    up = a @ w_up
    gate, val = up[..., :D_FF], up[..., D_FF:]
    gate = _dwconv_grid(gate, conv_w, 'col') if on_grid else _dwconv_seq(gate, conv_w)
    return (jax.nn.silu(gate) * val) @ w_down


def setup_inputs(seed: int = 0) -> dict:
    key = jax.random.key(seed)
    k = jax.random.split(key, 40)
    D = D_MODEL
    nrm = lambda kk, shape, std: jax.random.normal(kk, shape, jnp.float32) * std
    mu_prev = jax.random.uniform(k[36], (N_EVEN, IN_A), jnp.float32, 0.0, 0.5)
    mu_next = jax.random.uniform(k[37], (N_EVEN, IN_A), jnp.float32, 0.0, 0.5)
    ev_shift = jnp.stack([mu_prev, 1.0 - mu_prev - mu_next, mu_next], axis=1)
    return {
        "x": nrm(k[0], (BATCH, SEQ, D), 1.0),
        "c": nrm(k[1], (BATCH, D), 1.0),
        "ctx": nrm(k[2], (BATCH, CTX_LEN, D), 1.0),
        "c_ctx": nrm(k[3], (D,), 1.0),
        "ada_w": nrm(k[4], (DEPTH, D, 6 * D), 0.5 * D ** -0.5),
        "ada_b": nrm(k[5], (DEPTH, 6 * D), 0.01),
        "ln_g": 1.0 + nrm(k[6], (DEPTH, 2, D), 0.02),
        "ln_b": nrm(k[7], (DEPTH, 2, D), 0.02),
        "ffn_w_up": nrm(k[8], (DEPTH, D, 2 * D_FF), D ** -0.5),
        "ffn_conv": nrm(k[9], (DEPTH, CONV_K, D_FF), CONV_K ** -0.5),
        "ffn_w_down": nrm(k[10], (DEPTH, D_FF, D), BETA * D_FF ** -0.5),
        "ev_w_in": nrm(k[11], (N_EVEN, D, IN_EVEN), D ** -0.5),
        "ev_w_out": nrm(k[12], (N_EVEN, W_A + W_B, D), BETA * (W_A + W_B) ** -0.5),
        "ev_shift": ev_shift,
        "rwkv_w0": jax.random.uniform(k[13], (N_EVEN, 2, W_A), jnp.float32, -6.0, 1.0),
        "rwkv_w2": nrm(k[14], (N_EVEN, 2, R_DECAY, W_A), 0.1 * R_DECAY ** -0.5),
        "rwkv_a0": nrm(k[15], (N_EVEN, 2, W_A), 0.1),
        "rwkv_a2": nrm(k[16], (N_EVEN, 2, R_AAA, W_A), 0.1 * R_AAA ** -0.5),
        "rwkv_g2": nrm(k[17], (N_EVEN, R_GATE, W_A), R_GATE ** -0.5),
        "rwkv_kk": 0.85 + nrm(k[18], (N_EVEN, W_A), 0.05),
        "rwkv_ka": 1.0 + nrm(k[19], (N_EVEN, W_A), 0.05),
        "rwkv_rk": nrm(k[20], (N_EVEN, H_A, HEAD_A), 0.1),
        "rwkv_lnx_g": 1.0 + nrm(k[21], (N_EVEN, W_A), 0.02),
        "rwkv_lnx_b": nrm(k[22], (N_EVEN, W_A), 0.02),
        "sgu_ln_g": 1.0 + nrm(k[23], (N_EVEN, W_B), 0.02),
        "sgu_ln_b": nrm(k[24], (N_EVEN, W_B), 0.02),
        "sgu_w": nrm(k[25], (N_EVEN, H_B, CHUNK, CHUNK), 0.5 * CHUNK ** -0.5),
        "sgu_b": 1.0 + nrm(k[26], (N_EVEN, H_B, CHUNK), 0.1),
        "od_w_in": nrm(k[27], (N_ODD, D, IN_ODD), D ** -0.5),
        "od_w_out": nrm(k[28], (N_ODD, W_C + W_D, D), BETA * (W_C + W_D) ** -0.5),
        "pool_w": nrm(k[29], (N_ODD, len(POOL_WINDOWS), G_C, G_C), G_C ** -0.5),
        "pool_scale": 1.0 + nrm(k[30], (N_ODD, W_C), 0.1),
        "sconv_w": nrm(k[31], (N_ODD, CONV_K, W_D), CONV_K ** -0.5),
    }


def reference(x, c, ctx, c_ctx, ada_w, ada_b, ln_g, ln_b, ffn_w_up, ffn_conv, ffn_w_down,
              ev_w_in, ev_w_out, ev_shift, rwkv_w0, rwkv_w2, rwkv_a0, rwkv_a2, rwkv_g2,
              rwkv_kk, rwkv_ka, rwkv_rk, rwkv_lnx_g, rwkv_lnx_b, sgu_ln_g, sgu_ln_b, sgu_w, sgu_b,
              od_w_in, od_w_out, pool_w, pool_scale, sconv_w):
    last_cross = 2 * ((DEPTH - 1) // 2)
    h_lat, h_ctx = x, ctx
    for i in range(DEPTH):
        ctx_in = i <= last_cross
        ctx_out = i < last_cross
        j = i // 2
        sh1, sc1, g1, sh2, sc2, g2 = jnp.split(_modulation(c, ada_w[i], ada_b[i])[:, None, :], 6, axis=-1)
        a_lat = h_lat * (1.0 + sc1) + sh1
        if ctx_in:
            ch1, cc1, cg1, ch2, cc2, cg2 = jnp.split(_modulation(c_ctx, ada_w[i], ada_b[i])[None, None, :], 6, axis=-1)
            a_ctx = h_ctx * (1.0 + cc1) + ch1
        if i % 2 == 0:
            rw = (rwkv_w0[j], rwkv_w2[j], rwkv_a0[j], rwkv_a2[j], rwkv_g2[j], rwkv_kk[j], rwkv_ka[j],
                  rwkv_rk[j], rwkv_lnx_g[j], rwkv_lnx_b[j])
            sgu_p = (sgu_ln_g[j], sgu_ln_b[j], sgu_w[j], sgu_b[j])
            mix_lat, mix_ctx = _even_mixer(a_lat, a_ctx, ev_w_in[j], ev_w_out[j], ev_shift[j], rw, sgu_p, ctx_out)
        else:
            mix_lat = _odd_mixer(a_lat, od_w_in[j], od_w_out[j], pool_w[j], pool_scale[j], sconv_w[j], True)
            if ctx_out:
                mix_ctx = _odd_mixer(a_ctx, od_w_in[j], od_w_out[j], pool_w[j], pool_scale[j], sconv_w[j], False)
        h_lat = _layernorm(ALPHA * h_lat + g1 * mix_lat, ln_g[i, 0], ln_b[i, 0], LN_EPS)
        f_lat = _conv_ffn(h_lat * (1.0 + sc2) + sh2, ffn_w_up[i], ffn_conv[i], ffn_w_down[i], True)
        h_lat = _layernorm(ALPHA * h_lat + g2 * f_lat, ln_g[i, 1], ln_b[i, 1], LN_EPS)
        if ctx_out:
            h_ctx = _layernorm(ALPHA * h_ctx + cg1 * mix_ctx, ln_g[i, 0], ln_b[i, 0], LN_EPS)
            f_ctx = _conv_ffn(h_ctx * (1.0 + cc2) + ch2, ffn_w_up[i], ffn_conv[i], ffn_w_down[i], False)
            h_ctx = _layernorm(ALPHA * h_ctx + cg2 * f_ctx, ln_g[i, 1], ln_b[i, 1], LN_EPS)
    return h_lat
```

```python
import functools
import math

import jax
import jax.numpy as jnp
from jax import lax
from jax.experimental import pallas as pl
from jax.experimental.pallas import tpu as pltpu

GRID_W = 64
POOL_WINDOWS = (2, 4, 8, 16)
LN_EPS = 1e-6
GN_EPS = 64e-5
HEAD = 64
SCAN_CHUNK = 64
LANES = 128
SUBLANES = 8
VMEM_LIMIT_BYTES = 56 * 1024 * 1024

F32 = jnp.float32
BF16 = jnp.bfloat16
HIGHEST = lax.Precision.HIGHEST


def _round_up(n, m):
    return (n + m - 1) // m * m


def _params(*sem):
    return pltpu.CompilerParams(dimension_semantics=sem, vmem_limit_bytes=VMEM_LIMIT_BYTES)


def _sigmoid(z):
    return 1.0 / (1.0 + jnp.exp(-z))


def _softplus(z):
    return jnp.maximum(z, 0.0) + jnp.log1p(jnp.exp(-jnp.abs(z)))


def _mod_kernel(c_ref, w_ref, b_ref, o_ref):
    c = c_ref[...]
    s = (c * _sigmoid(c)).astype(BF16)
    o_ref[...] = jnp.dot(s, w_ref[...].astype(BF16), preferred_element_type=F32) + b_ref[...]


def _modulation_all(cc, ada_w, ada_b):
    depth, d, n = ada_w.shape
    r = cc.shape[0]
    tn = min(512, n)
    return pl.pallas_call(
        _mod_kernel,
        grid=(depth, n // tn),
        in_specs=[pl.BlockSpec((r, d), lambda l, j: (0, 0)),
                  pl.BlockSpec((None, d, tn), lambda l, j: (l, 0, j)),
                  pl.BlockSpec((None, 1, tn), lambda l, j: (l, 0, j))],
        out_specs=pl.BlockSpec((None, r, tn), lambda l, j: (l, 0, j)),
        out_shape=jax.ShapeDtypeStruct((depth, r, n), F32),
        compiler_params=_params("parallel", "parallel"),
        name="adaln_modulation",
    )(cc, ada_w, ada_b.reshape(depth, 1, n))


def _vec_spec(d, layer, part, row):
    return pl.BlockSpec((None, None, 1, d), lambda b, t: (layer, row(b), 0, part))


def _modulate_kernel(x_ref, sc_ref, sh_ref, a_ref):
    a_ref[...] = (x_ref[...] * (1.0 + sc_ref[...]) + sh_ref[...]).astype(a_ref.dtype)


def _modulate(x, mod4, layer, row, sc_part, sh_part):
    b, t, d = x.shape
    tt = min(256, t)
    return pl.pallas_call(
        _modulate_kernel,
        grid=(b, t // tt),
        in_specs=[pl.BlockSpec((None, tt, d), lambda bi, ti: (bi, ti, 0)),
                  _vec_spec(d, layer, sc_part, row), _vec_spec(d, layer, sh_part, row)],
        out_specs=pl.BlockSpec((None, tt, d), lambda bi, ti: (bi, ti, 0)),
        out_shape=jax.ShapeDtypeStruct((b, t, d), BF16),
        compiler_params=_params("parallel", "parallel"),
        name="modulate",
    )(x, mod4, mod4)


def _ln_kernel(*refs, alpha, with_next):
    if with_next:
        h_ref, f_ref, gate_ref, g_ref, b_ref, sc_ref, sh_ref, hn_ref, a_ref = refs
    else:
        h_ref, f_ref, gate_ref, g_ref, b_ref, hn_ref = refs
    z = alpha * h_ref[...] + gate_ref[...] * f_ref[...]
    mu = jnp.mean(z, axis=-1, keepdims=True)
    zc = z - mu
    var = jnp.mean(zc * zc, axis=-1, keepdims=True)
    y = zc * lax.rsqrt(var + LN_EPS) * g_ref[...] + b_ref[...]
    hn_ref[...] = y
    if with_next:
        a_ref[...] = (y * (1.0 + sc_ref[...]) + sh_ref[...]).astype(a_ref.dtype)


def _ln_residual(h, f, mod4, lng4, lnb4, row, alpha, layer, gate_part, ln_idx, nxt):
    b, t, d = h.shape
    tt = min(256, t)
    tok = pl.BlockSpec((None, tt, d), lambda bi, ti: (bi, ti, 0))
    ln_spec = pl.BlockSpec((None, None, 1, d), lambda bi, ti: (layer, ln_idx, 0, 0))
    in_specs = [tok, tok, _vec_spec(d, layer, gate_part, row), ln_spec, ln_spec]
    args = [h, f.reshape(b, t, d), mod4, lng4, lnb4]
    out_specs = [tok]
    out_shape = [jax.ShapeDtypeStruct((b, t, d), F32)]
    if nxt is not None:
        in_specs += [_vec_spec(d, nxt[0], nxt[1], row), _vec_spec(d, nxt[0], nxt[2], row)]
        args += [mod4, mod4]
        out_specs.append(tok)
        out_shape.append(jax.ShapeDtypeStruct((b, t, d), BF16))
    out = pl.pallas_call(
        functools.partial(_ln_kernel, alpha=alpha, with_next=nxt is not None),
        grid=(b, t // tt),
        in_specs=in_specs, out_specs=out_specs, out_shape=out_shape,
        compiler_params=_params("parallel", "parallel"),
        name="deepnorm_residual",
    )(*args)
    return (out[0], out[1]) if nxt is not None else (out[0], None)


def _mm_kernel(a_ref, w_ref, o_ref):
    o_ref[...] = jnp.dot(a_ref[...], w_ref[...], preferred_element_type=F32).astype(o_ref.dtype)


def _matmul(a, w, out_dtype=F32):
    lead = a.shape[:-1]
    k = a.shape[-1]
    a2 = a.reshape(-1, k)
    m = a2.shape[0]
    n = w.shape[1]
    tm = min(512, m)
    tn = min(1024, n)
    assert m % tm == 0 and n % tn == 0, (m, n)
    out = pl.pallas_call(
        _mm_kernel,
        grid=(n // tn, m // tm),
        in_specs=[pl.BlockSpec((tm, k), lambda j, i: (i, 0)),
                  pl.BlockSpec((k, tn), lambda j, i: (0, j))],
        out_specs=pl.BlockSpec((tm, tn), lambda j, i: (i, j)),
        out_shape=jax.ShapeDtypeStruct((m, n), out_dtype),
        compiler_params=_params("parallel", "parallel"),
        name="projection",
    )(a2, w)
    return out.reshape(lead + (n,))


def _head_sum(x):
    li = lax.broadcasted_iota(jnp.int32, (LANES, LANES), 0) // HEAD
    lj = lax.broadcasted_iota(jnp.int32, (LANES, LANES), 1) // HEAD
    ones = (li == lj).astype(F32)
    return jnp.dot(x, ones, precision=HIGHEST, preferred_element_type=F32)


def _prep_kernel(p_ref, pp_ref, pn_ref, shw_ref, w0_ref, w2_ref, a0_ref, a2_ref, g2_ref,
                 kkw_ref, ka_ref, rk_ref,
                 r_o, v_o, kk_o, lwf_o, lwb_o, kf_o, kb_o, bf_o, bb_o, g_o, bv_o,
                 *, width, rp, rgp):
    i = pl.program_id(1)
    last = pl.num_programs(1) - 1
    tt = p_ref.shape[0]
    row = lax.broadcasted_iota(jnp.int32, (tt, 1), 0)

    def shifted(c0, c1):
        x = p_ref[:, c0:c1]
        prev_row = jnp.where(i > 0, pp_ref[SUBLANES - 1:SUBLANES, c0:c1], 0.0)
        next_row = jnp.where(i < last, pn_ref[0:1, c0:c1], 0.0)
        xp = jnp.where(row == 0, prev_row, pltpu.roll(x, 1, 0))
        xn = jnp.where(row == tt - 1, next_row, pltpu.roll(x, tt - 1, 0))
        return shw_ref[0:1, c0:c1] * xp + shw_ref[1:2, c0:c1] * x + shw_ref[2:3, c0:c1] * xn

    hp_count = width // LANES
    o = 3 * width
    kkw = kkw_ref[...]
    ka = ka_ref[...]
    rk = rk_ref[...]
    g = jnp.dot(_sigmoid(shifted(o + 4 * rp, o + 4 * rp + rgp)).astype(BF16), g2_ref[...],
                preferred_element_type=F32)
    g_o[...] = g
    rates = []
    for d in range(2):
        wd = shifted(o + d * rp, o + (d + 1) * rp)
        ad = shifted(o + (2 + d) * rp, o + (3 + d) * rp)
        wl = w0_ref[d:d + 1, :] + jnp.dot(jnp.tanh(wd).astype(BF16), w2_ref[d], preferred_element_type=F32)
        w_log = -_softplus(-wl) - 0.5
        lw = -jnp.exp(w_log)
        ar = _sigmoid(a0_ref[d:d + 1, :] + jnp.dot(ad.astype(BF16), a2_ref[d], preferred_element_type=F32))
        rates.append(ar)
        lw_o = lwf_o if d == 0 else lwb_o
        for hp in range(hp_count):
            lw_o[hp] = lw[:, hp * LANES:(hp + 1) * LANES]
    for hp in range(hp_count):
        c0, c1 = hp * LANES, (hp + 1) * LANES
        r = shifted(c0, c1)
        k = shifted(width + c0, width + c1)
        v = shifted(2 * width + c0, 2 * width + c1)
        kkr = k * kkw[:, c0:c1]
        kk = kkr / jnp.maximum(jnp.sqrt(_head_sum(kkr * kkr)), 1e-12)
        kd0 = k * (1.0 + (rates[0][:, c0:c1] - 1.0) * ka[:, c0:c1])
        kd1 = k * (1.0 + (rates[1][:, c0:c1] - 1.0) * ka[:, c0:c1])
        r_o[hp] = r
        v_o[hp] = v
        kk_o[hp] = kk
        kf_o[hp] = kd0
        kb_o[hp] = kd1
        bf_o[hp] = kk * rates[0][:, c0:c1]
        bb_o[hp] = kk * rates[1][:, c0:c1]
        bonus = _head_sum(r * (0.5 * (kd0 + kd1)) * rk[:, c0:c1])
        bv_o[:, c0:c1] = bonus * v


def _rwkv_prep(pa, lay, rw):
    b, t, na = pa.shape
    width, rp, rgp = lay["width"], lay["rp"], lay["rgp"]
    hp = width // LANES
    tt = min(128, t)
    nb = t // SUBLANES
    tok = lambda w: pl.BlockSpec((None, tt, w), lambda bi, ti: (bi, ti, 0))
    halo_prev = pl.BlockSpec((None, SUBLANES, na),
                             lambda bi, ti: (bi, jnp.maximum(ti * (tt // SUBLANES) - 1, 0), 0))
    halo_next = pl.BlockSpec((None, SUBLANES, na),
                             lambda bi, ti: (bi, jnp.minimum((ti + 1) * (tt // SUBLANES), nb - 1), 0))
    full = lambda a: pl.BlockSpec(a.shape, lambda bi, ti: (0,) * a.ndim)
    pair = pl.BlockSpec((None, hp, tt, LANES), lambda bi, ti: (bi, 0, ti, 0))
    pair_shape = jax.ShapeDtypeStruct((b, hp, t, LANES), F32)
    consts = [rw["shift"], rw["w0"], rw["w2"], rw["a0"], rw["a2"], rw["g2"], rw["kk"], rw["ka"], rw["rk"]]
    outs = pl.pallas_call(
        functools.partial(_prep_kernel, width=width, rp=rp, rgp=rgp),
        grid=(b, t // tt),
        in_specs=[tok(na), halo_prev, halo_next] + [full(a) for a in consts],
        out_specs=[pair] * 9 + [tok(width), tok(width)],
        out_shape=[pair_shape] * 9 + [jax.ShapeDtypeStruct((b, t, width), F32)] * 2,
        compiler_params=_params("parallel", "parallel"),
        name="rwkv_streams",
    )(pa, pa, pa, *consts)
    names = ("r", "v", "kk", "lwf", "lwb", "kf", "kb", "bf", "bb", "g", "bv")
    return dict(zip(names, outs))


def _dot_nt(a, b):
    return lax.dot_general(a, b, (((1,), (1,)), ((), ())), precision=HIGHEST, preferred_element_type=F32)


def _dot_tn(a, b):
    return lax.dot_general(a, b, (((0,), (0,)), ((), ())), precision=HIGHEST, preferred_element_type=F32)


def _dot(a, b):
    return jnp.dot(a, b, precision=HIGHEST, preferred_element_type=F32)


def _wkv_kernel(r_ref, v_ref, kk_ref, lw_ref, k_ref, b_ref, s0_ref, y_ref, s_ref, *, reverse):
    c = pl.program_id(1)

    @pl.when(c == 0)
    def _():
        s_ref[...] = s0_ref[...]

    hp_count, ch, _ = r_ref.shape
    ti = lax.broadcasted_iota(jnp.int32, (ch, ch), 0)
    tj = lax.broadcasted_iota(jnp.int32, (ch, ch), 1)
    upto = ((tj >= ti) if reverse else (tj <= ti)).astype(F32)
    eye = (ti == tj).astype(F32)
    gi = lax.broadcasted_iota(jnp.int32, (2 * ch, 2 * ch), 0)
    gj = lax.broadcasted_iota(jnp.int32, (2 * ch, 2 * ch), 1)
    it = jnp.where(gi >= ch, gi - ch, gi)
    jt = jnp.where(gj >= ch, gj - ch, gj)
    earlier = (jt > it) if reverse else (jt < it)
    gmask = earlier | ((jt == it) & (gi >= ch))
    doublings = int(math.log2(ch)) - 1

    def pair_body(hp, carry):
        lw = lw_ref[hp]
        cl = _dot(upto, lw)
        tot = jnp.sum(lw, axis=0, keepdims=True)
        e_in = jnp.exp(cl)
        e_out = jnp.exp(-cl)
        e_end = jnp.exp(tot - cl)
        a_t = -kk_ref[hp] * jnp.exp(cl - lw)
        r_t = r_ref[hp] * e_in
        b_t = b_ref[hp] * e_out
        k_t = k_ref[hp] * e_out
        b_e = b_ref[hp] * e_end
        k_e = k_ref[hp] * e_end
        w_c = jnp.exp(tot)
        v = v_ref[hp]
        for j in range(LANES // HEAD):
            sl = slice(j * HEAD, (j + 1) * HEAD)
            ar = jnp.concatenate([a_t[:, sl], r_t[:, sl]], axis=0)
            bk = jnp.concatenate([b_t[:, sl], k_t[:, sl]], axis=0)
            g = jnp.where(gmask, _dot_nt(ar, bk), 0.0)
            low = g[:ch, :ch]
            inv = eye + low
            pw = low
            for _ in range(doublings):
                pw = _dot(pw, pw)
                inv = inv + _dot(inv, pw)
            s = s_ref[2 * hp + j]
            ars = _dot_nt(ar, s)
            vj = v[:, sl]
            u = _dot(inv, ars[:ch] + _dot(g[:ch, ch:], vj))
            uv = jnp.concatenate([u, vj], axis=0)
            y_ref[hp, :, sl] = ars[ch:] + _dot(g[ch:, :], uv)
            bke = jnp.concatenate([b_e[:, sl], k_e[:, sl]], axis=0)
            s_ref[2 * hp + j] = s * w_c[:, sl] + _dot_tn(uv, bke)
        return carry

    lax.fori_loop(0, hp_count, pair_body, 0)


def _wkv_scan(st, d, s0, reverse):
    r = st["r"]
    b, hp, t, _ = r.shape
    ch = min(SCAN_CHUNK, t)
    nc = t // ch
    heads = hp * (LANES // HEAD)
    cidx = (lambda c: nc - 1 - c) if reverse else (lambda c: c)
    blk = pl.BlockSpec((None, hp, ch, LANES), lambda bi, c: (bi, 0, cidx(c), 0))
    st_spec = pl.BlockSpec((None, heads, HEAD, HEAD), lambda bi, c: (bi, 0, 0, 0))
    lw, k, bb = (st["lwf"], st["kf"], st["bf"]) if d == 0 else (st["lwb"], st["kb"], st["bb"])
    y, s_last = pl.pallas_call(
        functools.partial(_wkv_kernel, reverse=reverse),
        grid=(b, nc),
        in_specs=[blk] * 6 + [st_spec],
        out_specs=[blk, st_spec],
        out_shape=[jax.ShapeDtypeStruct((b, hp, t, LANES), F32),
                   jax.ShapeDtypeStruct((b, heads, HEAD, HEAD), F32)],
        compiler_params=_params("parallel", "arbitrary"),
        name="wkv_scan",
    )(r, st["v"], st["kk"], lw, k, bb, s0)
    return y, s_last


def _rwkv_out_kernel(yf_ref, yb_ref, bv_ref, g_ref, lng_ref, lnb_ref, o_ref):
    hp_count = yf_ref.shape[0]
    for hp in range(hp_count):
        c0, c1 = hp * LANES, (hp + 1) * LANES
        y = yf_ref[hp] + yb_ref[hp]
        mu = _head_sum(y) * (1.0 / HEAD)
        yc = y - mu
        var = _head_sum(yc * yc) * (1.0 / HEAD)
        yn = yc * lax.rsqrt(var + GN_EPS) * lng_ref[:, c0:c1] + lnb_ref[:, c0:c1]
        o_ref[:, c0:c1] = ((yn + bv_ref[:, c0:c1]) * g_ref[:, c0:c1]).astype(o_ref.dtype)


def _rwkv_out(yf, yb, st, lnx_g, lnx_b):
    b, hp, t, _ = yf.shape
    width = hp * LANES
    tt = min(256, t)
    pair = pl.BlockSpec((None, hp, tt, LANES), lambda bi, ti: (bi, 0, ti, 0))
    tok = pl.BlockSpec((None, tt, width), lambda bi, ti: (bi, ti, 0))
    vec = pl.BlockSpec((1, width), lambda bi, ti: (0, 0))
    return pl.pallas_call(
        _rwkv_out_kernel,
        grid=(b, t // tt),
        in_specs=[pair, pair, tok, tok, vec, vec],
        out_specs=tok,
        out_shape=jax.ShapeDtypeStruct((b, t, width), BF16),
        compiler_params=_params("parallel", "parallel"),
        name="rwkv_out",
    )(yf, yb, st["bv"], st["g"], lnx_g, lnx_b)


def _gelu(z):
    return 0.5 * z * (1.0 + lax.erf(z * (2.0 ** -0.5)))


def _sgu_kernel(u_ref, v_ref, lng_ref, lnb_ref, ws_ref, bs_ref, o_ref):
    heads = ws_ref.shape[0]
    width = u_ref.shape[1]
    hw = width // heads
    v = _gelu(v_ref[...])
    mu = jnp.mean(v, axis=-1, keepdims=True)
    vc = v - mu
    var = jnp.mean(vc * vc, axis=-1, keepdims=True)
    vn = (vc * lax.rsqrt(var + LN_EPS) * lng_ref[...] + lnb_ref[...]).astype(BF16)
    for h in range(heads):
        c0, c1 = h * hw, (h + 1) * hw
        s = jnp.dot(ws_ref[h], vn[:, c0:c1], preferred_element_type=F32) + bs_ref[:, h:h + 1]
        o_ref[:, c0:c1] = (_gelu(u_ref[:, c0:c1]) * s).astype(o_ref.dtype)


def _sgu(pb, ln_g, ln_b, ws, bs_t):
    b, t, w2 = pb.shape
    width = w2 // 2
    heads, chunk, _ = ws.shape
    tok_u = pl.BlockSpec((None, chunk, width), lambda bi, ti: (bi, ti, 0))
    tok_v = pl.BlockSpec((None, chunk, width), lambda bi, ti: (bi, ti, 1))
    full = lambda a: pl.BlockSpec(a.shape, lambda bi, ti: (0,) * a.ndim)
    return pl.pallas_call(
        _sgu_kernel,
        grid=(b, t // chunk),
        in_specs=[tok_u, tok_v, full(ln_g), full(ln_b), full(ws), full(bs_t)],
        out_specs=tok_u,
        out_shape=jax.ShapeDtypeStruct((b, t, width), BF16),
        compiler_params=_params("parallel", "parallel"),
        name="spatial_gating",
    )(pb, pb, ln_g, ln_b, ws, bs_t)


def _odd_kernel(xc_ref, bg_ref, cg_ref, xd_ref, pw_ref, ps_ref, cw_ref, o_ref, *, seg):
    tt, width = xc_ref.shape
    groups = pw_ref.shape[0]
    gc = width // groups
    ti = lax.broadcasted_iota(jnp.int32, (tt, tt), 0)
    tj = lax.broadcasted_iota(jnp.int32, (tt, tt), 1)
    same_seg = (ti // seg) == (tj // seg)
    pos = lax.broadcasted_iota(jnp.int32, (tt, 1), 0) % seg
    for gi in range(groups):
        win = POOL_WINDOWS[gi]
        lo = win // 2
        hi = win - 1 - lo
        band = (same_seg & (tj >= ti - lo) & (tj <= ti + hi)).astype(F32)
        count = (jnp.minimum(pos + hi, seg - 1) - jnp.maximum(pos - lo, 0) + 1).astype(F32)
        c0, c1 = gi * gc, (gi + 1) * gc
        x = xc_ref[:, c0:c1]
        mean = jnp.dot(band, x, precision=HIGHEST, preferred_element_type=F32) / count
        p = (mean - x).astype(BF16)
        y = jnp.dot(p, pw_ref[gi], preferred_element_type=F32) * ps_ref[:, c0:c1]
        o_ref[:, c0:c1] = y.astype(o_ref.dtype)
    z = cg_ref[...] * xd_ref[...]
    zp = jnp.where(pos == 0, 0.0, pltpu.roll(z, 1, 0))
    zn = jnp.where(pos == seg - 1, 0.0, pltpu.roll(z, tt - 1, 0))
    conv = cw_ref[0:1, :] * zp + cw_ref[1:2, :] * z + cw_ref[2:3, :] * zn
    o_ref[:, width:] = (bg_ref[...] * conv).astype(o_ref.dtype)


def _odd_mix(p, pool_w, pool_scale, sconv_w, seg):
    b, t, w4 = p.shape
    width = w4 // 4
    tt = min(max(256, seg), t)
    assert tt % seg == 0 and t % tt == 0
    col = lambda ci: pl.BlockSpec((None, tt, width), lambda bi, ti: (bi, ti, ci))
    full = lambda a: pl.BlockSpec(a.shape, lambda bi, ti: (0,) * a.ndim)
    return pl.pallas_call(
        functools.partial(_odd_kernel, seg=seg),
        grid=(b, t // tt),
        in_specs=[col(0), col(1), col(2), col(3), full(pool_w), full(pool_scale), full(sconv_w)],
        out_specs=pl.BlockSpec((None, tt, 2 * width), lambda bi, ti: (bi, ti, 0)),
        out_shape=jax.ShapeDtypeStruct((b, t, 2 * width), BF16),
        compiler_params=_params("parallel", "parallel"),
        name="pool_shortconv",
    )(p, p, p, p, pool_w, pool_scale, sconv_w)


def _ffn_mid_kernel(g_ref, gp_ref, gn_ref, val_ref, cw_ref, o_ref, *, shift):
    i = pl.program_id(1)
    last = pl.num_programs(1) - 1
    tt = g_ref.shape[0]
    hs = gp_ref.shape[0]
    g = g_ref[...]
    if shift % SUBLANES == 0:
        prev_blk = jnp.where(i > 0, gp_ref[...], 0.0)
        next_blk = jnp.where(i < last, gn_ref[...], 0.0)
        if tt > shift:
            gp = jnp.concatenate([prev_blk, g[:tt - shift]], axis=0)
            gn = jnp.concatenate([g[shift:], next_blk], axis=0)
        else:
            gp, gn = prev_blk, next_blk
    else:
        row = lax.broadcasted_iota(jnp.int32, (tt, 1), 0)
        prev_row = jnp.where(i > 0, gp_ref[hs - 1:hs, :], 0.0)
        next_row = jnp.where(i < last, gn_ref[0:1, :], 0.0)
        gp = jnp.where(row == 0, prev_row, pltpu.roll(g, 1, 0))
        gn = jnp.where(row == tt - 1, next_row, pltpu.roll(g, tt - 1, 0))
    conv = cw_ref[0:1, :] * gp + cw_ref[1:2, :] * g + cw_ref[2:3, :] * gn
    o_ref[...] = (conv * _sigmoid(conv) * val_ref[...]).astype(o_ref.dtype)


def _ffn_mid(up, conv_w, shift):
    b, t, f2 = up.shape
    f = f2 // 2
    tt = min(256, t)
    tc = f
    for cand in (1280, 1024, 512, 256, 128):
        if f % cand == 0:
            tc = cand
            break
    ncol = f // tc
    hs = shift if shift % SUBLANES == 0 else SUBLANES
    assert shift == 1 or (shift % SUBLANES == 0 and tt % shift == 0)
    nh = t // hs
    main = pl.BlockSpec((None, tt, tc), lambda bi, ti, ci: (bi, ti, ci))
    prev = pl.BlockSpec((None, hs, tc), lambda bi, ti, ci: (bi, jnp.maximum(ti * (tt // hs) - 1, 0), ci))
    nxt = pl.BlockSpec((None, hs, tc), lambda bi, ti, ci: (bi, jnp.minimum((ti + 1) * (tt // hs), nh - 1), ci))
    val = pl.BlockSpec((None, tt, tc), lambda bi, ti, ci: (bi, ti, ci + ncol))
    cw = pl.BlockSpec((3, tc), lambda bi, ti, ci: (0, ci))
    return pl.pallas_call(
        functools.partial(_ffn_mid_kernel, shift=shift),
        grid=(b, t // tt, ncol),
        in_specs=[main, prev, nxt, val, cw],
        out_specs=main,
        out_shape=jax.ShapeDtypeStruct((b, t, f), BF16),
        compiler_params=_params("parallel", "parallel", "parallel"),
        name="convffn_gate",
    )(up, up, up, up, conv_w)


def _pad_cols(w, n):
    return jnp.pad(w, ((0, 0), (0, n - w.shape[1])))


def _even_layout(width, r_decay, r_aaa, r_gate):
    rp = _round_up(max(r_decay, r_aaa), LANES)
    rgp = _round_up(r_gate, LANES)
    na = _round_up(3 * width + 4 * rp + rgp, 1024)
    return dict(width=width, rp=rp, rgp=rgp, na=na)


def _permute_rwkv_cols(w, lay, r_decay, r_aaa, r_gate):
    width, rp, rgp, na = lay["width"], lay["rp"], lay["rgp"], lay["na"]
    o = 3 * width
    parts = [w[:, :o]]
    for seg in (r_decay, r_decay, r_aaa, r_aaa):
        parts.append(_pad_cols(w[:, o:o + seg], rp))
        o += seg
    parts.append(_pad_cols(w[:, o:o + r_gate], rgp))
    return _pad_cols(jnp.concatenate(parts, axis=1), na)


def _pad_rows(w, n):
    pad = [(0, 0)] * w.ndim
    pad[-2] = (0, n - w.shape[-2])
    return jnp.pad(w, pad)


def kernel(x, c, ctx, c_ctx, ada_w, ada_b, ln_g, ln_b, ffn_w_up, ffn_conv, ffn_w_down, ev_w_in, ev_w_out, ev_shift, rwkv_w0, rwkv_w2, rwkv_a0, rwkv_a2, rwkv_g2, rwkv_kk, rwkv_ka, rwkv_rk, rwkv_lnx_g, rwkv_lnx_b, sgu_ln_g, sgu_ln_b, sgu_w, sgu_b, od_w_in, od_w_out, pool_w, pool_scale, sconv_w):
    batch, seq, d = x.shape
    depth = ada_w.shape[0]
    alpha = (2 * depth) ** 0.25
    last_cross = 2 * ((depth - 1) // 2)
    heads = rwkv_rk.shape[1]
    width_a = heads * HEAD
    r_decay, r_aaa, r_gate = rwkv_w2.shape[2], rwkv_a2.shape[2], rwkv_g2.shape[1]
    in_a = 3 * width_a + 2 * r_decay + 2 * r_aaa + r_gate
    lay = _even_layout(width_a, r_decay, r_aaa, r_gate)

    rows = _round_up(batch + 1, SUBLANES)
    cc = jnp.zeros((rows, d), F32).at[:batch].set(c).at[batch].set(c_ctx)
    mod4 = _modulation_all(cc, ada_w, ada_b).reshape(depth, rows, 1, 6 * d)
    lng4 = ln_g.reshape(depth, 2, 1, d)
    lnb4 = ln_b.reshape(depth, 2, 1, d)
    lat_row = lambda b: b
    ctx_row = lambda b: batch

    h_lat, h_ctx = x, ctx
    a_lat = _modulate(x, mod4, 0, lat_row, 1, 0)
    a_ctx = _modulate(ctx, mod4, 0, ctx_row, 1, 0)
    for i in range(depth):
        ctx_in = i <= last_cross
        ctx_out = i < last_cross
        j = i // 2
        mix_ctx = None
        if i % 2 == 0:
            w_in = ev_w_in[j]
            w_a = _permute_rwkv_cols(w_in[:, :in_a], lay, r_decay, r_aaa, r_gate).astype(BF16)
            w_b = w_in[:, in_a:].astype(BF16)
            w_out = ev_w_out[j].astype(BF16)
            rw = dict(
                shift=_permute_rwkv_cols(ev_shift[j], lay, r_decay, r_aaa, r_gate),
                w0=rwkv_w0[j], a0=rwkv_a0[j],
                w2=_pad_rows(rwkv_w2[j], lay["rp"]).astype(BF16),
                a2=_pad_rows(rwkv_a2[j], lay["rp"]).astype(BF16),
                g2=_pad_rows(rwkv_g2[j], lay["rgp"]).astype(BF16),
                kk=rwkv_kk[j].reshape(1, width_a), ka=rwkv_ka[j].reshape(1, width_a),
                rk=rwkv_rk[j].reshape(1, width_a))
            lnx_g = rwkv_lnx_g[j].reshape(1, width_a)
            lnx_b = rwkv_lnx_b[j].reshape(1, width_a)
            sgu_g = sgu_ln_g[j].reshape(1, -1)
            sgu_bb = sgu_ln_b[j].reshape(1, -1)
            sgu_ws = sgu_w[j].astype(BF16)
            sgu_bt = sgu_b[j].T

            st_lat = _rwkv_prep(_matmul(a_lat, w_a), lay, rw)
            st_ctx = _rwkv_prep(_matmul(a_ctx, w_a), lay, rw)
            s0 = jnp.zeros((batch, heads, HEAD, HEAD), F32)
            yc_f, sc_f = _wkv_scan(st_ctx, 0, s0, False)
            yl_f, _ = _wkv_scan(st_lat, 0, sc_f, False)
            yc_b, sc_b = _wkv_scan(st_ctx, 1, s0, True)
            yl_b, _ = _wkv_scan(st_lat, 1, sc_b, True)
            ya_lat = _rwkv_out(yl_f, yl_b, st_lat, lnx_g, lnx_b)
            yb_lat = _sgu(_matmul(a_lat, w_b), sgu_g, sgu_bb, sgu_ws, sgu_bt)
            mix_lat = _matmul(jnp.concatenate([ya_lat, yb_lat], axis=-1), w_out)
            if ctx_out:
                ya_ctx = _rwkv_out(yc_f, yc_b, st_ctx, lnx_g, lnx_b)
                yb_ctx = _sgu(_matmul(a_ctx, w_b), sgu_g, sgu_bb, sgu_ws, sgu_bt)
                mix_ctx = _matmul(jnp.concatenate([ya_ctx, yb_ctx], axis=-1), w_out)
        else:
            w_in = od_w_in[j].astype(BF16)
            w_out = od_w_out[j].astype(BF16)
            pw = pool_w[j].astype(BF16)
            ps = pool_scale[j].reshape(1, -1)
            mix_lat = _matmul(_odd_mix(_matmul(a_lat, w_in), pw, ps, sconv_w[j], GRID_W), w_out)
            if ctx_out:
                mix_ctx = _matmul(_odd_mix(_matmul(a_ctx, w_in), pw, ps, sconv_w[j], ctx.shape[1]), w_out)

        w_up = ffn_w_up[i].astype(BF16)
        w_down = ffn_w_down[i].astype(BF16)
        nxt = (i + 1, 1, 0) if i + 1 < depth else None
        h_lat, a2 = _ln_residual(h_lat, mix_lat, mod4, lng4, lnb4, lat_row, alpha, i, 2, 0, (i, 4, 3))
        f_lat = _matmul(_ffn_mid(_matmul(a2, w_up), ffn_conv[i], GRID_W), w_down)
        h_lat, a_lat = _ln_residual(h_lat, f_lat, mod4, lng4, lnb4, lat_row, alpha, i, 5, 1, nxt)
        if ctx_out:
            h_ctx, a2 = _ln_residual(h_ctx, mix_ctx, mod4, lng4, lnb4, ctx_row, alpha, i, 2, 0, (i, 4, 3))
            f_ctx = _matmul(_ffn_mid(_matmul(a2, w_up), ffn_conv[i], 1), w_down)
            h_ctx, a_ctx = _ln_residual(h_ctx, f_ctx, mod4, lng4, lnb4, ctx_row, alpha, i, 5, 1, nxt)
    return h_lat
```

```python
import functools
import math

import jax
import jax.numpy as jnp
from jax import lax
from jax.experimental import pallas as pl
from jax.experimental.pallas import tpu as pltpu

GRID_W = 64
POOL_WINDOWS = (2, 4, 8, 16)
LN_EPS = 1e-6
GN_EPS = 64e-5
HEAD = 64
SCAN_CHUNK = 64
LANES = 128
SUBLANES = 8
VMEM_LIMIT_BYTES = 56 * 1024 * 1024

F32 = jnp.float32
BF16 = jnp.bfloat16
HIGHEST = lax.Precision.HIGHEST


def _round_up(n, m):
    return (n + m - 1) // m * m


def _params(*sem):
    return pltpu.CompilerParams(dimension_semantics=sem, vmem_limit_bytes=VMEM_LIMIT_BYTES)


def _sigmoid(z):
    return 1.0 / (1.0 + jnp.exp(-z))


def _softplus(z):
    return jnp.maximum(z, 0.0) + jnp.log1p(jnp.exp(-jnp.abs(z)))


def _mod_kernel(c_ref, w_ref, b_ref, o_ref):
    c = c_ref[...]
    s = (c * _sigmoid(c)).astype(BF16)
    o_ref[...] = jnp.dot(s, w_ref[...].astype(BF16), preferred_element_type=F32) + b_ref[...]


def _modulation_all(cc, ada_w, ada_b):
    depth, d, n = ada_w.shape
    r = cc.shape[0]
    tn = min(512, n)
    return pl.pallas_call(
        _mod_kernel,
        grid=(depth, n // tn),
        in_specs=[pl.BlockSpec((r, d), lambda l, j: (0, 0)),
                  pl.BlockSpec((None, d, tn), lambda l, j: (l, 0, j)),
                  pl.BlockSpec((None, 1, tn), lambda l, j: (l, 0, j))],
        out_specs=pl.BlockSpec((None, r, tn), lambda l, j: (l, 0, j)),
        out_shape=jax.ShapeDtypeStruct((depth, r, n), F32),
        compiler_params=_params("parallel", "parallel"),
        name="adaln_modulation",
    )(cc, ada_w, ada_b.reshape(depth, 1, n))


def _vec_spec(d, layer, part, row):
    return pl.BlockSpec((None, None, 1, d), lambda b, t: (layer, row(b), 0, part))


def _modulate_kernel(x_ref, sc_ref, sh_ref, a_ref):
    a_ref[...] = (x_ref[...] * (1.0 + sc_ref[...]) + sh_ref[...]).astype(a_ref.dtype)


def _modulate(x, mod4, layer, row, sc_part, sh_part):
    b, t, d = x.shape
    tt = min(256, t)
    return pl.pallas_call(
        _modulate_kernel,
        grid=(b, t // tt),
        in_specs=[pl.BlockSpec((None, tt, d), lambda bi, ti: (bi, ti, 0)),
                  _vec_spec(d, layer, sc_part, row), _vec_spec(d, layer, sh_part, row)],
        out_specs=pl.BlockSpec((None, tt, d), lambda bi, ti: (bi, ti, 0)),
        out_shape=jax.ShapeDtypeStruct((b, t, d), BF16),
        compiler_params=_params("parallel", "parallel"),
        name="modulate",
    )(x, mod4, mod4)


def _ln_kernel(*refs, alpha, with_next):
    if with_next:
        h_ref, f_ref, gate_ref, g_ref, b_ref, sc_ref, sh_ref, hn_ref, a_ref = refs
    else:
        h_ref, f_ref, gate_ref, g_ref, b_ref, hn_ref = refs
    z = alpha * h_ref[...] + gate_ref[...] * f_ref[...]
    mu = jnp.mean(z, axis=-1, keepdims=True)
    zc = z - mu
    var = jnp.mean(zc * zc, axis=-1, keepdims=True)
    y = zc * lax.rsqrt(var + LN_EPS) * g_ref[...] + b_ref[...]
    hn_ref[...] = y
    if with_next:
        a_ref[...] = (y * (1.0 + sc_ref[...]) + sh_ref[...]).astype(a_ref.dtype)


def _ln_residual(h, f, mod4, lng4, lnb4, row, alpha, layer, gate_part, ln_idx, nxt):
    b, t, d = h.shape
    tt = min(256, t)
    tok = pl.BlockSpec((None, tt, d), lambda bi, ti: (bi, ti, 0))
    ln_spec = pl.BlockSpec((None, None, 1, d), lambda bi, ti: (layer, ln_idx, 0, 0))
    in_specs = [tok, tok, _vec_spec(d, layer, gate_part, row), ln_spec, ln_spec]
    args = [h, f.reshape(b, t, d), mod4, lng4, lnb4]
    out_specs = [tok]
    out_shape = [jax.ShapeDtypeStruct((b, t, d), F32)]
    if nxt is not None:
        in_specs += [_vec_spec(d, nxt[0], nxt[1], row), _vec_spec(d, nxt[0], nxt[2], row)]
        args += [mod4, mod4]
        out_specs.append(tok)
        out_shape.append(jax.ShapeDtypeStruct((b, t, d), BF16))
    out = pl.pallas_call(
        functools.partial(_ln_kernel, alpha=alpha, with_next=nxt is not None),
        grid=(b, t // tt),
        in_specs=in_specs, out_specs=out_specs, out_shape=out_shape,
        compiler_params=_params("parallel", "parallel"),
        name="deepnorm_residual",
    )(*args)
    return (out[0], out[1]) if nxt is not None else (out[0], None)


def _mm_kernel(*refs):
    *a_refs, w_ref, o_ref, wb_ref = refs

    @pl.when(pl.program_id(1) == 0)
    def _():
        wb_ref[...] = w_ref[...].astype(BF16)

    acc = None
    k0 = 0
    for a_ref in a_refs:
        k = a_ref.shape[1]
        part = jnp.dot(a_ref[...], wb_ref[k0:k0 + k, :], preferred_element_type=F32)
        acc = part if acc is None else acc + part
        k0 += k
    o_ref[...] = acc.astype(o_ref.dtype)


def _matmul(a_parts, w, out_dtype=F32):
    lead = a_parts[0].shape[:-1]
    a2 = [a.reshape(-1, a.shape[-1]) for a in a_parts]
    m = a2[0].shape[0]
    k, n = w.shape
    assert sum(a.shape[1] for a in a2) == k
    tm = min(512, m)
    tn = min(512, n)
    assert m % tm == 0 and n % tn == 0, (m, n)
    out = pl.pallas_call(
        _mm_kernel,
        grid=(n // tn, m // tm),
        in_specs=[pl.BlockSpec((tm, a.shape[1]), lambda j, i: (i, 0)) for a in a2]
        + [pl.BlockSpec((k, tn), lambda j, i: (0, j))],
        out_specs=pl.BlockSpec((tm, tn), lambda j, i: (i, j)),
        out_shape=jax.ShapeDtypeStruct((m, n), out_dtype),
        scratch_shapes=[pltpu.VMEM((k, tn), BF16)],
        compiler_params=_params("parallel", "arbitrary"),
        name="projection",
    )(*a2, w)
    return out.reshape(lead + (n,))


def _head_sum(x):
    li = lax.broadcasted_iota(jnp.int32, (LANES, LANES), 0) // HEAD
    lj = lax.broadcasted_iota(jnp.int32, (LANES, LANES), 1) // HEAD
    ones = (li == lj).astype(F32)
    return jnp.dot(x, ones, precision=HIGHEST, preferred_element_type=F32)


def _prep_kernel(p_ref, pp_ref, pn_ref, shw_ref, w0_ref, w2_ref, a0_ref, a2_ref, g2_ref,
                 kkw_ref, ka_ref, rk_ref,
                 r_o, v_o, kk_o, lwf_o, lwb_o, kf_o, kb_o, bf_o, bb_o, g_o, bv_o,
                 *, width, rp, rgp):
    i = pl.program_id(1)
    last = pl.num_programs(1) - 1
    tt = p_ref.shape[0]
    row = lax.broadcasted_iota(jnp.int32, (tt, 1), 0)

    def shifted(c0, c1):
        x = p_ref[:, c0:c1]
        prev_row = jnp.where(i > 0, pp_ref[SUBLANES - 1:SUBLANES, c0:c1], 0.0)
        next_row = jnp.where(i < last, pn_ref[0:1, c0:c1], 0.0)
        xp = jnp.where(row == 0, prev_row, pltpu.roll(x, 1, 0))
        xn = jnp.where(row == tt - 1, next_row, pltpu.roll(x, tt - 1, 0))
        return shw_ref[0:1, c0:c1] * xp + shw_ref[1:2, c0:c1] * x + shw_ref[2:3, c0:c1] * xn

    hp_count = width // LANES
    o = 3 * width
    kkw = kkw_ref[...]
    ka = ka_ref[...]
    rk = rk_ref[...]
    g = jnp.dot(_sigmoid(shifted(o + 4 * rp, o + 4 * rp + rgp)).astype(BF16), g2_ref[...],
                preferred_element_type=F32)
    g_o[...] = g
    rates = []
    for d in range(2):
        wd = shifted(o + d * rp, o + (d + 1) * rp)
        ad = shifted(o + (2 + d) * rp, o + (3 + d) * rp)
        wl = w0_ref[d:d + 1, :] + jnp.dot(jnp.tanh(wd).astype(BF16), w2_ref[d], preferred_element_type=F32)
        w_log = -_softplus(-wl) - 0.5
        lw = -jnp.exp(w_log)
        ar = _sigmoid(a0_ref[d:d + 1, :] + jnp.dot(ad.astype(BF16), a2_ref[d], preferred_element_type=F32))
        rates.append(ar)
        lw_o = lwf_o if d == 0 else lwb_o
        for hp in range(hp_count):
            lw_o[hp] = lw[:, hp * LANES:(hp + 1) * LANES]
    for hp in range(hp_count):
        c0, c1 = hp * LANES, (hp + 1) * LANES
        r = shifted(c0, c1)
        k = shifted(width + c0, width + c1)
        v = shifted(2 * width + c0, 2 * width + c1)
        kkr = k * kkw[:, c0:c1]
        kk = kkr / jnp.maximum(jnp.sqrt(_head_sum(kkr * kkr)), 1e-12)
        kd0 = k * (1.0 + (rates[0][:, c0:c1] - 1.0) * ka[:, c0:c1])
        kd1 = k * (1.0 + (rates[1][:, c0:c1] - 1.0) * ka[:, c0:c1])
        r_o[hp] = r
        v_o[hp] = v
        kk_o[hp] = kk
        kf_o[hp] = kd0
        kb_o[hp] = kd1
        bf_o[hp] = kk * rates[0][:, c0:c1]
        bb_o[hp] = kk * rates[1][:, c0:c1]
        bonus = _head_sum(r * (0.5 * (kd0 + kd1)) * rk[:, c0:c1])
        bv_o[:, c0:c1] = bonus * v


def _rwkv_prep(pa, lay, rw):
    b, t, na = pa.shape
    width, rp, rgp = lay["width"], lay["rp"], lay["rgp"]
    hp = width // LANES
    tt = min(128, t)
    nb = t // SUBLANES
    tok = lambda w: pl.BlockSpec((None, tt, w), lambda bi, ti: (bi, ti, 0))
    halo_prev = pl.BlockSpec((None, SUBLANES, na),
                             lambda bi, ti: (bi, jnp.maximum(ti * (tt // SUBLANES) - 1, 0), 0))
    halo_next = pl.BlockSpec((None, SUBLANES, na),
                             lambda bi, ti: (bi, jnp.minimum((ti + 1) * (tt // SUBLANES), nb - 1), 0))
    full = lambda a: pl.BlockSpec(a.shape, lambda bi, ti: (0,) * a.ndim)
    pair = pl.BlockSpec((None, hp, tt, LANES), lambda bi, ti: (bi, 0, ti, 0))
    pair_shape = jax.ShapeDtypeStruct((b, hp, t, LANES), F32)
    consts = [rw["shift"], rw["w0"], rw["w2"], rw["a0"], rw["a2"], rw["g2"], rw["kk"], rw["ka"], rw["rk"]]
    outs = pl.pallas_call(
        functools.partial(_prep_kernel, width=width, rp=rp, rgp=rgp),
        grid=(b, t // tt),
        in_specs=[tok(na), halo_prev, halo_next] + [full(a) for a in consts],
        out_specs=[pair] * 9 + [tok(width), tok(width)],
        out_shape=[pair_shape] * 9 + [jax.ShapeDtypeStruct((b, t, width), F32)] * 2,
        compiler_params=_params("parallel", "parallel"),
        name="rwkv_streams",
    )(pa, pa, pa, *consts)
    names = ("r", "v", "kk", "lwf", "lwb", "kf", "kb", "bf", "bb", "g", "bv")
    return dict(zip(names, outs))


def _bdot(a, b):
    return jnp.dot(a.astype(BF16), b.astype(BF16), preferred_element_type=F32)


def _bdot_nt(a, b):
    return lax.dot_general(a.astype(BF16), b.astype(BF16), (((1,), (1,)), ((), ())), preferred_element_type=F32)


def _bdot_tn(a, b):
    return lax.dot_general(a.astype(BF16), b.astype(BF16), (((0,), (0,)), ((), ())), preferred_element_type=F32)


def _running_sum(upto, x):
    tri = upto.astype(BF16)
    hi = x.astype(BF16)
    rest = x - hi.astype(F32)
    mid = rest.astype(BF16)
    lo = (rest - mid.astype(F32)).astype(BF16)
    dot = lambda p: jnp.dot(tri, p, preferred_element_type=F32)
    return dot(hi) + dot(mid) + dot(lo)


def _wkv_kernel(r_ref, v_ref, kk_ref, lw_ref, k_ref, b_ref, s0_ref, y_ref, s_ref, *, reverse):
    c = pl.program_id(1)

    @pl.when(c == 0)
    def _():
        s_ref[...] = s0_ref[...]

    hp_count, ch, _ = r_ref.shape
    ti = lax.broadcasted_iota(jnp.int32, (ch, ch), 0)
    tj = lax.broadcasted_iota(jnp.int32, (ch, ch), 1)
    upto = ((tj >= ti) if reverse else (tj <= ti)).astype(F32)
    eye = (ti == tj).astype(F32)
    gi = lax.broadcasted_iota(jnp.int32, (2 * ch, 2 * ch), 0)
    gj = lax.broadcasted_iota(jnp.int32, (2 * ch, 2 * ch), 1)
    it = jnp.where(gi >= ch, gi - ch, gi)
    jt = jnp.where(gj >= ch, gj - ch, gj)
    earlier = (jt > it) if reverse else (jt < it)
    gmask = earlier | ((jt == it) & (gi >= ch))
    doublings = int(math.log2(ch)) - 1

    heads = range(2 * hp_count)
    halves = [slice(j * HEAD, (j + 1) * HEAD) for j in range(LANES // HEAD)]
    ar, bk, bke, vs, wcs = [], [], [], [], []
    for hp in range(hp_count):
        lw = lw_ref[hp]
        cl = _running_sum(upto, lw)
        tot = jnp.sum(lw, axis=0, keepdims=True)
        e_in = jnp.exp(cl)
        e_out = jnp.exp(-cl)
        e_end = jnp.exp(tot - cl)
        a_t = -kk_ref[hp] * jnp.exp(cl - lw)
        r_t = r_ref[hp] * e_in
        b_t = b_ref[hp] * e_out
        k_t = k_ref[hp] * e_out
        b_e = b_ref[hp] * e_end
        k_e = k_ref[hp] * e_end
        w_c = jnp.exp(tot)
        v = v_ref[hp]
        for sl in halves:
            ar.append(jnp.concatenate([a_t[:, sl], r_t[:, sl]], axis=0).astype(BF16))
            bk.append(jnp.concatenate([b_t[:, sl], k_t[:, sl]], axis=0).astype(BF16))
            bke.append(jnp.concatenate([b_e[:, sl], k_e[:, sl]], axis=0).astype(BF16))
            vs.append(v[:, sl])
            wcs.append(w_c[:, sl])
    g = [jnp.where(gmask, _bdot_nt(ar[h], bk[h]), 0.0) for h in heads]
    state = [s_ref[h] for h in heads]
    ars = [_bdot_nt(ar[h], state[h]) for h in heads]
    rhs = [ars[h][:ch] + _bdot(g[h][:ch, ch:], vs[h]) for h in heads]
    pw = [g[h][:ch, :ch] for h in heads]
    inv = [eye + pw[h] for h in heads]
    for _ in range(doublings):
        pw = [_bdot(p, p) for p in pw]
        inv = [inv[h] + _bdot(inv[h], pw[h]) for h in heads]
    uv = [jnp.concatenate([_bdot(inv[h], rhs[h]), vs[h]], axis=0).astype(BF16) for h in heads]
    ys = [ars[h][ch:] + _bdot(g[h][ch:, :], uv[h]) for h in heads]
    new_state = [state[h] * wcs[h] + _bdot_tn(uv[h], bke[h]) for h in heads]
    for hp in range(hp_count):
        for j, sl in enumerate(halves):
            y_ref[hp, :, sl] = ys[2 * hp + j]
            s_ref[2 * hp + j] = new_state[2 * hp + j]


def _wkv_scan(st, d, s0, reverse):
    r = st["r"]
    b, hp, t, _ = r.shape
    ch = min(SCAN_CHUNK, t)
    nc = t // ch
    heads = hp * (LANES // HEAD)
    cidx = (lambda c: nc - 1 - c) if reverse else (lambda c: c)
    blk = pl.BlockSpec((None, hp, ch, LANES), lambda bi, c: (bi, 0, cidx(c), 0))
    st_spec = pl.BlockSpec((None, heads, HEAD, HEAD), lambda bi, c: (bi, 0, 0, 0))
    lw, k, bb = (st["lwf"], st["kf"], st["bf"]) if d == 0 else (st["lwb"], st["kb"], st["bb"])
    y, s_last = pl.pallas_call(
        functools.partial(_wkv_kernel, reverse=reverse),
        grid=(b, nc),
        in_specs=[blk] * 6 + [st_spec],
        out_specs=[blk, st_spec],
        out_shape=[jax.ShapeDtypeStruct((b, hp, t, LANES), F32),
                   jax.ShapeDtypeStruct((b, heads, HEAD, HEAD), F32)],
        compiler_params=_params("parallel", "arbitrary"),
        name="wkv_scan",
    )(r, st["v"], st["kk"], lw, k, bb, s0)
    return y, s_last


def _rwkv_out_kernel(yf_ref, yb_ref, bv_ref, g_ref, lng_ref, lnb_ref, o_ref):
    hp_count = yf_ref.shape[0]
    for hp in range(hp_count):
        c0, c1 = hp * LANES, (hp + 1) * LANES
        y = yf_ref[hp] + yb_ref[hp]
        mu = _head_sum(y) * (1.0 / HEAD)
        yc = y - mu
        var = _head_sum(yc * yc) * (1.0 / HEAD)
        yn = yc * lax.rsqrt(var + GN_EPS) * lng_ref[:, c0:c1] + lnb_ref[:, c0:c1]
        o_ref[:, c0:c1] = ((yn + bv_ref[:, c0:c1]) * g_ref[:, c0:c1]).astype(o_ref.dtype)


def _rwkv_out(yf, yb, st, lnx_g, lnx_b):
    b, hp, t, _ = yf.shape
    width = hp * LANES
    tt = min(256, t)
    pair = pl.BlockSpec((None, hp, tt, LANES), lambda bi, ti: (bi, 0, ti, 0))
    tok = pl.BlockSpec((None, tt, width), lambda bi, ti: (bi, ti, 0))
    vec = pl.BlockSpec((1, width), lambda bi, ti: (0, 0))
    return pl.pallas_call(
        _rwkv_out_kernel,
        grid=(b, t // tt),
        in_specs=[pair, pair, tok, tok, vec, vec],
        out_specs=tok,
        out_shape=jax.ShapeDtypeStruct((b, t, width), BF16),
        compiler_params=_params("parallel", "parallel"),
        name="rwkv_out",
    )(yf, yb, st["bv"], st["g"], lnx_g, lnx_b)


def _gelu(z):
    return 0.5 * z * (1.0 + lax.erf(z * (2.0 ** -0.5)))


def _sgu_kernel(u_ref, v_ref, lng_ref, lnb_ref, ws_ref, bs_ref, o_ref):
    heads = ws_ref.shape[0]
    width = u_ref.shape[1]
    hw = width // heads
    v = _gelu(v_ref[...].astype(F32))
    mu = jnp.mean(v, axis=-1, keepdims=True)
    vc = v - mu
    var = jnp.mean(vc * vc, axis=-1, keepdims=True)
    vn = (vc * lax.rsqrt(var + LN_EPS) * lng_ref[...] + lnb_ref[...]).astype(BF16)
    for h in range(heads):
        c0, c1 = h * hw, (h + 1) * hw
        s = jnp.dot(ws_ref[h], vn[:, c0:c1], preferred_element_type=F32) + bs_ref[:, h:h + 1]
        o_ref[:, c0:c1] = (_gelu(u_ref[:, c0:c1].astype(F32)) * s).astype(o_ref.dtype)


def _sgu(pb, ln_g, ln_b, ws, bs_t):
    b, t, w2 = pb.shape
    width = w2 // 2
    heads, chunk, _ = ws.shape
    tok_u = pl.BlockSpec((None, chunk, width), lambda bi, ti: (bi, ti, 0))
    tok_v = pl.BlockSpec((None, chunk, width), lambda bi, ti: (bi, ti, 1))
    full = lambda a: pl.BlockSpec(a.shape, lambda bi, ti: (0,) * a.ndim)
    return pl.pallas_call(
        _sgu_kernel,
        grid=(b, t // chunk),
        in_specs=[tok_u, tok_v, full(ln_g), full(ln_b), full(ws), full(bs_t)],
        out_specs=tok_u,
        out_shape=jax.ShapeDtypeStruct((b, t, width), BF16),
        compiler_params=_params("parallel", "parallel"),
        name="spatial_gating",
    )(pb, pb, ln_g, ln_b, ws, bs_t)


def _odd_kernel(xc_ref, bg_ref, cg_ref, xd_ref, pw_ref, ps_ref, cw_ref, o_ref, *, seg):
    tt, width = xc_ref.shape
    groups = pw_ref.shape[0]
    gc = width // groups
    ti = lax.broadcasted_iota(jnp.int32, (tt, tt), 0)
    tj = lax.broadcasted_iota(jnp.int32, (tt, tt), 1)
    same_seg = (ti // seg) == (tj // seg)
    pos = lax.broadcasted_iota(jnp.int32, (tt, 1), 0) % seg
    for gi in range(groups):
        win = POOL_WINDOWS[gi]
        lo = win // 2
        hi = win - 1 - lo
        band = (same_seg & (tj >= ti - lo) & (tj <= ti + hi)).astype(BF16)
        count = (jnp.minimum(pos + hi, seg - 1) - jnp.maximum(pos - lo, 0) + 1).astype(F32)
        c0, c1 = gi * gc, (gi + 1) * gc
        x = xc_ref[:, c0:c1]
        mean = jnp.dot(band, x, preferred_element_type=F32) / count
        p = (mean - x.astype(F32)).astype(BF16)
        y = jnp.dot(p, pw_ref[gi], preferred_element_type=F32) * ps_ref[:, c0:c1]
        o_ref[:, c0:c1] = y.astype(o_ref.dtype)
    z = cg_ref[...].astype(F32) * xd_ref[...].astype(F32)
    zp = jnp.where(pos == 0, 0.0, pltpu.roll(z, 1, 0))
    zn = jnp.where(pos == seg - 1, 0.0, pltpu.roll(z, tt - 1, 0))
    conv = cw_ref[0:1, :] * zp + cw_ref[1:2, :] * z + cw_ref[2:3, :] * zn
    o_ref[:, width:] = (bg_ref[...].astype(F32) * conv).astype(o_ref.dtype)


def _odd_mix(p, pool_w, pool_scale, sconv_w, seg):
    b, t, w4 = p.shape
    width = w4 // 4
    tt = min(max(256, seg), t)
    assert tt % seg == 0 and t % tt == 0
    col = lambda ci: pl.BlockSpec((None, tt, width), lambda bi, ti: (bi, ti, ci))
    full = lambda a: pl.BlockSpec(a.shape, lambda bi, ti: (0,) * a.ndim)
    return pl.pallas_call(
        functools.partial(_odd_kernel, seg=seg),
        grid=(b, t // tt),
        in_specs=[col(0), col(1), col(2), col(3), full(pool_w), full(pool_scale), full(sconv_w)],
        out_specs=pl.BlockSpec((None, tt, 2 * width), lambda bi, ti: (bi, ti, 0)),
        out_shape=jax.ShapeDtypeStruct((b, t, 2 * width), BF16),
        compiler_params=_params("parallel", "parallel"),
        name="pool_shortconv",
    )(p, p, p, p, pool_w, pool_scale, sconv_w)


def _ffn_mid_kernel(g_ref, gp_ref, gn_ref, val_ref, cw_ref, o_ref, *, shift):
    i = pl.program_id(1)
    last = pl.num_programs(1) - 1
    tt = g_ref.shape[0]
    hs = gp_ref.shape[0]
    g = g_ref[...].astype(F32)
    if hs == shift:
        prev_blk = jnp.where(i > 0, gp_ref[...].astype(F32), 0.0)
        next_blk = jnp.where(i < last, gn_ref[...].astype(F32), 0.0)
        if tt > shift:
            gp = jnp.concatenate([prev_blk, g[:tt - shift]], axis=0)
            gn = jnp.concatenate([g[shift:], next_blk], axis=0)
        else:
            gp, gn = prev_blk, next_blk
    else:
        row = lax.broadcasted_iota(jnp.int32, (tt, 1), 0)
        prev_row = jnp.where(i > 0, gp_ref[...].astype(F32)[hs - 1:hs, :], 0.0)
        next_row = jnp.where(i < last, gn_ref[...].astype(F32)[0:1, :], 0.0)
        gp = jnp.where(row == 0, prev_row, pltpu.roll(g, 1, 0))
        gn = jnp.where(row == tt - 1, next_row, pltpu.roll(g, tt - 1, 0))
    conv = cw_ref[0:1, :] * gp + cw_ref[1:2, :] * g + cw_ref[2:3, :] * gn
    o_ref[...] = (conv * _sigmoid(conv) * val_ref[...].astype(F32)).astype(o_ref.dtype)


def _ffn_mid(up, conv_w, shift):
    b, t, f2 = up.shape
    f = f2 // 2
    tt = min(256, t)
    tc = f
    for cand in (1280, 1024, 512, 256, 128):
        if f % cand == 0:
            tc = cand
            break
    ncol = f // tc
    row_tile = SUBLANES * (4 // up.dtype.itemsize)
    hs = shift if shift % row_tile == 0 else row_tile
    assert shift == 1 or (hs == shift and tt % shift == 0)
    nh = t // hs
    main = pl.BlockSpec((None, tt, tc), lambda bi, ti, ci: (bi, ti, ci))
    prev = pl.BlockSpec((None, hs, tc), lambda bi, ti, ci: (bi, jnp.maximum(ti * (tt // hs) - 1, 0), ci))
    nxt = pl.BlockSpec((None, hs, tc), lambda bi, ti, ci: (bi, jnp.minimum((ti + 1) * (tt // hs), nh - 1), ci))
    val = pl.BlockSpec((None, tt, tc), lambda bi, ti, ci: (bi, ti, ci + ncol))
    cw = pl.BlockSpec((3, tc), lambda bi, ti, ci: (0, ci))
    return pl.pallas_call(
        functools.partial(_ffn_mid_kernel, shift=shift),
        grid=(b, t // tt, ncol),
        in_specs=[main, prev, nxt, val, cw],
        out_specs=main,
        out_shape=jax.ShapeDtypeStruct((b, t, f), BF16),
        compiler_params=_params("parallel", "parallel", "parallel"),
        name="convffn_gate",
    )(up, up, up, up, conv_w)


def _pad_cols(w, n):
    return jnp.pad(w, ((0, 0), (0, n - w.shape[1])))


def _even_layout(width, r_decay, r_aaa, r_gate):
    rp = _round_up(max(r_decay, r_aaa), LANES)
    rgp = _round_up(r_gate, LANES)
    na = _round_up(3 * width + 4 * rp + rgp, 1024)
    return dict(width=width, rp=rp, rgp=rgp, na=na)


def _permute_rwkv_cols(w, lay, r_decay, r_aaa, r_gate):
    width, rp, rgp, na = lay["width"], lay["rp"], lay["rgp"], lay["na"]
    o = 3 * width
    parts = [w[:, :o]]
    for seg in (r_decay, r_decay, r_aaa, r_aaa):
        parts.append(_pad_cols(w[:, o:o + seg], rp))
        o += seg
    parts.append(_pad_cols(w[:, o:o + r_gate], rgp))
    return _pad_cols(jnp.concatenate(parts, axis=1), na)


def _pad_rows(w, n):
    pad = [(0, 0)] * w.ndim
    pad[-2] = (0, n - w.shape[-2])
    return jnp.pad(w, pad)


def kernel(x, c, ctx, c_ctx, ada_w, ada_b, ln_g, ln_b, ffn_w_up, ffn_conv, ffn_w_down, ev_w_in, ev_w_out, ev_shift, rwkv_w0, rwkv_w2, rwkv_a0, rwkv_a2, rwkv_g2, rwkv_kk, rwkv_ka, rwkv_rk, rwkv_lnx_g, rwkv_lnx_b, sgu_ln_g, sgu_ln_b, sgu_w, sgu_b, od_w_in, od_w_out, pool_w, pool_scale, sconv_w):
    batch, seq, d = x.shape
    depth = ada_w.shape[0]
    alpha = (2 * depth) ** 0.25
    last_cross = 2 * ((depth - 1) // 2)
    heads = rwkv_rk.shape[1]
    width_a = heads * HEAD
    r_decay, r_aaa, r_gate = rwkv_w2.shape[2], rwkv_a2.shape[2], rwkv_g2.shape[1]
    in_a = 3 * width_a + 2 * r_decay + 2 * r_aaa + r_gate
    lay = _even_layout(width_a, r_decay, r_aaa, r_gate)

    rows = _round_up(batch + 1, SUBLANES)
    cc = jnp.zeros((rows, d), F32).at[:batch].set(c).at[batch].set(c_ctx)
    mod4 = _modulation_all(cc, ada_w, ada_b).reshape(depth, rows, 1, 6 * d)
    lng4 = ln_g.reshape(depth, 2, 1, d)
    lnb4 = ln_b.reshape(depth, 2, 1, d)
    lat_row = lambda b: b
    ctx_row = lambda b: batch

    h_lat, h_ctx = x, ctx
    a_lat = _modulate(x, mod4, 0, lat_row, 1, 0)
    a_ctx = _modulate(ctx, mod4, 0, ctx_row, 1, 0)
    for i in range(depth):
        ctx_in = i <= last_cross
        ctx_out = i < last_cross
        j = i // 2
        mix_ctx = None
        if i % 2 == 0:
            w_in = ev_w_in[j]
            w_a = _permute_rwkv_cols(w_in[:, :in_a], lay, r_decay, r_aaa, r_gate)
            w_b = w_in[:, in_a:]
            w_out = ev_w_out[j]
            rw = dict(
                shift=_permute_rwkv_cols(ev_shift[j], lay, r_decay, r_aaa, r_gate),
                w0=rwkv_w0[j], a0=rwkv_a0[j],
                w2=_pad_rows(rwkv_w2[j], lay["rp"]).astype(BF16),
                a2=_pad_rows(rwkv_a2[j], lay["rp"]).astype(BF16),
                g2=_pad_rows(rwkv_g2[j], lay["rgp"]).astype(BF16),
                kk=rwkv_kk[j].reshape(1, width_a), ka=rwkv_ka[j].reshape(1, width_a),
                rk=rwkv_rk[j].reshape(1, width_a))
            lnx_g = rwkv_lnx_g[j].reshape(1, width_a)
            lnx_b = rwkv_lnx_b[j].reshape(1, width_a)
            sgu_g = sgu_ln_g[j].reshape(1, -1)
            sgu_bb = sgu_ln_b[j].reshape(1, -1)
            sgu_ws = sgu_w[j].astype(BF16)
            sgu_bt = sgu_b[j].T

            st_lat = _rwkv_prep(_matmul([a_lat], w_a), lay, rw)
            st_ctx = _rwkv_prep(_matmul([a_ctx], w_a), lay, rw)
            s0 = jnp.zeros((batch, heads, HEAD, HEAD), F32)
            yc_f, sc_f = _wkv_scan(st_ctx, 0, s0, False)
            yl_f, _ = _wkv_scan(st_lat, 0, sc_f, False)
            yc_b, sc_b = _wkv_scan(st_ctx, 1, s0, True)
            yl_b, _ = _wkv_scan(st_lat, 1, sc_b, True)
            ya_lat = _rwkv_out(yl_f, yl_b, st_lat, lnx_g, lnx_b)
            yb_lat = _sgu(_matmul([a_lat], w_b, BF16), sgu_g, sgu_bb, sgu_ws, sgu_bt)
            mix_lat = _matmul([ya_lat, yb_lat], w_out)
            if ctx_out:
                ya_ctx = _rwkv_out(yc_f, yc_b, st_ctx, lnx_g, lnx_b)
                yb_ctx = _sgu(_matmul([a_ctx], w_b, BF16), sgu_g, sgu_bb, sgu_ws, sgu_bt)
                mix_ctx = _matmul([ya_ctx, yb_ctx], w_out)
        else:
            w_in = od_w_in[j]
            w_out = od_w_out[j]
            pw = pool_w[j].astype(BF16)
            ps = pool_scale[j].reshape(1, -1)
            mix_lat = _matmul([_odd_mix(_matmul([a_lat], w_in, BF16), pw, ps, sconv_w[j], GRID_W)], w_out)
            if ctx_out:
                mix_ctx = _matmul([_odd_mix(_matmul([a_ctx], w_in, BF16), pw, ps, sconv_w[j], ctx.shape[1])], w_out)

        w_up = ffn_w_up[i]
        w_down = ffn_w_down[i]
        nxt = (i + 1, 1, 0) if i + 1 < depth else None
        h_lat, a2 = _ln_residual(h_lat, mix_lat, mod4, lng4, lnb4, lat_row, alpha, i, 2, 0, (i, 4, 3))
        f_lat = _matmul([_ffn_mid(_matmul([a2], w_up, BF16), ffn_conv[i], GRID_W)], w_down)
        h_lat, a_lat = _ln_residual(h_lat, f_lat, mod4, lng4, lnb4, lat_row, alpha, i, 5, 1, nxt)
        if ctx_out:
            h_ctx, a2 = _ln_residual(h_ctx, mix_ctx, mod4, lng4, lnb4, ctx_row, alpha, i, 2, 0, (i, 4, 3))
            f_ctx = _matmul([_ffn_mid(_matmul([a2], w_up, BF16), ffn_conv[i], 1)], w_down)
            h_ctx, a_ctx = _ln_residual(h_ctx, f_ctx, mod4, lng4, lnb4, ctx_row, alpha, i, 5, 1, nxt)
    return h_lat
```

```python
import functools
import math

import jax
import jax.numpy as jnp
from jax import lax
from jax.experimental import pallas as pl
from jax.experimental.pallas import tpu as pltpu

GRID_W = 64
POOL_WINDOWS = (2, 4, 8, 16)
LN_EPS = 1e-6
GN_EPS = 64e-5
HEAD = 64
SCAN_CHUNK = 64
LANES = 128
SUBLANES = 8
VMEM_LIMIT_BYTES = 56 * 1024 * 1024

F32 = jnp.float32
BF16 = jnp.bfloat16


def _round_up(n, m):
    return (n + m - 1) // m * m


def _params(*sem):
    return pltpu.CompilerParams(dimension_semantics=sem, vmem_limit_bytes=VMEM_LIMIT_BYTES)


def _sigmoid(z):
    return 1.0 / (1.0 + jnp.exp(-z))


def _softplus(z):
    return jnp.maximum(z, 0.0) + jnp.log1p(jnp.exp(-jnp.abs(z)))


def _mod_kernel(c_ref, w_ref, b_ref, o_ref):
    c = c_ref[...]
    s = (c * _sigmoid(c)).astype(BF16)
    o_ref[...] = jnp.dot(s, w_ref[...].astype(BF16), preferred_element_type=F32) + b_ref[...]


def _modulation_all(cc, ada_w, ada_b):
    depth, d, n = ada_w.shape
    r = cc.shape[0]
    tn = min(512, n)
    return pl.pallas_call(
        _mod_kernel,
        grid=(depth, n // tn),
        in_specs=[pl.BlockSpec((r, d), lambda l, j: (0, 0)),
                  pl.BlockSpec((None, d, tn), lambda l, j: (l, 0, j)),
                  pl.BlockSpec((None, 1, tn), lambda l, j: (l, 0, j))],
        out_specs=pl.BlockSpec((None, r, tn), lambda l, j: (l, 0, j)),
        out_shape=jax.ShapeDtypeStruct((depth, r, n), F32),
        compiler_params=_params("parallel", "parallel"),
        name="adaln_modulation",
    )(cc, ada_w, ada_b.reshape(depth, 1, n))


def _vec_spec(d, layer, part, row):
    return pl.BlockSpec((None, None, 1, d), lambda b, t: (layer, row(b), 0, part))


def _modulate_kernel(x_ref, sc_ref, sh_ref, a_ref):
    a_ref[...] = (x_ref[...] * (1.0 + sc_ref[...]) + sh_ref[...]).astype(a_ref.dtype)


def _modulate(x, mod4, layer, row, sc_part, sh_part):
    b, t, d = x.shape
    tt = min(256, t)
    return pl.pallas_call(
        _modulate_kernel,
        grid=(b, t // tt),
        in_specs=[pl.BlockSpec((None, tt, d), lambda bi, ti: (bi, ti, 0)),
                  _vec_spec(d, layer, sc_part, row), _vec_spec(d, layer, sh_part, row)],
        out_specs=pl.BlockSpec((None, tt, d), lambda bi, ti: (bi, ti, 0)),
        out_shape=jax.ShapeDtypeStruct((b, t, d), BF16),
        compiler_params=_params("parallel", "parallel"),
        name="modulate",
    )(x, mod4, mod4)


def _ln_kernel(*refs, alpha, with_next):
    if with_next:
        h_ref, f_ref, gate_ref, g_ref, b_ref, sc_ref, sh_ref, hn_ref, a_ref = refs
    else:
        h_ref, f_ref, gate_ref, g_ref, b_ref, hn_ref = refs
    z = alpha * h_ref[...] + gate_ref[...] * f_ref[...]
    mu = jnp.mean(z, axis=-1, keepdims=True)
    zc = z - mu
    var = jnp.mean(zc * zc, axis=-1, keepdims=True)
    y = zc * lax.rsqrt(var + LN_EPS) * g_ref[...] + b_ref[...]
    hn_ref[...] = y
    if with_next:
        a_ref[...] = (y * (1.0 + sc_ref[...]) + sh_ref[...]).astype(a_ref.dtype)


def _ln_residual(h, f, mod4, lng4, lnb4, row, alpha, layer, gate_part, ln_idx, nxt):
    b, t, d = h.shape
    tt = min(256, t)
    tok = pl.BlockSpec((None, tt, d), lambda bi, ti: (bi, ti, 0))
    ln_spec = pl.BlockSpec((None, None, 1, d), lambda bi, ti: (layer, ln_idx, 0, 0))
    in_specs = [tok, tok, _vec_spec(d, layer, gate_part, row), ln_spec, ln_spec]
    args = [h, f.reshape(b, t, d), mod4, lng4, lnb4]
    out_specs = [tok]
    out_shape = [jax.ShapeDtypeStruct((b, t, d), F32)]
    if nxt is not None:
        in_specs += [_vec_spec(d, nxt[0], nxt[1], row), _vec_spec(d, nxt[0], nxt[2], row)]
        args += [mod4, mod4]
        out_specs.append(tok)
        out_shape.append(jax.ShapeDtypeStruct((b, t, d), BF16))
    out = pl.pallas_call(
        functools.partial(_ln_kernel, alpha=alpha, with_next=nxt is not None),
        grid=(b, t // tt),
        in_specs=in_specs, out_specs=out_specs, out_shape=out_shape,
        compiler_params=_params("parallel", "parallel"),
        name="deepnorm_residual",
    )(*args)
    return (out[0], out[1]) if nxt is not None else (out[0], None)


def _mm_kernel(*refs):
    *a_refs, w_ref, o_ref = refs
    acc = None
    k0 = 0
    for a_ref in a_refs:
        k = a_ref.shape[1]
        part = jnp.dot(a_ref[...], w_ref[k0:k0 + k, :], preferred_element_type=F32)
        acc = part if acc is None else acc + part
        k0 += k
    o_ref[...] = acc.astype(o_ref.dtype)


def _matmul_tiles(m, k, n, out_bytes):
    budget = VMEM_LIMIT_BYTES - 8 * 1024 * 1024
    best = None
    for tm in (1024, 512, 256, 128, 64, 32, 16):
        if m % tm:
            continue
        for tn in (2048, 1792, 1536, 1280, 1024, 768, 512, 256, 128):
            if n % tn:
                continue
            need = 2 * tm * k * 2 + 2 * k * tn * 2 + 2 * tm * tn * out_bytes + tm * tn * 4
            if need <= budget and (best is None or tm * tn > best[0] * best[1]):
                best = (tm, tn)
    assert best is not None, (m, k, n)
    return best


def _matmul(a_parts, w, out_dtype=F32):
    lead = a_parts[0].shape[:-1]
    a2 = [a.reshape(-1, a.shape[-1]) for a in a_parts]
    m = a2[0].shape[0]
    k, n = w.shape
    assert sum(a.shape[1] for a in a2) == k
    tm, tn = _matmul_tiles(m, k, n, jnp.dtype(out_dtype).itemsize)
    out = pl.pallas_call(
        _mm_kernel,
        grid=(n // tn, m // tm),
        in_specs=[pl.BlockSpec((tm, a.shape[1]), lambda j, i: (i, 0)) for a in a2]
        + [pl.BlockSpec((k, tn), lambda j, i: (0, j))],
        out_specs=pl.BlockSpec((tm, tn), lambda j, i: (i, j)),
        out_shape=jax.ShapeDtypeStruct((m, n), out_dtype),
        compiler_params=_params("parallel", "parallel"),
        name="projection",
    )(*a2, w)
    return out.reshape(lead + (n,))


def _head_sum(x):
    li = (lax.broadcasted_iota(jnp.int32, (2 * LANES, LANES), 0) % LANES) // HEAD
    lj = lax.broadcasted_iota(jnp.int32, (2 * LANES, LANES), 1) // HEAD
    ones = (li == lj).astype(BF16)
    hi = x.astype(BF16)
    lo = (x - hi.astype(F32)).astype(BF16)
    return jnp.dot(jnp.concatenate([hi, lo], axis=1), ones, preferred_element_type=F32)


def _prep_kernel(p_ref, pp_ref, pn_ref, shw_ref, w0_ref, w2_ref, a0_ref, a2_ref, g2_ref,
                 kkw_ref, ka_ref, rk_ref,
                 r_o, v_o, kk_o, lwf_o, lwb_o, kf_o, kb_o, bf_o, bb_o, g_o, bv_o,
                 *, width, rp, rgp):
    i = pl.program_id(1)
    last = pl.num_programs(1) - 1
    tt = p_ref.shape[0]
    row = lax.broadcasted_iota(jnp.int32, (tt, 1), 0)

    def shifted(c0, c1):
        x = p_ref[:, c0:c1]
        prev_row = jnp.where(i > 0, pp_ref[SUBLANES - 1:SUBLANES, c0:c1], 0.0)
        next_row = jnp.where(i < last, pn_ref[0:1, c0:c1], 0.0)
        xp = jnp.where(row == 0, prev_row, pltpu.roll(x, 1, 0))
        xn = jnp.where(row == tt - 1, next_row, pltpu.roll(x, tt - 1, 0))
        return shw_ref[0:1, c0:c1] * xp + shw_ref[1:2, c0:c1] * x + shw_ref[2:3, c0:c1] * xn

    hp_count = width // LANES
    o = 3 * width
    kkw = kkw_ref[...]
    ka = ka_ref[...]
    rk = rk_ref[...]
    g = jnp.dot(_sigmoid(shifted(o + 4 * rp, o + 4 * rp + rgp)).astype(BF16), g2_ref[...],
                preferred_element_type=F32)
    g_o[...] = g
    rates = []
    for d in range(2):
        wd = shifted(o + d * rp, o + (d + 1) * rp)
        ad = shifted(o + (2 + d) * rp, o + (3 + d) * rp)
        wl = w0_ref[d:d + 1, :] + jnp.dot(jnp.tanh(wd).astype(BF16), w2_ref[d], preferred_element_type=F32)
        w_log = -_softplus(-wl) - 0.5
        lw = -jnp.exp(w_log)
        ar = _sigmoid(a0_ref[d:d + 1, :] + jnp.dot(ad.astype(BF16), a2_ref[d], preferred_element_type=F32))
        rates.append(ar)
        lw_o = lwf_o if d == 0 else lwb_o
        for hp in range(hp_count):
            lw_o[hp] = lw[:, hp * LANES:(hp + 1) * LANES]
    for hp in range(hp_count):
        c0, c1 = hp * LANES, (hp + 1) * LANES
        r = shifted(c0, c1)
        k = shifted(width + c0, width + c1)
        v = shifted(2 * width + c0, 2 * width + c1)
        kkr = k * kkw[:, c0:c1]
        kk = kkr / jnp.maximum(jnp.sqrt(_head_sum(kkr * kkr)), 1e-12)
        kd0 = k * (1.0 + (rates[0][:, c0:c1] - 1.0) * ka[:, c0:c1])
        kd1 = k * (1.0 + (rates[1][:, c0:c1] - 1.0) * ka[:, c0:c1])
        r_o[hp] = r
        v_o[hp] = v
        kk_o[hp] = kk
        kf_o[hp] = kd0
        kb_o[hp] = kd1
        bf_o[hp] = kk * rates[0][:, c0:c1]
        bb_o[hp] = kk * rates[1][:, c0:c1]
        bonus = _head_sum(r * (0.5 * (kd0 + kd1)) * rk[:, c0:c1])
        bv_o[:, c0:c1] = bonus * v


def _rwkv_prep(pa, lay, rw):
    b, t, na = pa.shape
    width, rp, rgp = lay["width"], lay["rp"], lay["rgp"]
    hp = width // LANES
    tt = min(128, t)
    nb = t // SUBLANES
    tok = lambda w: pl.BlockSpec((None, tt, w), lambda bi, ti: (bi, ti, 0))
    halo_prev = pl.BlockSpec((None, SUBLANES, na),
                             lambda bi, ti: (bi, jnp.maximum(ti * (tt // SUBLANES) - 1, 0), 0))
    halo_next = pl.BlockSpec((None, SUBLANES, na),
                             lambda bi, ti: (bi, jnp.minimum((ti + 1) * (tt // SUBLANES), nb - 1), 0))
    full = lambda a: pl.BlockSpec(a.shape, lambda bi, ti: (0,) * a.ndim)
    pair = pl.BlockSpec((None, hp, tt, LANES), lambda bi, ti: (bi, 0, ti, 0))
    pair_shape = jax.ShapeDtypeStruct((b, hp, t, LANES), F32)
    consts = [rw["shift"], rw["w0"], rw["w2"], rw["a0"], rw["a2"], rw["g2"], rw["kk"], rw["ka"], rw["rk"]]
    outs = pl.pallas_call(
        functools.partial(_prep_kernel, width=width, rp=rp, rgp=rgp),
        grid=(b, t // tt),
        in_specs=[tok(na), halo_prev, halo_next] + [full(a) for a in consts],
        out_specs=[pair] * 9 + [tok(width), tok(width)],
        out_shape=[pair_shape] * 9 + [jax.ShapeDtypeStruct((b, t, width), F32)] * 2,
        compiler_params=_params("parallel", "parallel"),
        name="rwkv_streams",
    )(pa, pa, pa, *consts)
    names = ("r", "v", "kk", "lwf", "lwb", "kf", "kb", "bf", "bb", "g", "bv")
    return dict(zip(names, outs))


def _bdot(a, b):
    return jnp.dot(a.astype(BF16), b.astype(BF16), preferred_element_type=F32)


def _bdot_nt(a, b):
    return lax.dot_general(a.astype(BF16), b.astype(BF16), (((1,), (1,)), ((), ())), preferred_element_type=F32)


def _bdot_tn(a, b):
    return lax.dot_general(a.astype(BF16), b.astype(BF16), (((0,), (0,)), ((), ())), preferred_element_type=F32)


def _running_sum(upto, x):
    tri = upto.astype(BF16)
    hi = x.astype(BF16)
    rest = x - hi.astype(F32)
    mid = rest.astype(BF16)
    lo = (rest - mid.astype(F32)).astype(BF16)
    dot = lambda p: jnp.dot(tri, p, preferred_element_type=F32)
    return dot(hi) + dot(mid) + dot(lo)


def _wkv_kernel(r_ref, v_ref, kk_ref, lw_ref, k_ref, b_ref, s0_ref, y_ref, s_ref, *, reverse):
    c = pl.program_id(1)

    @pl.when(c == 0)
    def _():
        s_ref[...] = s0_ref[...]

    hp_count, ch, _ = r_ref.shape
    pairs = range(hp_count)
    ti = lax.broadcasted_iota(jnp.int32, (ch, ch), 0)
    tj = lax.broadcasted_iota(jnp.int32, (ch, ch), 1)
    upto = ((tj >= ti) if reverse else (tj <= ti)).astype(F32)
    ei = lax.broadcasted_iota(jnp.int32, (ch, LANES), 0)
    ej = lax.broadcasted_iota(jnp.int32, (ch, LANES), 1)
    eye2 = (ei == ej % HEAD).astype(F32)
    gi = lax.broadcasted_iota(jnp.int32, (2 * ch, 2 * LANES), 0)
    gj = lax.broadcasted_iota(jnp.int32, (2 * ch, 2 * LANES), 1) % ch
    it = gi % ch
    earlier = (gj > it) if reverse else (gj < it)
    gmask = earlier | ((gj == it) & (gi >= ch))
    first = lax.broadcasted_iota(jnp.int32, (1, LANES), 1) < HEAD
    si = lax.broadcasted_iota(jnp.int32, (LANES, LANES), 0) // HEAD
    sj = lax.broadcasted_iota(jnp.int32, (LANES, LANES), 1) // HEAD
    same_head = si == sj
    doublings = int(math.log2(ch)) - 1
    zero = jnp.zeros((), BF16)

    def split(x):
        x = x.astype(BF16)
        return jnp.concatenate([jnp.where(first, x, zero), jnp.where(first, zero, x)], axis=0)

    ar, brows, bke, vs, wcs = [], [], [], [], []
    for hp in pairs:
        lw = lw_ref[hp]
        cl = _running_sum(upto, lw)
        tot = jnp.sum(lw, axis=0, keepdims=True)
        e_in = jnp.exp(cl)
        e_out = jnp.exp(-cl)
        e_end = jnp.exp(tot - cl)
        a_t = -kk_ref[hp] * jnp.exp(cl - lw)
        b = b_ref[hp]
        k = k_ref[hp]
        ar.append(jnp.concatenate([a_t, r_ref[hp] * e_in], axis=0).astype(BF16))
        brows.append(jnp.concatenate([split(b * e_out), split(k * e_out)], axis=0))
        bke.append(jnp.concatenate([b * e_end, k * e_end], axis=0).astype(BF16))
        vs.append(v_ref[hp])
        wcs.append(jnp.exp(tot))
    state = [s_ref[hp] for hp in pairs]
    g = [jnp.where(gmask, _bdot_nt(ar[p], brows[p]), 0.0) for p in pairs]
    ars = [_bdot_nt(ar[p], state[p]) for p in pairs]
    vsplit = [split(vs[p]) for p in pairs]
    rhs = [ars[p][:ch] + _bdot(g[p][:ch, LANES:], vsplit[p]) for p in pairs]
    inv = [eye2 + g[p][:ch, :LANES] for p in pairs]
    pw = [_bdot(g[p][:ch, :LANES], split(g[p][:ch, :LANES])) for p in pairs]
    for _ in range(doublings - 1):
        prod = [_bdot(pw[p], jnp.concatenate([split(inv[p]), split(pw[p])], axis=1)) for p in pairs]
        inv = [inv[p] + prod[p][:, :LANES] for p in pairs]
        pw = [prod[p][:, LANES:] for p in pairs]
    inv = [inv[p] + _bdot(pw[p], split(inv[p])) for p in pairs]
    u = [_bdot(inv[p], split(rhs[p])) for p in pairs]
    ys = [ars[p][ch:] + _bdot(g[p][ch:, :], jnp.concatenate([split(u[p]), vsplit[p]], axis=0)) for p in pairs]
    upd = [_bdot_tn(jnp.concatenate([u[p], vs[p]], axis=0), bke[p]) for p in pairs]
    for hp in pairs:
        y_ref[hp] = ys[hp]
        s_ref[hp] = state[hp] * wcs[hp] + jnp.where(same_head, upd[hp], 0.0)


def _wkv_scan(st, d, s0, reverse):
    r = st["r"]
    b, hp, t, _ = r.shape
    ch = min(SCAN_CHUNK, t)
    nc = t // ch
    cidx = (lambda c: nc - 1 - c) if reverse else (lambda c: c)
    blk = pl.BlockSpec((None, hp, ch, LANES), lambda bi, c: (bi, 0, cidx(c), 0))
    st_spec = pl.BlockSpec((None, hp, LANES, LANES), lambda bi, c: (bi, 0, 0, 0))
    lw, k, bb = (st["lwf"], st["kf"], st["bf"]) if d == 0 else (st["lwb"], st["kb"], st["bb"])
    y, s_last = pl.pallas_call(
        functools.partial(_wkv_kernel, reverse=reverse),
        grid=(b, nc),
        in_specs=[blk] * 6 + [st_spec],
        out_specs=[blk, st_spec],
        out_shape=[jax.ShapeDtypeStruct((b, hp, t, LANES), F32),
                   jax.ShapeDtypeStruct((b, hp, LANES, LANES), F32)],
        compiler_params=_params("parallel", "arbitrary"),
        name="wkv_scan",
    )(r, st["v"], st["kk"], lw, k, bb, s0)
    return y, s_last


def _rwkv_out_kernel(yf_ref, yb_ref, bv_ref, g_ref, lng_ref, lnb_ref, o_ref):
    hp_count = yf_ref.shape[0]
    for hp in range(hp_count):
        c0, c1 = hp * LANES, (hp + 1) * LANES
        y = yf_ref[hp] + yb_ref[hp]
        mu = _head_sum(y) * (1.0 / HEAD)
        yc = y - mu
        var = _head_sum(yc * yc) * (1.0 / HEAD)
        yn = yc * lax.rsqrt(var + GN_EPS) * lng_ref[:, c0:c1] + lnb_ref[:, c0:c1]
        o_ref[:, c0:c1] = ((yn + bv_ref[:, c0:c1]) * g_ref[:, c0:c1]).astype(o_ref.dtype)


def _rwkv_out(yf, yb, st, lnx_g, lnx_b):
    b, hp, t, _ = yf.shape
    width = hp * LANES
    tt = min(256, t)
    pair = pl.BlockSpec((None, hp, tt, LANES), lambda bi, ti: (bi, 0, ti, 0))
    tok = pl.BlockSpec((None, tt, width), lambda bi, ti: (bi, ti, 0))
    vec = pl.BlockSpec((1, width), lambda bi, ti: (0, 0))
    return pl.pallas_call(
        _rwkv_out_kernel,
        grid=(b, t // tt),
        in_specs=[pair, pair, tok, tok, vec, vec],
        out_specs=tok,
        out_shape=jax.ShapeDtypeStruct((b, t, width), BF16),
        compiler_params=_params("parallel", "parallel"),
        name="rwkv_out",
    )(yf, yb, st["bv"], st["g"], lnx_g, lnx_b)


def _gelu(z):
    return 0.5 * z * (1.0 + lax.erf(z * (2.0 ** -0.5)))


def _sgu_kernel(u_ref, v_ref, lng_ref, lnb_ref, ws_ref, bs_ref, o_ref):
    heads = ws_ref.shape[0]
    width = u_ref.shape[1]
    hw = width // heads
    v = _gelu(v_ref[...].astype(F32))
    mu = jnp.mean(v, axis=-1, keepdims=True)
    vc = v - mu
    var = jnp.mean(vc * vc, axis=-1, keepdims=True)
    vn = (vc * lax.rsqrt(var + LN_EPS) * lng_ref[...] + lnb_ref[...]).astype(BF16)
    for h in range(heads):
        c0, c1 = h * hw, (h + 1) * hw
        s = jnp.dot(ws_ref[h], vn[:, c0:c1], preferred_element_type=F32) + bs_ref[:, h:h + 1]
        o_ref[:, c0:c1] = (_gelu(u_ref[:, c0:c1].astype(F32)) * s).astype(o_ref.dtype)


def _sgu(pb, ln_g, ln_b, ws, bs_t):
    b, t, w2 = pb.shape
    width = w2 // 2
    heads, chunk, _ = ws.shape
    tok_u = pl.BlockSpec((None, chunk, width), lambda bi, ti: (bi, ti, 0))
    tok_v = pl.BlockSpec((None, chunk, width), lambda bi, ti: (bi, ti, 1))
    full = lambda a: pl.BlockSpec(a.shape, lambda bi, ti: (0,) * a.ndim)
    return pl.pallas_call(
        _sgu_kernel,
        grid=(b, t // chunk),
        in_specs=[tok_u, tok_v, full(ln_g), full(ln_b), full(ws), full(bs_t)],
        out_specs=tok_u,
        out_shape=jax.ShapeDtypeStruct((b, t, width), BF16),
        compiler_params=_params("parallel", "parallel"),
        name="spatial_gating",
    )(pb, pb, ln_g, ln_b, ws, bs_t)


def _odd_kernel(xc_ref, bg_ref, cg_ref, xd_ref, pw_ref, ps_ref, cw_ref, o_ref, *, seg):
    tt, width = xc_ref.shape
    groups = pw_ref.shape[0]
    gc = width // groups
    ti = lax.broadcasted_iota(jnp.int32, (tt, tt), 0)
    tj = lax.broadcasted_iota(jnp.int32, (tt, tt), 1)
    same_seg = (ti // seg) == (tj // seg)
    pos = lax.broadcasted_iota(jnp.int32, (tt, 1), 0) % seg
    for gi in range(groups):
        win = POOL_WINDOWS[gi]
        lo = win // 2
        hi = win - 1 - lo
        band = (same_seg & (tj >= ti - lo) & (tj <= ti + hi)).astype(BF16)
        count = (jnp.minimum(pos + hi, seg - 1) - jnp.maximum(pos - lo, 0) + 1).astype(F32)
        c0, c1 = gi * gc, (gi + 1) * gc
        x = xc_ref[:, c0:c1]
        mean = jnp.dot(band, x, preferred_element_type=F32) / count
        p = (mean - x.astype(F32)).astype(BF16)
        y = jnp.dot(p, pw_ref[gi], preferred_element_type=F32) * ps_ref[:, c0:c1]
        o_ref[:, c0:c1] = y.astype(o_ref.dtype)
    z = cg_ref[...].astype(F32) * xd_ref[...].astype(F32)
    zp = jnp.where(pos == 0, 0.0, pltpu.roll(z, 1, 0))
    zn = jnp.where(pos == seg - 1, 0.0, pltpu.roll(z, tt - 1, 0))
    conv = cw_ref[0:1, :] * zp + cw_ref[1:2, :] * z + cw_ref[2:3, :] * zn
    o_ref[:, width:] = (bg_ref[...].astype(F32) * conv).astype(o_ref.dtype)


def _odd_mix(p, pool_w, pool_scale, sconv_w, seg):
    b, t, w4 = p.shape
    width = w4 // 4
    tt = min(max(256, seg), t)
    assert tt % seg == 0 and t % tt == 0
    col = lambda ci: pl.BlockSpec((None, tt, width), lambda bi, ti: (bi, ti, ci))
    full = lambda a: pl.BlockSpec(a.shape, lambda bi, ti: (0,) * a.ndim)
    return pl.pallas_call(
        functools.partial(_odd_kernel, seg=seg),
        grid=(b, t // tt),
        in_specs=[col(0), col(1), col(2), col(3), full(pool_w), full(pool_scale), full(sconv_w)],
        out_specs=pl.BlockSpec((None, tt, 2 * width), lambda bi, ti: (bi, ti, 0)),
        out_shape=jax.ShapeDtypeStruct((b, t, 2 * width), BF16),
        compiler_params=_params("parallel", "parallel"),
        name="pool_shortconv",
    )(p, p, p, p, pool_w, pool_scale, sconv_w)


def _ffn_mid_kernel(g_ref, gp_ref, gn_ref, val_ref, cw_ref, o_ref, *, shift):
    i = pl.program_id(1)
    last = pl.num_programs(1) - 1
    tt = g_ref.shape[0]
    hs = gp_ref.shape[0]
    g = g_ref[...].astype(F32)
    if hs == shift:
        prev_blk = jnp.where(i > 0, gp_ref[...].astype(F32), 0.0)
        next_blk = jnp.where(i < last, gn_ref[...].astype(F32), 0.0)
        if tt > shift:
            gp = jnp.concatenate([prev_blk, g[:tt - shift]], axis=0)
            gn = jnp.concatenate([g[shift:], next_blk], axis=0)
        else:
            gp, gn = prev_blk, next_blk
    else:
        row = lax.broadcasted_iota(jnp.int32, (tt, 1), 0)
        prev_row = jnp.where(i > 0, gp_ref[...].astype(F32)[hs - 1:hs, :], 0.0)
        next_row = jnp.where(i < last, gn_ref[...].astype(F32)[0:1, :], 0.0)
        gp = jnp.where(row == 0, prev_row, pltpu.roll(g, 1, 0))
        gn = jnp.where(row == tt - 1, next_row, pltpu.roll(g, tt - 1, 0))
    conv = cw_ref[0:1, :] * gp + cw_ref[1:2, :] * g + cw_ref[2:3, :] * gn
    o_ref[...] = (conv * _sigmoid(conv) * val_ref[...].astype(F32)).astype(o_ref.dtype)


def _ffn_mid(up, conv_w, shift):
    b, t, f2 = up.shape
    f = f2 // 2
    tt = min(512, t)
    tc = f
    for cand in (1280, 1024, 512, 256, 128):
        if f % cand == 0:
            tc = cand
            break
    ncol = f // tc
    row_tile = SUBLANES * (4 // up.dtype.itemsize)
    hs = shift if shift % row_tile == 0 else row_tile
    assert shift == 1 or (hs == shift and tt % shift == 0)
    nh = t // hs
    main = pl.BlockSpec((None, tt, tc), lambda bi, ti, ci: (bi, ti, ci))
    prev = pl.BlockSpec((None, hs, tc), lambda bi, ti, ci: (bi, jnp.maximum(ti * (tt // hs) - 1, 0), ci))
    nxt = pl.BlockSpec((None, hs, tc), lambda bi, ti, ci: (bi, jnp.minimum((ti + 1) * (tt // hs), nh - 1), ci))
    val = pl.BlockSpec((None, tt, tc), lambda bi, ti, ci: (bi, ti, ci + ncol))
    cw = pl.BlockSpec((3, tc), lambda bi, ti, ci: (0, ci))
    return pl.pallas_call(
        functools.partial(_ffn_mid_kernel, shift=shift),
        grid=(b, t // tt, ncol),
        in_specs=[main, prev, nxt, val, cw],
        out_specs=main,
        out_shape=jax.ShapeDtypeStruct((b, t, f), BF16),
        compiler_params=_params("parallel", "parallel", "parallel"),
        name="convffn_gate",
    )(up, up, up, up, conv_w)


def _pad_cols(w, n):
    return jnp.pad(w, ((0, 0), (0, n - w.shape[1])))


def _even_layout(width, r_decay, r_aaa, r_gate):
    rp = _round_up(max(r_decay, r_aaa), LANES)
    rgp = _round_up(r_gate, LANES)
    na = _round_up(3 * width + 4 * rp + rgp, 1024)
    return dict(width=width, rp=rp, rgp=rgp, na=na)


def _permute_rwkv_cols(w, lay, r_decay, r_aaa, r_gate):
    width, rp, rgp, na = lay["width"], lay["rp"], lay["rgp"], lay["na"]
    o = 3 * width
    parts = [w[:, :o]]
    for seg in (r_decay, r_decay, r_aaa, r_aaa):
        parts.append(_pad_cols(w[:, o:o + seg], rp))
        o += seg
    parts.append(_pad_cols(w[:, o:o + r_gate], rgp))
    return _pad_cols(jnp.concatenate(parts, axis=1), na)


def _pad_rows(w, n):
    pad = [(0, 0)] * w.ndim
    pad[-2] = (0, n - w.shape[-2])
    return jnp.pad(w, pad)


def kernel(x, c, ctx, c_ctx, ada_w, ada_b, ln_g, ln_b, ffn_w_up, ffn_conv, ffn_w_down, ev_w_in, ev_w_out, ev_shift, rwkv_w0, rwkv_w2, rwkv_a0, rwkv_a2, rwkv_g2, rwkv_kk, rwkv_ka, rwkv_rk, rwkv_lnx_g, rwkv_lnx_b, sgu_ln_g, sgu_ln_b, sgu_w, sgu_b, od_w_in, od_w_out, pool_w, pool_scale, sconv_w):
    batch, seq, d = x.shape
    depth = ada_w.shape[0]
    alpha = (2 * depth) ** 0.25
    last_cross = 2 * ((depth - 1) // 2)
    heads = rwkv_rk.shape[1]
    width_a = heads * HEAD
    r_decay, r_aaa, r_gate = rwkv_w2.shape[2], rwkv_a2.shape[2], rwkv_g2.shape[1]
    in_a = 3 * width_a + 2 * r_decay + 2 * r_aaa + r_gate
    lay = _even_layout(width_a, r_decay, r_aaa, r_gate)

    rows = _round_up(batch + 1, SUBLANES)
    cc = jnp.zeros((rows, d), F32).at[:batch].set(c).at[batch].set(c_ctx)
    mod4 = _modulation_all(cc, ada_w, ada_b).reshape(depth, rows, 1, 6 * d)
    lng4 = ln_g.reshape(depth, 2, 1, d)
    lnb4 = ln_b.reshape(depth, 2, 1, d)
    lat_row = lambda b: b
    ctx_row = lambda b: batch

    h_lat, h_ctx = x, ctx
    a_lat = _modulate(x, mod4, 0, lat_row, 1, 0)
    a_ctx = _modulate(ctx, mod4, 0, ctx_row, 1, 0)
    for i in range(depth):
        ctx_in = i <= last_cross
        ctx_out = i < last_cross
        j = i // 2
        mix_ctx = None
        if i % 2 == 0:
            w_in = ev_w_in[j].astype(BF16)
            w_a = _permute_rwkv_cols(w_in[:, :in_a], lay, r_decay, r_aaa, r_gate)
            w_b = w_in[:, in_a:]
            w_out = ev_w_out[j].astype(BF16)
            rw = dict(
                shift=_permute_rwkv_cols(ev_shift[j], lay, r_decay, r_aaa, r_gate),
                w0=rwkv_w0[j], a0=rwkv_a0[j],
                w2=_pad_rows(rwkv_w2[j], lay["rp"]).astype(BF16),
                a2=_pad_rows(rwkv_a2[j], lay["rp"]).astype(BF16),
                g2=_pad_rows(rwkv_g2[j], lay["rgp"]).astype(BF16),
                kk=rwkv_kk[j].reshape(1, width_a), ka=rwkv_ka[j].reshape(1, width_a),
                rk=rwkv_rk[j].reshape(1, width_a))
            lnx_g = rwkv_lnx_g[j].reshape(1, width_a)
            lnx_b = rwkv_lnx_b[j].reshape(1, width_a)
            sgu_g = sgu_ln_g[j].reshape(1, -1)
            sgu_bb = sgu_ln_b[j].reshape(1, -1)
            sgu_ws = sgu_w[j].astype(BF16)
            sgu_bt = sgu_b[j].T

            st_lat = _rwkv_prep(_matmul([a_lat], w_a), lay, rw)
            st_ctx = _rwkv_prep(_matmul([a_ctx], w_a), lay, rw)
            s0 = jnp.zeros((batch, width_a // LANES, LANES, LANES), F32)
            yc_f, sc_f = _wkv_scan(st_ctx, 0, s0, False)
            yl_f, _ = _wkv_scan(st_lat, 0, sc_f, False)
            yc_b, sc_b = _wkv_scan(st_ctx, 1, s0, True)
            yl_b, _ = _wkv_scan(st_lat, 1, sc_b, True)
            ya_lat = _rwkv_out(yl_f, yl_b, st_lat, lnx_g, lnx_b)
            yb_lat = _sgu(_matmul([a_lat], w_b, BF16), sgu_g, sgu_bb, sgu_ws, sgu_bt)
            mix_lat = _matmul([ya_lat, yb_lat], w_out)
            if ctx_out:
                ya_ctx = _rwkv_out(yc_f, yc_b, st_ctx, lnx_g, lnx_b)
                yb_ctx = _sgu(_matmul([a_ctx], w_b, BF16), sgu_g, sgu_bb, sgu_ws, sgu_bt)
                mix_ctx = _matmul([ya_ctx, yb_ctx], w_out)
        else:
            w_in = od_w_in[j].astype(BF16)
            w_out = od_w_out[j].astype(BF16)
            pw = pool_w[j].astype(BF16)
            ps = pool_scale[j].reshape(1, -1)
            mix_lat = _matmul([_odd_mix(_matmul([a_lat], w_in, BF16), pw, ps, sconv_w[j], GRID_W)], w_out)
            if ctx_out:
                mix_ctx = _matmul([_odd_mix(_matmul([a_ctx], w_in, BF16), pw, ps, sconv_w[j], ctx.shape[1])], w_out)

        w_up = ffn_w_up[i].astype(BF16)
        w_down = ffn_w_down[i].astype(BF16)
        nxt = (i + 1, 1, 0) if i + 1 < depth else None
        h_lat, a2 = _ln_residual(h_lat, mix_lat, mod4, lng4, lnb4, lat_row, alpha, i, 2, 0, (i, 4, 3))
        f_lat = _matmul([_ffn_mid(_matmul([a2], w_up, BF16), ffn_conv[i], GRID_W)], w_down)
        h_lat, a_lat = _ln_residual(h_lat, f_lat, mod4, lng4, lnb4, lat_row, alpha, i, 5, 1, nxt)
        if ctx_out:
            h_ctx, a2 = _ln_residual(h_ctx, mix_ctx, mod4, lng4, lnb4, ctx_row, alpha, i, 2, 0, (i, 4, 3))
            f_ctx = _matmul([_ffn_mid(_matmul([a2], w_up, BF16), ffn_conv[i], 1)], w_down)
            h_ctx, a_ctx = _ln_residual(h_ctx, f_ctx, mod4, lng4, lnb4, ctx_row, alpha, i, 5, 1, nxt)
    return h_lat
```

```python
import functools
import math

import jax
import jax.numpy as jnp
from jax import lax
from jax.experimental import pallas as pl
from jax.experimental.pallas import tpu as pltpu

GRID_W = 64
POOL_WINDOWS = (2, 4, 8, 16)
LN_EPS = 1e-6
GN_EPS = 64e-5
HEAD = 64
SCAN_CHUNK = 64
LANES = 128
SUBLANES = 8
VMEM_LIMIT_BYTES = 56 * 1024 * 1024

F32 = jnp.float32
BF16 = jnp.bfloat16


def _round_up(n, m):
    return (n + m - 1) // m * m


def _params(*sem):
    return pltpu.CompilerParams(dimension_semantics=sem, vmem_limit_bytes=VMEM_LIMIT_BYTES)


def _sigmoid(z):
    return 1.0 / (1.0 + jnp.exp(-z))


def _softplus(z):
    return jnp.maximum(z, 0.0) + jnp.log1p(jnp.exp(-jnp.abs(z)))


def _mod_kernel(c_ref, w_ref, b_ref, o_ref):
    c = c_ref[...]
    s = (c * _sigmoid(c)).astype(BF16)
    o_ref[...] = jnp.dot(s, w_ref[...].astype(BF16), preferred_element_type=F32) + b_ref[...]


def _modulation_all(cc, ada_w, ada_b):
    depth, d, n = ada_w.shape
    r = cc.shape[0]
    tn = min(512, n)
    return pl.pallas_call(
        _mod_kernel,
        grid=(depth, n // tn),
        in_specs=[pl.BlockSpec((r, d), lambda l, j: (0, 0)),
                  pl.BlockSpec((None, d, tn), lambda l, j: (l, 0, j)),
                  pl.BlockSpec((None, 1, tn), lambda l, j: (l, 0, j))],
        out_specs=pl.BlockSpec((None, r, tn), lambda l, j: (l, 0, j)),
        out_shape=jax.ShapeDtypeStruct((depth, r, n), F32),
        compiler_params=_params("parallel", "parallel"),
        name="adaln_modulation",
    )(cc, ada_w, ada_b.reshape(depth, 1, n))


def _vec_spec(d, layer, part, row):
    return pl.BlockSpec((None, None, 1, d), lambda b, t: (layer, row(b), 0, part))


def _modulate_kernel(x_ref, sc_ref, sh_ref, a_ref):
    a_ref[...] = (x_ref[...] * (1.0 + sc_ref[...]) + sh_ref[...]).astype(a_ref.dtype)


def _modulate(x, mod4, layer, row, sc_part, sh_part):
    b, t, d = x.shape
    tt = min(256, t)
    return pl.pallas_call(
        _modulate_kernel,
        grid=(b, t // tt),
        in_specs=[pl.BlockSpec((None, tt, d), lambda bi, ti: (bi, ti, 0)),
                  _vec_spec(d, layer, sc_part, row), _vec_spec(d, layer, sh_part, row)],
        out_specs=pl.BlockSpec((None, tt, d), lambda bi, ti: (bi, ti, 0)),
        out_shape=jax.ShapeDtypeStruct((b, t, d), BF16),
        compiler_params=_params("parallel", "parallel"),
        name="modulate",
    )(x, mod4, mod4)


def _ln_kernel(*refs, alpha, with_next):
    if with_next:
        h_ref, f_ref, gate_ref, g_ref, b_ref, sc_ref, sh_ref, hn_ref, a_ref = refs
    else:
        h_ref, f_ref, gate_ref, g_ref, b_ref, hn_ref = refs
    z = alpha * h_ref[...] + gate_ref[...] * f_ref[...]
    mu = jnp.mean(z, axis=-1, keepdims=True)
    zc = z - mu
    var = jnp.mean(zc * zc, axis=-1, keepdims=True)
    y = zc * lax.rsqrt(var + LN_EPS) * g_ref[...] + b_ref[...]
    hn_ref[...] = y
    if with_next:
        a_ref[...] = (y * (1.0 + sc_ref[...]) + sh_ref[...]).astype(a_ref.dtype)


def _ln_residual(h, f, mod4, lng4, lnb4, row, alpha, layer, gate_part, ln_idx, nxt):
    b, t, d = h.shape
    tt = min(256, t)
    tok = pl.BlockSpec((None, tt, d), lambda bi, ti: (bi, ti, 0))
    ln_spec = pl.BlockSpec((None, None, 1, d), lambda bi, ti: (layer, ln_idx, 0, 0))
    in_specs = [tok, tok, _vec_spec(d, layer, gate_part, row), ln_spec, ln_spec]
    args = [h, f.reshape(b, t, d), mod4, lng4, lnb4]
    out_specs = [tok]
    out_shape = [jax.ShapeDtypeStruct((b, t, d), F32)]
    if nxt is not None:
        in_specs += [_vec_spec(d, nxt[0], nxt[1], row), _vec_spec(d, nxt[0], nxt[2], row)]
        args += [mod4, mod4]
        out_specs.append(tok)
        out_shape.append(jax.ShapeDtypeStruct((b, t, d), BF16))
    out = pl.pallas_call(
        functools.partial(_ln_kernel, alpha=alpha, with_next=nxt is not None),
        grid=(b, t // tt),
        in_specs=in_specs, out_specs=out_specs, out_shape=out_shape,
        compiler_params=_params("parallel", "parallel"),
        name="deepnorm_residual",
    )(*args)
    return (out[0], out[1]) if nxt is not None else (out[0], None)


def _mm_kernel(*refs):
    *a_refs, w_ref, o_ref = refs
    acc = None
    k0 = 0
    for a_ref in a_refs:
        k = a_ref.shape[1]
        part = jnp.dot(a_ref[...], w_ref[k0:k0 + k, :], preferred_element_type=F32)
        acc = part if acc is None else acc + part
        k0 += k
    o_ref[...] = acc.astype(o_ref.dtype)


def _matmul_tiles(m, k, n, out_bytes):
    budget = VMEM_LIMIT_BYTES - 8 * 1024 * 1024
    best = None
    for tm in (1024, 512, 256, 128, 64, 32, 16):
        if m % tm:
            continue
        for tn in (2048, 1792, 1536, 1280, 1024, 768, 512, 256, 128):
            if n % tn:
                continue
            need = 2 * tm * k * 2 + 2 * k * tn * 2 + 2 * tm * tn * out_bytes + tm * tn * 4
            if need <= budget and (best is None or tm * tn > best[0] * best[1]):
                best = (tm, tn)
    assert best is not None, (m, k, n)
    return best


def _matmul(a_parts, w, layer, out_dtype=F32):
    lead = a_parts[0].shape[:-1]
    a2 = [a.reshape(-1, a.shape[-1]) for a in a_parts]
    m = a2[0].shape[0]
    _, k, n = w.shape
    assert sum(a.shape[1] for a in a2) == k
    tm, tn = _matmul_tiles(m, k, n, jnp.dtype(out_dtype).itemsize)
    out = pl.pallas_call(
        _mm_kernel,
        grid=(n // tn, m // tm),
        in_specs=[pl.BlockSpec((tm, a.shape[1]), lambda j, i: (i, 0)) for a in a2]
        + [pl.BlockSpec((None, k, tn), lambda j, i: (layer, 0, j))],
        out_specs=pl.BlockSpec((tm, tn), lambda j, i: (i, j)),
        out_shape=jax.ShapeDtypeStruct((m, n), out_dtype),
        compiler_params=_params("parallel", "parallel"),
        name="projection",
    )(*a2, w)
    return out.reshape(lead + (n,))


def _head_sum(x):
    li = (lax.broadcasted_iota(jnp.int32, (2 * LANES, LANES), 0) % LANES) // HEAD
    lj = lax.broadcasted_iota(jnp.int32, (2 * LANES, LANES), 1) // HEAD
    ones = (li == lj).astype(BF16)
    hi = x.astype(BF16)
    lo = (x - hi.astype(F32)).astype(BF16)
    return jnp.dot(jnp.concatenate([hi, lo], axis=1), ones, preferred_element_type=F32)


def _prep_kernel(p_ref, pp_ref, pn_ref, shw_ref, w0_ref, w2_ref, a0_ref, a2_ref, g2_ref,
                 kkw_ref, ka_ref, rk_ref,
                 r_o, v_o, kk_o, lwf_o, lwb_o, kf_o, kb_o, bf_o, bb_o, g_o, bv_o,
                 *, width, rp, rgp):
    i = pl.program_id(1)
    last = pl.num_programs(1) - 1
    tt = p_ref.shape[0]
    row = lax.broadcasted_iota(jnp.int32, (tt, 1), 0)

    def shifted(c0, c1):
        x = p_ref[:, c0:c1].astype(F32)
        hs = pp_ref.shape[0]
        prev_row = jnp.where(i > 0, pp_ref[:, c0:c1].astype(F32)[hs - 1:hs], 0.0)
        next_row = jnp.where(i < last, pn_ref[:, c0:c1].astype(F32)[0:1], 0.0)
        xp = jnp.where(row == 0, prev_row, pltpu.roll(x, 1, 0))
        xn = jnp.where(row == tt - 1, next_row, pltpu.roll(x, tt - 1, 0))
        return shw_ref[0:1, c0:c1] * xp + shw_ref[1:2, c0:c1] * x + shw_ref[2:3, c0:c1] * xn

    hp_count = width // LANES
    o = 3 * width
    kkw = kkw_ref[...]
    ka = ka_ref[...]
    rk = rk_ref[...]
    g = jnp.dot(_sigmoid(shifted(o + 4 * rp, o + 4 * rp + rgp)).astype(BF16), g2_ref[...],
                preferred_element_type=F32)
    g_o[...] = g.astype(g_o.dtype)
    rates = []
    for d in range(2):
        wd = shifted(o + d * rp, o + (d + 1) * rp)
        ad = shifted(o + (2 + d) * rp, o + (3 + d) * rp)
        wl = w0_ref[d:d + 1, :] + jnp.dot(jnp.tanh(wd).astype(BF16), w2_ref[d], preferred_element_type=F32)
        w_log = -_softplus(-wl) - 0.5
        lw = -jnp.exp(w_log)
        ar = _sigmoid(a0_ref[d:d + 1, :] + jnp.dot(ad.astype(BF16), a2_ref[d], preferred_element_type=F32))
        rates.append(ar)
        lw_o = lwf_o if d == 0 else lwb_o
        for hp in range(hp_count):
            lw_o[hp] = lw[:, hp * LANES:(hp + 1) * LANES]
    for hp in range(hp_count):
        c0, c1 = hp * LANES, (hp + 1) * LANES
        r = shifted(c0, c1)
        k = shifted(width + c0, width + c1)
        v = shifted(2 * width + c0, 2 * width + c1)
        kkr = k * kkw[:, c0:c1]
        kk = kkr / jnp.maximum(jnp.sqrt(_head_sum(kkr * kkr)), 1e-12)
        kd0 = k * (1.0 + (rates[0][:, c0:c1] - 1.0) * ka[:, c0:c1])
        kd1 = k * (1.0 + (rates[1][:, c0:c1] - 1.0) * ka[:, c0:c1])
        r_o[hp] = r.astype(r_o.dtype)
        v_o[hp] = v.astype(v_o.dtype)
        kk_o[hp] = kk.astype(kk_o.dtype)
        kf_o[hp] = kd0.astype(kf_o.dtype)
        kb_o[hp] = kd1.astype(kb_o.dtype)
        bf_o[hp] = (kk * rates[0][:, c0:c1]).astype(bf_o.dtype)
        bb_o[hp] = (kk * rates[1][:, c0:c1]).astype(bb_o.dtype)
        bonus = _head_sum(r * (0.5 * (kd0 + kd1)) * rk[:, c0:c1])
        bv_o[:, c0:c1] = (bonus * v).astype(bv_o.dtype)


def _rwkv_prep(pa, lay, rw):
    b, t, na = pa.shape
    width, rp, rgp = lay["width"], lay["rp"], lay["rgp"]
    hp = width // LANES
    tt = min(128, t)
    hs = SUBLANES * (4 // pa.dtype.itemsize)
    nb = t // hs
    tok = lambda w: pl.BlockSpec((None, tt, w), lambda bi, ti: (bi, ti, 0))
    halo_prev = pl.BlockSpec((None, hs, na), lambda bi, ti: (bi, jnp.maximum(ti * (tt // hs) - 1, 0), 0))
    halo_next = pl.BlockSpec((None, hs, na), lambda bi, ti: (bi, jnp.minimum((ti + 1) * (tt // hs), nb - 1), 0))
    full = lambda a: pl.BlockSpec(a.shape, lambda bi, ti: (0,) * a.ndim)
    pair = pl.BlockSpec((None, hp, tt, LANES), lambda bi, ti: (bi, 0, ti, 0))
    pair_shapes = [jax.ShapeDtypeStruct((b, hp, t, LANES), dt) for dt in (BF16, BF16, BF16, F32, F32, BF16, BF16, BF16, BF16)]
    consts = [rw["shift"], rw["w0"], rw["w2"], rw["a0"], rw["a2"], rw["g2"], rw["kk"], rw["ka"], rw["rk"]]
    outs = pl.pallas_call(
        functools.partial(_prep_kernel, width=width, rp=rp, rgp=rgp),
        grid=(b, t // tt),
        in_specs=[tok(na), halo_prev, halo_next] + [full(a) for a in consts],
        out_specs=[pair] * 9 + [tok(width), tok(width)],
        out_shape=pair_shapes + [jax.ShapeDtypeStruct((b, t, width), BF16)] * 2,
        compiler_params=_params("parallel", "parallel"),
        name="rwkv_streams",
    )(pa, pa, pa, *consts)
    names = ("r", "v", "kk", "lwf", "lwb", "kf", "kb", "bf", "bb", "g", "bv")
    return dict(zip(names, outs))


def _bdot(a, b):
    return jnp.dot(a.astype(BF16), b.astype(BF16), preferred_element_type=F32)


def _bdot_nt(a, b):
    return lax.dot_general(a.astype(BF16), b.astype(BF16), (((1,), (1,)), ((), ())), preferred_element_type=F32)


def _bdot_tn(a, b):
    return lax.dot_general(a.astype(BF16), b.astype(BF16), (((0,), (0,)), ((), ())), preferred_element_type=F32)


def _running_sum(upto, x):
    tri = upto.astype(BF16)
    hi = x.astype(BF16)
    rest = x - hi.astype(F32)
    mid = rest.astype(BF16)
    lo = (rest - mid.astype(F32)).astype(BF16)
    dot = lambda p: jnp.dot(tri, p, preferred_element_type=F32)
    return dot(hi) + dot(mid) + dot(lo)


def _wkv_kernel(r_ref, v_ref, kk_ref, lw_ref, k_ref, b_ref, s0_ref, y_ref, s_ref, *, reverse):
    c = pl.program_id(1)

    @pl.when(c == 0)
    def _():
        s_ref[...] = s0_ref[...]

    hp_count, ch, _ = r_ref.shape
    pairs = range(hp_count)
    ti = lax.broadcasted_iota(jnp.int32, (ch, ch), 0)
    tj = lax.broadcasted_iota(jnp.int32, (ch, ch), 1)
    upto = ((tj >= ti) if reverse else (tj <= ti)).astype(F32)
    ei = lax.broadcasted_iota(jnp.int32, (ch, LANES), 0)
    ej = lax.broadcasted_iota(jnp.int32, (ch, LANES), 1)
    eye2 = (ei == ej % HEAD).astype(F32)
    gi = lax.broadcasted_iota(jnp.int32, (2 * ch, 2 * LANES), 0)
    gj = lax.broadcasted_iota(jnp.int32, (2 * ch, 2 * LANES), 1) % ch
    it = gi % ch
    earlier = (gj > it) if reverse else (gj < it)
    gmask = earlier | ((gj == it) & (gi >= ch))
    first = lax.broadcasted_iota(jnp.int32, (1, LANES), 1) < HEAD
    si = lax.broadcasted_iota(jnp.int32, (LANES, LANES), 0) // HEAD
    sj = lax.broadcasted_iota(jnp.int32, (LANES, LANES), 1) // HEAD
    same_head = si == sj
    doublings = int(math.log2(ch)) - 1
    zero = jnp.zeros((), BF16)

    def split(x):
        x = x.astype(BF16)
        return jnp.concatenate([jnp.where(first, x, zero), jnp.where(first, zero, x)], axis=0)

    ar, brows, bke, vs, wcs = [], [], [], [], []
    for hp in pairs:
        lw = lw_ref[hp]
        cl = _running_sum(upto, lw)
        tot = jnp.sum(lw, axis=0, keepdims=True)
        e_in = jnp.exp(cl)
        e_out = jnp.exp(-cl)
        e_end = jnp.exp(tot - cl)
        a_t = -kk_ref[hp].astype(F32) * jnp.exp(cl - lw)
        b = b_ref[hp].astype(F32)
        k = k_ref[hp].astype(F32)
        ar.append(jnp.concatenate([a_t, r_ref[hp].astype(F32) * e_in], axis=0).astype(BF16))
        brows.append(jnp.concatenate([split(b * e_out), split(k * e_out)], axis=0))
        bke.append(jnp.concatenate([b * e_end, k * e_end], axis=0).astype(BF16))
        vs.append(v_ref[hp])
        wcs.append(jnp.exp(tot))
    state = [s_ref[hp] for hp in pairs]
    g = [jnp.where(gmask, _bdot_nt(ar[p], brows[p]), 0.0) for p in pairs]
    ars = [_bdot_nt(ar[p], state[p]) for p in pairs]
    vsplit = [split(vs[p]) for p in pairs]
    rhs = [ars[p][:ch] + _bdot(g[p][:ch, LANES:], vsplit[p]) for p in pairs]
    inv = [eye2 + g[p][:ch, :LANES] for p in pairs]
    pw = [_bdot(g[p][:ch, :LANES], split(g[p][:ch, :LANES])) for p in pairs]
    for _ in range(doublings - 1):
        prod = [_bdot(pw[p], jnp.concatenate([split(inv[p]), split(pw[p])], axis=1)) for p in pairs]
        inv = [inv[p] + prod[p][:, :LANES] for p in pairs]
        pw = [prod[p][:, LANES:] for p in pairs]
    inv = [inv[p] + _bdot(pw[p], split(inv[p])) for p in pairs]
    u = [_bdot(inv[p], split(rhs[p])) for p in pairs]
    ys = [ars[p][ch:] + _bdot(g[p][ch:, :], jnp.concatenate([split(u[p]), vsplit[p]], axis=0)) for p in pairs]
    upd = [_bdot_tn(jnp.concatenate([u[p].astype(BF16), vs[p].astype(BF16)], axis=0), bke[p]) for p in pairs]
    for hp in pairs:
        y_ref[hp] = ys[hp]
        s_ref[hp] = state[hp] * wcs[hp] + jnp.where(same_head, upd[hp], 0.0)


def _wkv_scan(st, d, s0, reverse):
    r = st["r"]
    b, hp, t, _ = r.shape
    ch = min(SCAN_CHUNK, t)
    nc = t // ch
    cidx = (lambda c: nc - 1 - c) if reverse else (lambda c: c)
    blk = pl.BlockSpec((None, hp, ch, LANES), lambda bi, c: (bi, 0, cidx(c), 0))
    st_spec = pl.BlockSpec((None, hp, LANES, LANES), lambda bi, c: (bi, 0, 0, 0))
    lw, k, bb = (st["lwf"], st["kf"], st["bf"]) if d == 0 else (st["lwb"], st["kb"], st["bb"])
    y, s_last = pl.pallas_call(
        functools.partial(_wkv_kernel, reverse=reverse),
        grid=(b, nc),
        in_specs=[blk] * 6 + [st_spec],
        out_specs=[blk, st_spec],
        out_shape=[jax.ShapeDtypeStruct((b, hp, t, LANES), F32),
                   jax.ShapeDtypeStruct((b, hp, LANES, LANES), F32)],
        compiler_params=_params("parallel", "arbitrary"),
        name="wkv_scan",
    )(r, st["v"], st["kk"], lw, k, bb, s0)
    return y, s_last


def _rwkv_out_kernel(yf_ref, yb_ref, bv_ref, g_ref, lng_ref, lnb_ref, o_ref):
    hp_count = yf_ref.shape[0]
    for hp in range(hp_count):
        c0, c1 = hp * LANES, (hp + 1) * LANES
        y = yf_ref[hp] + yb_ref[hp]
        mu = _head_sum(y) * (1.0 / HEAD)
        yc = y - mu
        var = _head_sum(yc * yc) * (1.0 / HEAD)
        yn = yc * lax.rsqrt(var + GN_EPS) * lng_ref[:, c0:c1] + lnb_ref[:, c0:c1]
        o_ref[:, c0:c1] = ((yn + bv_ref[:, c0:c1].astype(F32)) * g_ref[:, c0:c1].astype(F32)).astype(o_ref.dtype)


def _rwkv_out(yf, yb, st, lnx_g, lnx_b):
    b, hp, t, _ = yf.shape
    width = hp * LANES
    tt = min(256, t)
    pair = pl.BlockSpec((None, hp, tt, LANES), lambda bi, ti: (bi, 0, ti, 0))
    tok = pl.BlockSpec((None, tt, width), lambda bi, ti: (bi, ti, 0))
    vec = pl.BlockSpec((1, width), lambda bi, ti: (0, 0))
    return pl.pallas_call(
        _rwkv_out_kernel,
        grid=(b, t // tt),
        in_specs=[pair, pair, tok, tok, vec, vec],
        out_specs=tok,
        out_shape=jax.ShapeDtypeStruct((b, t, width), BF16),
        compiler_params=_params("parallel", "parallel"),
        name="rwkv_out",
    )(yf, yb, st["bv"], st["g"], lnx_g, lnx_b)


def _gelu(z):
    return 0.5 * z * (1.0 + lax.erf(z * (2.0 ** -0.5)))


def _sgu_kernel(u_ref, v_ref, lng_ref, lnb_ref, ws_ref, bs_ref, o_ref):
    heads = ws_ref.shape[0]
    width = u_ref.shape[1]
    hw = width // heads
    v = _gelu(v_ref[...].astype(F32))
    mu = jnp.mean(v, axis=-1, keepdims=True)
    vc = v - mu
    var = jnp.mean(vc * vc, axis=-1, keepdims=True)
    vn = (vc * lax.rsqrt(var + LN_EPS) * lng_ref[...] + lnb_ref[...]).astype(BF16)
    for h in range(heads):
        c0, c1 = h * hw, (h + 1) * hw
        s = jnp.dot(ws_ref[h], vn[:, c0:c1], preferred_element_type=F32) + bs_ref[:, h:h + 1]
        o_ref[:, c0:c1] = (_gelu(u_ref[:, c0:c1].astype(F32)) * s).astype(o_ref.dtype)


def _sgu(pb, ln_g, ln_b, ws, bs_t):
    b, t, w2 = pb.shape
    width = w2 // 2
    heads, chunk, _ = ws.shape
    tok_u = pl.BlockSpec((None, chunk, width), lambda bi, ti: (bi, ti, 0))
    tok_v = pl.BlockSpec((None, chunk, width), lambda bi, ti: (bi, ti, 1))
    full = lambda a: pl.BlockSpec(a.shape, lambda bi, ti: (0,) * a.ndim)
    return pl.pallas_call(
        _sgu_kernel,
        grid=(b, t // chunk),
        in_specs=[tok_u, tok_v, full(ln_g), full(ln_b), full(ws), full(bs_t)],
        out_specs=tok_u,
        out_shape=jax.ShapeDtypeStruct((b, t, width), BF16),
        compiler_params=_params("parallel", "parallel"),
        name="spatial_gating",
    )(pb, pb, ln_g, ln_b, ws, bs_t)


def _odd_kernel(xc_ref, bg_ref, cg_ref, xd_ref, pw_ref, ps_ref, cw_ref, o_ref, *, seg):
    tt, width = xc_ref.shape
    groups = pw_ref.shape[0]
    gc = width // groups
    ti = lax.broadcasted_iota(jnp.int32, (tt, tt), 0)
    tj = lax.broadcasted_iota(jnp.int32, (tt, tt), 1)
    same_seg = (ti // seg) == (tj // seg)
    pos = lax.broadcasted_iota(jnp.int32, (tt, 1), 0) % seg
    for gi in range(groups):
        win = POOL_WINDOWS[gi]
        lo = win // 2
        hi = win - 1 - lo
        band = (same_seg & (tj >= ti - lo) & (tj <= ti + hi)).astype(BF16)
        count = (jnp.minimum(pos + hi, seg - 1) - jnp.maximum(pos - lo, 0) + 1).astype(F32)
        c0, c1 = gi * gc, (gi + 1) * gc
        x = xc_ref[:, c0:c1]
        mean = jnp.dot(band, x, preferred_element_type=F32) / count
        p = (mean - x.astype(F32)).astype(BF16)
        y = jnp.dot(p, pw_ref[gi], preferred_element_type=F32) * ps_ref[:, c0:c1]
        o_ref[:, c0:c1] = y.astype(o_ref.dtype)
    z = cg_ref[...].astype(F32) * xd_ref[...].astype(F32)
    zp = jnp.where(pos == 0, 0.0, pltpu.roll(z, 1, 0))
    zn = jnp.where(pos == seg - 1, 0.0, pltpu.roll(z, tt - 1, 0))
    conv = cw_ref[0:1, :] * zp + cw_ref[1:2, :] * z + cw_ref[2:3, :] * zn
    o_ref[:, width:] = (bg_ref[...].astype(F32) * conv).astype(o_ref.dtype)


def _odd_mix(p, pool_w, pool_scale, sconv_w, seg):
    b, t, w4 = p.shape
    width = w4 // 4
    tt = min(max(256, seg), t)
    assert tt % seg == 0 and t % tt == 0
    col = lambda ci: pl.BlockSpec((None, tt, width), lambda bi, ti: (bi, ti, ci))
    full = lambda a: pl.BlockSpec(a.shape, lambda bi, ti: (0,) * a.ndim)
    return pl.pallas_call(
        functools.partial(_odd_kernel, seg=seg),
        grid=(b, t // tt),
        in_specs=[col(0), col(1), col(2), col(3), full(pool_w), full(pool_scale), full(sconv_w)],
        out_specs=pl.BlockSpec((None, tt, 2 * width), lambda bi, ti: (bi, ti, 0)),
        out_shape=jax.ShapeDtypeStruct((b, t, 2 * width), BF16),
        compiler_params=_params("parallel", "parallel"),
        name="pool_shortconv",
    )(p, p, p, p, pool_w, pool_scale, sconv_w)


def _ffn_mid_kernel(g_ref, gp_ref, gn_ref, val_ref, cw_ref, o_ref, *, shift):
    i = pl.program_id(1)
    last = pl.num_programs(1) - 1
    tt = g_ref.shape[0]
    hs = gp_ref.shape[0]
    g = g_ref[...].astype(F32)
    if hs == shift:
        prev_blk = jnp.where(i > 0, gp_ref[...].astype(F32), 0.0)
        next_blk = jnp.where(i < last, gn_ref[...].astype(F32), 0.0)
        if tt > shift:
            gp = jnp.concatenate([prev_blk, g[:tt - shift]], axis=0)
            gn = jnp.concatenate([g[shift:], next_blk], axis=0)
        else:
            gp, gn = prev_blk, next_blk
    else:
        row = lax.broadcasted_iota(jnp.int32, (tt, 1), 0)
        prev_row = jnp.where(i > 0, gp_ref[...].astype(F32)[hs - 1:hs, :], 0.0)
        next_row = jnp.where(i < last, gn_ref[...].astype(F32)[0:1, :], 0.0)
        gp = jnp.where(row == 0, prev_row, pltpu.roll(g, 1, 0))
        gn = jnp.where(row == tt - 1, next_row, pltpu.roll(g, tt - 1, 0))
    conv = cw_ref[0:1, :] * gp + cw_ref[1:2, :] * g + cw_ref[2:3, :] * gn
    o_ref[...] = (conv * _sigmoid(conv) * val_ref[...].astype(F32)).astype(o_ref.dtype)


def _ffn_mid(up, conv_w, shift):
    b, t, f2 = up.shape
    f = f2 // 2
    tt = min(512, t)
    tc = f
    for cand in (1280, 1024, 512, 256, 128):
        if f % cand == 0:
            tc = cand
            break
    ncol = f // tc
    row_tile = SUBLANES * (4 // up.dtype.itemsize)
    hs = shift if shift % row_tile == 0 else row_tile
    assert shift == 1 or (hs == shift and tt % shift == 0)
    nh = t // hs
    main = pl.BlockSpec((None, tt, tc), lambda bi, ti, ci: (bi, ti, ci))
    prev = pl.BlockSpec((None, hs, tc), lambda bi, ti, ci: (bi, jnp.maximum(ti * (tt // hs) - 1, 0), ci))
    nxt = pl.BlockSpec((None, hs, tc), lambda bi, ti, ci: (bi, jnp.minimum((ti + 1) * (tt // hs), nh - 1), ci))
    val = pl.BlockSpec((None, tt, tc), lambda bi, ti, ci: (bi, ti, ci + ncol))
    cw = pl.BlockSpec((3, tc), lambda bi, ti, ci: (0, ci))
    return pl.pallas_call(
        functools.partial(_ffn_mid_kernel, shift=shift),
        grid=(b, t // tt, ncol),
        in_specs=[main, prev, nxt, val, cw],
        out_specs=main,
        out_shape=jax.ShapeDtypeStruct((b, t, f), BF16),
        compiler_params=_params("parallel", "parallel", "parallel"),
        name="convffn_gate",
    )(up, up, up, up, conv_w)


def _pad_cols(w, n):
    return jnp.pad(w, ((0, 0),) * (w.ndim - 1) + ((0, n - w.shape[-1]),))


def _even_layout(width, r_decay, r_aaa, r_gate):
    rp = _round_up(max(r_decay, r_aaa), LANES)
    rgp = _round_up(r_gate, LANES)
    na = _round_up(3 * width + 4 * rp + rgp, 1024)
    return dict(width=width, rp=rp, rgp=rgp, na=na)


def _permute_rwkv_cols(w, lay, r_decay, r_aaa, r_gate):
    width, rp, rgp, na = lay["width"], lay["rp"], lay["rgp"], lay["na"]
    o = 3 * width
    parts = [w[..., :o]]
    for seg in (r_decay, r_decay, r_aaa, r_aaa):
        parts.append(_pad_cols(w[..., o:o + seg], rp))
        o += seg
    parts.append(_pad_cols(w[..., o:o + r_gate], rgp))
    return _pad_cols(jnp.concatenate(parts, axis=-1), na)


def _pad_rows(w, n):
    pad = [(0, 0)] * w.ndim
    pad[-2] = (0, n - w.shape[-2])
    return jnp.pad(w, pad)


def kernel(x, c, ctx, c_ctx, ada_w, ada_b, ln_g, ln_b, ffn_w_up, ffn_conv, ffn_w_down, ev_w_in, ev_w_out, ev_shift, rwkv_w0, rwkv_w2, rwkv_a0, rwkv_a2, rwkv_g2, rwkv_kk, rwkv_ka, rwkv_rk, rwkv_lnx_g, rwkv_lnx_b, sgu_ln_g, sgu_ln_b, sgu_w, sgu_b, od_w_in, od_w_out, pool_w, pool_scale, sconv_w):
    batch, seq, d = x.shape
    depth = ada_w.shape[0]
    alpha = (2 * depth) ** 0.25
    last_cross = 2 * ((depth - 1) // 2)
    heads = rwkv_rk.shape[1]
    width_a = heads * HEAD
    r_decay, r_aaa, r_gate = rwkv_w2.shape[2], rwkv_a2.shape[2], rwkv_g2.shape[1]
    in_a = 3 * width_a + 2 * r_decay + 2 * r_aaa + r_gate
    lay = _even_layout(width_a, r_decay, r_aaa, r_gate)

    rows = _round_up(batch + 1, SUBLANES)
    cc = jnp.zeros((rows, d), F32).at[:batch].set(c).at[batch].set(c_ctx)
    mod4 = _modulation_all(cc, ada_w, ada_b).reshape(depth, rows, 1, 6 * d)
    lng4 = ln_g.reshape(depth, 2, 1, d)
    lnb4 = ln_b.reshape(depth, 2, 1, d)
    lat_row = lambda b: b
    ctx_row = lambda b: batch

    ev_in = ev_w_in.astype(BF16)
    w_a = _permute_rwkv_cols(ev_in[..., :in_a], lay, r_decay, r_aaa, r_gate)
    w_b = ev_in[..., in_a:]
    ev_out = ev_w_out.astype(BF16)
    od_in = od_w_in.astype(BF16)
    od_out = od_w_out.astype(BF16)
    w_up = ffn_w_up.astype(BF16)
    w_down = ffn_w_down.astype(BF16)
    shift_all = _permute_rwkv_cols(ev_shift, lay, r_decay, r_aaa, r_gate)

    h_lat, h_ctx = x, ctx
    a_lat = _modulate(x, mod4, 0, lat_row, 1, 0)
    a_ctx = _modulate(ctx, mod4, 0, ctx_row, 1, 0)
    for i in range(depth):
        ctx_in = i <= last_cross
        ctx_out = i < last_cross
        j = i // 2
        mix_ctx = None
        if i % 2 == 0:
            rw = dict(
                shift=shift_all[j],
                w0=rwkv_w0[j], a0=rwkv_a0[j],
                w2=_pad_rows(rwkv_w2[j], lay["rp"]).astype(BF16),
                a2=_pad_rows(rwkv_a2[j], lay["rp"]).astype(BF16),
                g2=_pad_rows(rwkv_g2[j], lay["rgp"]).astype(BF16),
                kk=rwkv_kk[j].reshape(1, width_a), ka=rwkv_ka[j].reshape(1, width_a),
                rk=rwkv_rk[j].reshape(1, width_a))
            lnx_g = rwkv_lnx_g[j].reshape(1, width_a)
            lnx_b = rwkv_lnx_b[j].reshape(1, width_a)
            sgu_g = sgu_ln_g[j].reshape(1, -1)
            sgu_bb = sgu_ln_b[j].reshape(1, -1)
            sgu_ws = sgu_w[j].astype(BF16)
            sgu_bt = sgu_b[j].T

            st_lat = _rwkv_prep(_matmul([a_lat], w_a, j, BF16), lay, rw)
            st_ctx = _rwkv_prep(_matmul([a_ctx], w_a, j, BF16), lay, rw)
            s0 = jnp.zeros((batch, width_a // LANES, LANES, LANES), F32)
            yc_f, sc_f = _wkv_scan(st_ctx, 0, s0, False)
            yl_f, _ = _wkv_scan(st_lat, 0, sc_f, False)
            yc_b, sc_b = _wkv_scan(st_ctx, 1, s0, True)
            yl_b, _ = _wkv_scan(st_lat, 1, sc_b, True)
            ya_lat = _rwkv_out(yl_f, yl_b, st_lat, lnx_g, lnx_b)
            yb_lat = _sgu(_matmul([a_lat], w_b, j, BF16), sgu_g, sgu_bb, sgu_ws, sgu_bt)
            mix_lat = _matmul([ya_lat, yb_lat], ev_out, j)
            if ctx_out:
                ya_ctx = _rwkv_out(yc_f, yc_b, st_ctx, lnx_g, lnx_b)
                yb_ctx = _sgu(_matmul([a_ctx], w_b, j, BF16), sgu_g, sgu_bb, sgu_ws, sgu_bt)
                mix_ctx = _matmul([ya_ctx, yb_ctx], ev_out, j)
        else:
            pw = pool_w[j].astype(BF16)
            ps = pool_scale[j].reshape(1, -1)
            mix_lat = _matmul([_odd_mix(_matmul([a_lat], od_in, j, BF16), pw, ps, sconv_w[j], GRID_W)], od_out, j)
            if ctx_out:
                mix_ctx = _matmul([_odd_mix(_matmul([a_ctx], od_in, j, BF16), pw, ps, sconv_w[j], ctx.shape[1])],
                                  od_out, j)

        nxt = (i + 1, 1, 0) if i + 1 < depth else None
        h_lat, a2 = _ln_residual(h_lat, mix_lat, mod4, lng4, lnb4, lat_row, alpha, i, 2, 0, (i, 4, 3))
        f_lat = _matmul([_ffn_mid(_matmul([a2], w_up, i, BF16), ffn_conv[i], GRID_W)], w_down, i)
        h_lat, a_lat = _ln_residual(h_lat, f_lat, mod4, lng4, lnb4, lat_row, alpha, i, 5, 1, nxt)
        if ctx_out:
            h_ctx, a2 = _ln_residual(h_ctx, mix_ctx, mod4, lng4, lnb4, ctx_row, alpha, i, 2, 0, (i, 4, 3))
            f_ctx = _matmul([_ffn_mid(_matmul([a2], w_up, i, BF16), ffn_conv[i], 1)], w_down, i)
            h_ctx, a_ctx = _ln_residual(h_ctx, f_ctx, mod4, lng4, lnb4, ctx_row, alpha, i, 5, 1, nxt)
    return h_lat
```

```python
import functools
import math

import jax
import jax.numpy as jnp
from jax import lax
from jax.experimental import pallas as pl
from jax.experimental.pallas import tpu as pltpu

GRID_W = 64
POOL_WINDOWS = (2, 4, 8, 16)
LN_EPS = 1e-6
GN_EPS = 64e-5
HEAD = 64
SCAN_CHUNK = 64
LANES = 128
SUBLANES = 8
VMEM_LIMIT_BYTES = 60 * 1024 * 1024

F32 = jnp.float32
BF16 = jnp.bfloat16


def _round_up(n, m):
    return (n + m - 1) // m * m


def _params(*sem):
    return pltpu.CompilerParams(dimension_semantics=sem, vmem_limit_bytes=VMEM_LIMIT_BYTES)


def _sigmoid(z):
    return 1.0 / (1.0 + jnp.exp(-z))


def _mod_kernel(c_ref, w_ref, b_ref, o_ref):
    c = c_ref[...]
    s = (c * _sigmoid(c)).astype(BF16)
    o_ref[...] = jnp.dot(s, w_ref[...].astype(BF16), preferred_element_type=F32) + b_ref[...]


def _modulation_all(cc, ada_w, ada_b):
    depth, d, n = ada_w.shape
    r = cc.shape[0]
    tn = min(512, n)
    return pl.pallas_call(
        _mod_kernel,
        grid=(depth, n // tn),
        in_specs=[pl.BlockSpec((r, d), lambda l, j: (0, 0)),
                  pl.BlockSpec((None, d, tn), lambda l, j: (l, 0, j)),
                  pl.BlockSpec((None, 1, tn), lambda l, j: (l, 0, j))],
        out_specs=pl.BlockSpec((None, r, tn), lambda l, j: (l, 0, j)),
        out_shape=jax.ShapeDtypeStruct((depth, r, n), F32),
        compiler_params=_params("parallel", "parallel"),
        name="adaln_modulation",
    )(cc, ada_w, ada_b.reshape(depth, 1, n))


def _vec_spec(d, layer, part, row):
    return pl.BlockSpec((None, None, 1, d), lambda b, t: (layer, row(b), 0, part))


def _modulate_kernel(x_ref, sc_ref, sh_ref, a_ref):
    a_ref[...] = (x_ref[...] * (1.0 + sc_ref[...]) + sh_ref[...]).astype(a_ref.dtype)


def _modulate(x, mod4, layer, row, sc_part, sh_part):
    b, t, d = x.shape
    tt = min(256, t)
    return pl.pallas_call(
        _modulate_kernel,
        grid=(b, t // tt),
        in_specs=[pl.BlockSpec((None, tt, d), lambda bi, ti: (bi, ti, 0)),
                  _vec_spec(d, layer, sc_part, row), _vec_spec(d, layer, sh_part, row)],
        out_specs=pl.BlockSpec((None, tt, d), lambda bi, ti: (bi, ti, 0)),
        out_shape=jax.ShapeDtypeStruct((b, t, d), BF16),
        compiler_params=_params("parallel", "parallel"),
        name="modulate",
    )(x, mod4, mod4)


def _ln_kernel(*refs, with_next):
    if with_next:
        z_ref, g_ref, b_ref, sc_ref, sh_ref, hn_ref, a_ref = refs
    else:
        z_ref, g_ref, b_ref, hn_ref = refs
    z = z_ref[...]
    mu = jnp.mean(z, axis=-1, keepdims=True)
    zc = z - mu
    var = jnp.mean(zc * zc, axis=-1, keepdims=True)
    y = zc * lax.rsqrt(var + LN_EPS) * g_ref[...] + b_ref[...]
    hn_ref[...] = y
    if with_next:
        a_ref[...] = (y * (1.0 + sc_ref[...]) + sh_ref[...]).astype(a_ref.dtype)


def _ln_residual(z, mod4, lng4, lnb4, row, layer, ln_idx, nxt):
    b, t, d = z.shape
    tt = min(256, t)
    tok = pl.BlockSpec((None, tt, d), lambda bi, ti: (bi, ti, 0))
    ln_spec = pl.BlockSpec((None, None, 1, d), lambda bi, ti: (layer, ln_idx, 0, 0))
    in_specs = [tok, ln_spec, ln_spec]
    args = [z, lng4, lnb4]
    out_specs = [tok]
    out_shape = [jax.ShapeDtypeStruct((b, t, d), F32)]
    if nxt is not None:
        in_specs += [_vec_spec(d, nxt[0], nxt[1], row), _vec_spec(d, nxt[0], nxt[2], row)]
        args += [mod4, mod4]
        out_specs.append(tok)
        out_shape.append(jax.ShapeDtypeStruct((b, t, d), BF16))
    out = pl.pallas_call(
        functools.partial(_ln_kernel, with_next=nxt is not None),
        grid=(b, t // tt),
        in_specs=in_specs, out_specs=out_specs, out_shape=out_shape,
        compiler_params=_params("parallel", "parallel"),
        name="deepnorm_residual",
    )(*args)
    return (out[0], out[1]) if nxt is not None else (out[0], None)


def _mm_kernel(*refs, alpha):
    if alpha is None:
        *a_refs, w_ref, o_ref = refs
    else:
        *a_refs, w_ref, h_ref, gate_ref, o_ref = refs
    acc = None
    k0 = 0
    for a_ref in a_refs:
        k = a_ref.shape[1]
        part = jnp.dot(a_ref[...], w_ref[k0:k0 + k, :], preferred_element_type=F32)
        acc = part if acc is None else acc + part
        k0 += k
    if alpha is not None:
        acc = alpha * h_ref[...] + gate_ref[...] * acc
    o_ref[...] = acc.astype(o_ref.dtype)


def _matmul_tiles(m, k, n, out_bytes, max_rows, resid):
    budget = VMEM_LIMIT_BYTES - 4 * 1024 * 1024
    best = None
    for tm in (1024, 512, 256, 128, 64, 32, 16):
        if m % tm or tm > max_rows:
            continue
        for tn in (2048, 1792, 1536, 1280, 1024, 768, 512, 256, 128):
            if n % tn:
                continue
            need = 2 * tm * k * 2 + 2 * k * tn * 2 + 2 * tm * tn * out_bytes + tm * tn * 4
            if resid:
                need += 3 * tm * tn * 4
            if need <= budget and (best is None or tm * tn > best[0] * best[1]):
                best = (tm, tn)
    assert best is not None, (m, k, n)
    return best


def _matmul(a_parts, w, layer, out_dtype=F32, resid=None):
    lead = a_parts[0].shape[:-1]
    a2 = [a.reshape(-1, a.shape[-1]) for a in a_parts]
    m = a2[0].shape[0]
    _, k, n = w.shape
    assert sum(a.shape[1] for a in a2) == k
    seq = lead[-1]
    tm, tn = _matmul_tiles(m, k, n, jnp.dtype(out_dtype).itemsize,
                           seq if resid is not None else m, resid is not None)
    in_specs = [pl.BlockSpec((tm, a.shape[1]), lambda j, i: (i, 0)) for a in a2]
    in_specs.append(pl.BlockSpec((None, k, tn), lambda j, i: (layer, 0, j)))
    args = a2 + [w]
    alpha = None
    if resid is not None:
        h, mod4, mod_layer, gate_part, row, alpha = resid
        per_seq = seq // tm
        in_specs.append(pl.BlockSpec((tm, tn), lambda j, i: (i, j)))
        in_specs.append(pl.BlockSpec((None, None, 1, tn),
                                     lambda j, i: (mod_layer, row(i // per_seq), 0, gate_part * (n // tn) + j)))
        args += [h.reshape(m, n), mod4]
    out = pl.pallas_call(
        functools.partial(_mm_kernel, alpha=alpha),
        grid=(n // tn, m // tm),
        in_specs=in_specs,
        out_specs=pl.BlockSpec((tm, tn), lambda j, i: (i, j)),
        out_shape=jax.ShapeDtypeStruct((m, n), out_dtype),
        compiler_params=_params("parallel", "parallel"),
        name="projection",
    )(*args)
    return out.reshape(lead + (n,))


def _head_sum(x):
    li = (lax.broadcasted_iota(jnp.int32, (2 * LANES, LANES), 0) % LANES) // HEAD
    lj = lax.broadcasted_iota(jnp.int32, (2 * LANES, LANES), 1) // HEAD
    ones = (li == lj).astype(BF16)
    hi = x.astype(BF16)
    lo = (x - hi.astype(F32)).astype(BF16)
    return jnp.dot(jnp.concatenate([hi, lo], axis=1), ones, preferred_element_type=F32)


def _prep_kernel(p_ref, pp_ref, pn_ref, shw_ref, w0_ref, w2_ref, a0_ref, a2_ref, g2_ref,
                 kkw_ref, ka_ref, rk_ref,
                 r_o, v_o, kk_o, lwf_o, lwb_o, kf_o, kb_o, bf_o, bb_o, g_o, bv_o,
                 *, width, rp, rgp):
    i = pl.program_id(1)
    last = pl.num_programs(1) - 1
    tt = p_ref.shape[0]
    row = lax.broadcasted_iota(jnp.int32, (tt, 1), 0)

    def shifted(c0, c1):
        x = p_ref[:, c0:c1].astype(F32)
        hs = pp_ref.shape[0]
        prev_row = jnp.where(i > 0, pp_ref[:, c0:c1].astype(F32)[hs - 1:hs], 0.0)
        next_row = jnp.where(i < last, pn_ref[:, c0:c1].astype(F32)[0:1], 0.0)
        xp = jnp.where(row == 0, prev_row, pltpu.roll(x, 1, 0))
        xn = jnp.where(row == tt - 1, next_row, pltpu.roll(x, tt - 1, 0))
        return shw_ref[0:1, c0:c1] * xp + shw_ref[1:2, c0:c1] * x + shw_ref[2:3, c0:c1] * xn

    hp_count = width // LANES
    o = 3 * width
    kkw = kkw_ref[...]
    ka = ka_ref[...]
    rk = rk_ref[...]
    g = jnp.dot(_sigmoid(shifted(o + 4 * rp, o + 4 * rp + rgp)).astype(BF16), g2_ref[...],
                preferred_element_type=F32)
    g_o[...] = g.astype(g_o.dtype)
    rates = []
    for d in range(2):
        wd = shifted(o + d * rp, o + (d + 1) * rp)
        ad = shifted(o + (2 + d) * rp, o + (3 + d) * rp)
        wl = w0_ref[d:d + 1, :] + jnp.dot(jnp.tanh(wd).astype(BF16), w2_ref[d], preferred_element_type=F32)
        lw = -math.exp(-0.5) * _sigmoid(wl)
        ar = _sigmoid(a0_ref[d:d + 1, :] + jnp.dot(ad.astype(BF16), a2_ref[d], preferred_element_type=F32))
        rates.append(ar)
        lw_o = lwf_o if d == 0 else lwb_o
        for hp in range(hp_count):
            lw_o[hp] = lw[:, hp * LANES:(hp + 1) * LANES]
    for hp in range(hp_count):
        c0, c1 = hp * LANES, (hp + 1) * LANES
        r = shifted(c0, c1)
        k = shifted(width + c0, width + c1)
        v = shifted(2 * width + c0, 2 * width + c1)
        kkr = k * kkw[:, c0:c1]
        kk = kkr * lax.rsqrt(jnp.maximum(_head_sum(kkr * kkr), 1e-24))
        kd0 = k * (1.0 + (rates[0][:, c0:c1] - 1.0) * ka[:, c0:c1])
        kd1 = k * (1.0 + (rates[1][:, c0:c1] - 1.0) * ka[:, c0:c1])
        r_o[hp] = r.astype(r_o.dtype)
        v_o[hp] = v.astype(v_o.dtype)
        kk_o[hp] = kk.astype(kk_o.dtype)
        kf_o[hp] = kd0.astype(kf_o.dtype)
        kb_o[hp] = kd1.astype(kb_o.dtype)
        bf_o[hp] = (kk * rates[0][:, c0:c1]).astype(bf_o.dtype)
        bb_o[hp] = (kk * rates[1][:, c0:c1]).astype(bb_o.dtype)
        bonus = _head_sum(r * (0.5 * (kd0 + kd1)) * rk[:, c0:c1])
        bv_o[:, c0:c1] = (bonus * v).astype(bv_o.dtype)


def _rwkv_prep(pa, lay, rw):
    b, t, na = pa.shape
    width, rp, rgp = lay["width"], lay["rp"], lay["rgp"]
    hp = width // LANES
    tt = min(128, t)
    hs = SUBLANES * (4 // pa.dtype.itemsize)
    nb = t // hs
    tok = lambda w: pl.BlockSpec((None, tt, w), lambda bi, ti: (bi, ti, 0))
    halo_prev = pl.BlockSpec((None, hs, na), lambda bi, ti: (bi, jnp.maximum(ti * (tt // hs) - 1, 0), 0))
    halo_next = pl.BlockSpec((None, hs, na), lambda bi, ti: (bi, jnp.minimum((ti + 1) * (tt // hs), nb - 1), 0))
    full = lambda a: pl.BlockSpec(a.shape, lambda bi, ti: (0,) * a.ndim)
    pair = pl.BlockSpec((None, hp, tt, LANES), lambda bi, ti: (bi, 0, ti, 0))
    pair_shapes = [jax.ShapeDtypeStruct((b, hp, t, LANES), dt) for dt in (BF16, BF16, BF16, F32, F32, BF16, BF16, BF16, BF16)]
    consts = [rw["shift"], rw["w0"], rw["w2"], rw["a0"], rw["a2"], rw["g2"], rw["kk"], rw["ka"], rw["rk"]]
    outs = pl.pallas_call(
        functools.partial(_prep_kernel, width=width, rp=rp, rgp=rgp),
        grid=(b, t // tt),
        in_specs=[tok(na), halo_prev, halo_next] + [full(a) for a in consts],
        out_specs=[pair] * 9 + [tok(width), tok(width)],
        out_shape=pair_shapes + [jax.ShapeDtypeStruct((b, t, width), BF16)] * 2,
        compiler_params=_params("parallel", "parallel"),
        name="rwkv_streams",
    )(pa, pa, pa, *consts)
    names = ("r", "v", "kk", "lwf", "lwb", "kf", "kb", "bf", "bb", "g", "bv")
    return dict(zip(names, outs))


def _bdot(a, b):
    return jnp.dot(a.astype(BF16), b.astype(BF16), preferred_element_type=F32)


def _bdot_nt(a, b):
    return lax.dot_general(a.astype(BF16), b.astype(BF16), (((1,), (1,)), ((), ())), preferred_element_type=F32)


def _bdot_tn(a, b):
    return lax.dot_general(a.astype(BF16), b.astype(BF16), (((0,), (0,)), ((), ())), preferred_element_type=F32)


def _running_sum(upto, x):
    tri = upto.astype(BF16)
    hi = x.astype(BF16)
    rest = x - hi.astype(F32)
    mid = rest.astype(BF16)
    lo = (rest - mid.astype(F32)).astype(BF16)
    dot = lambda p: jnp.dot(tri, p, preferred_element_type=F32)
    return dot(hi) + dot(mid) + dot(lo)


def _wkv_kernel(r_ref, v_ref, kk_ref, lw_ref, k_ref, b_ref, s0_ref, y_ref, s_ref, *, reverse):
    c = pl.program_id(1)

    @pl.when(c == 0)
    def _():
        s_ref[...] = s0_ref[...]

    hp_count, ch, _ = r_ref.shape
    pairs = range(hp_count)
    ti = lax.broadcasted_iota(jnp.int32, (ch, ch), 0)
    tj = lax.broadcasted_iota(jnp.int32, (ch, ch), 1)
    upto = ((tj >= ti) if reverse else (tj <= ti)).astype(F32)
    ei = lax.broadcasted_iota(jnp.int32, (ch, LANES), 0)
    ej = lax.broadcasted_iota(jnp.int32, (ch, LANES), 1)
    eye2 = (ei == ej % HEAD).astype(F32)
    gi = lax.broadcasted_iota(jnp.int32, (2 * ch, 2 * LANES), 0)
    gj = lax.broadcasted_iota(jnp.int32, (2 * ch, 2 * LANES), 1) % ch
    it = gi % ch
    earlier = (gj > it) if reverse else (gj < it)
    gmask = earlier | ((gj == it) & (gi >= ch))
    first = lax.broadcasted_iota(jnp.int32, (1, LANES), 1) < HEAD
    si = lax.broadcasted_iota(jnp.int32, (LANES, LANES), 0) // HEAD
    sj = lax.broadcasted_iota(jnp.int32, (LANES, LANES), 1) // HEAD
    same_head = si == sj
    doublings = int(math.log2(ch)) - 1
    zero = jnp.zeros((), BF16)

    def split(x):
        x = x.astype(BF16)
        return jnp.concatenate([jnp.where(first, x, zero), jnp.where(first, zero, x)], axis=0)

    ar, brows, bke, vs, wcs = [], [], [], [], []
    for hp in pairs:
        lw = lw_ref[hp]
        cl = _running_sum(upto, lw)
        tot = jnp.sum(lw, axis=0, keepdims=True)
        e_in = jnp.exp(cl)
        e_out = jnp.exp(-cl)
        e_end = jnp.exp(tot - cl)
        a_t = -kk_ref[hp].astype(F32) * jnp.exp(cl - lw)
        b = b_ref[hp].astype(F32)
        k = k_ref[hp].astype(F32)
        ar.append(jnp.concatenate([a_t, r_ref[hp].astype(F32) * e_in], axis=0).astype(BF16))
        brows.append(jnp.concatenate([split(b * e_out), split(k * e_out)], axis=0))
        bke.append(jnp.concatenate([b * e_end, k * e_end], axis=0).astype(BF16))
        vs.append(v_ref[hp])
        wcs.append(jnp.exp(tot))
    state = [s_ref[hp] for hp in pairs]
    g = [jnp.where(gmask, _bdot_nt(ar[p], brows[p]), 0.0) for p in pairs]
    ars = [_bdot_nt(ar[p], state[p]) for p in pairs]
    vsplit = [split(vs[p]) for p in pairs]
    rhs = [ars[p][:ch] + _bdot(g[p][:ch, LANES:], vsplit[p]) for p in pairs]
    inv = [eye2 + g[p][:ch, :LANES] for p in pairs]
    pw = [_bdot(g[p][:ch, :LANES], split(g[p][:ch, :LANES])) for p in pairs]
    for _ in range(doublings - 1):
        prod = [_bdot(pw[p], jnp.concatenate([split(inv[p]), split(pw[p])], axis=1)) for p in pairs]
        inv = [inv[p] + prod[p][:, :LANES] for p in pairs]
        pw = [prod[p][:, LANES:] for p in pairs]
    inv = [inv[p] + _bdot(pw[p], split(inv[p])) for p in pairs]
    u = [_bdot(inv[p], split(rhs[p])) for p in pairs]
    ys = [ars[p][ch:] + _bdot(g[p][ch:, :], jnp.concatenate([split(u[p]), vsplit[p]], axis=0)) for p in pairs]
    upd = [_bdot_tn(jnp.concatenate([u[p].astype(BF16), vs[p].astype(BF16)], axis=0), bke[p]) for p in pairs]
    for hp in pairs:
        y_ref[hp] = ys[hp]
        s_ref[hp] = state[hp] * wcs[hp] + jnp.where(same_head, upd[hp], 0.0)


def _wkv_scan(st, d, s0, reverse):
    r = st["r"]
    b, hp, t, _ = r.shape
    ch = min(SCAN_CHUNK, t)
    nc = t // ch
    cidx = (lambda c: nc - 1 - c) if reverse else (lambda c: c)
    blk = pl.BlockSpec((None, hp, ch, LANES), lambda bi, c: (bi, 0, cidx(c), 0))
    st_spec = pl.BlockSpec((None, hp, LANES, LANES), lambda bi, c: (bi, 0, 0, 0))
    lw, k, bb = (st["lwf"], st["kf"], st["bf"]) if d == 0 else (st["lwb"], st["kb"], st["bb"])
    y, s_last = pl.pallas_call(
        functools.partial(_wkv_kernel, reverse=reverse),
        grid=(b, nc),
        in_specs=[blk] * 6 + [st_spec],
        out_specs=[blk, st_spec],
        out_shape=[jax.ShapeDtypeStruct((b, hp, t, LANES), F32),
                   jax.ShapeDtypeStruct((b, hp, LANES, LANES), F32)],
        compiler_params=_params("parallel", "arbitrary"),
        name="wkv_scan",
    )(r, st["v"], st["kk"], lw, k, bb, s0)
    return y, s_last


def _rwkv_out_kernel(yf_ref, yb_ref, bv_ref, g_ref, lng_ref, lnb_ref, o_ref):
    hp_count = yf_ref.shape[0]
    for hp in range(hp_count):
        c0, c1 = hp * LANES, (hp + 1) * LANES
        y = yf_ref[hp] + yb_ref[hp]
        mu = _head_sum(y) * (1.0 / HEAD)
        yc = y - mu
        var = _head_sum(yc * yc) * (1.0 / HEAD)
        yn = yc * lax.rsqrt(var + GN_EPS) * lng_ref[:, c0:c1] + lnb_ref[:, c0:c1]
        o_ref[:, c0:c1] = ((yn + bv_ref[:, c0:c1].astype(F32)) * g_ref[:, c0:c1].astype(F32)).astype(o_ref.dtype)


def _rwkv_out(yf, yb, st, lnx_g, lnx_b):
    b, hp, t, _ = yf.shape
    width = hp * LANES
    tt = min(256, t)
    pair = pl.BlockSpec((None, hp, tt, LANES), lambda bi, ti: (bi, 0, ti, 0))
    tok = pl.BlockSpec((None, tt, width), lambda bi, ti: (bi, ti, 0))
    vec = pl.BlockSpec((1, width), lambda bi, ti: (0, 0))
    return pl.pallas_call(
        _rwkv_out_kernel,
        grid=(b, t // tt),
        in_specs=[pair, pair, tok, tok, vec, vec],
        out_specs=tok,
        out_shape=jax.ShapeDtypeStruct((b, t, width), BF16),
        compiler_params=_params("parallel", "parallel"),
        name="rwkv_out",
    )(yf, yb, st["bv"], st["g"], lnx_g, lnx_b)


def _gelu(z):
    return 0.5 * z * (1.0 + lax.erf(z * (2.0 ** -0.5)))


def _sgu_kernel(u_ref, v_ref, lng_ref, lnb_ref, ws_ref, bs_ref, o_ref):
    heads = ws_ref.shape[0]
    width = u_ref.shape[1]
    hw = width // heads
    v = _gelu(v_ref[...].astype(F32))
    mu = jnp.mean(v, axis=-1, keepdims=True)
    vc = v - mu
    var = jnp.mean(vc * vc, axis=-1, keepdims=True)
    vn = (vc * lax.rsqrt(var + LN_EPS) * lng_ref[...] + lnb_ref[...]).astype(BF16)
    for h in range(heads):
        c0, c1 = h * hw, (h + 1) * hw
        s = jnp.dot(ws_ref[h], vn[:, c0:c1], preferred_element_type=F32) + bs_ref[:, h:h + 1]
        o_ref[:, c0:c1] = (_gelu(u_ref[:, c0:c1].astype(F32)) * s).astype(o_ref.dtype)


def _sgu(pb, ln_g, ln_b, ws, bs_t):
    b, t, w2 = pb.shape
    width = w2 // 2
    heads, chunk, _ = ws.shape
    tok_u = pl.BlockSpec((None, chunk, width), lambda bi, ti: (bi, ti, 0))
    tok_v = pl.BlockSpec((None, chunk, width), lambda bi, ti: (bi, ti, 1))
    full = lambda a: pl.BlockSpec(a.shape, lambda bi, ti: (0,) * a.ndim)
    return pl.pallas_call(
        _sgu_kernel,
        grid=(b, t // chunk),
        in_specs=[tok_u, tok_v, full(ln_g), full(ln_b), full(ws), full(bs_t)],
        out_specs=tok_u,
        out_shape=jax.ShapeDtypeStruct((b, t, width), BF16),
        compiler_params=_params("parallel", "parallel"),
        name="spatial_gating",
    )(pb, pb, ln_g, ln_b, ws, bs_t)


def _odd_kernel(xc_ref, bg_ref, cg_ref, xd_ref, pw_ref, ps_ref, cw_ref, o_ref, *, seg):
    tt, width = xc_ref.shape
    groups = pw_ref.shape[0]
    gc = width // groups
    ti = lax.broadcasted_iota(jnp.int32, (tt, tt), 0)
    tj = lax.broadcasted_iota(jnp.int32, (tt, tt), 1)
    same_seg = (ti // seg) == (tj // seg)
    pos = lax.broadcasted_iota(jnp.int32, (tt, 1), 0) % seg
    for gi in range(groups):
        win = POOL_WINDOWS[gi]
        lo = win // 2
        hi = win - 1 - lo
        band = (same_seg & (tj >= ti - lo) & (tj <= ti + hi)).astype(BF16)
        count = (jnp.minimum(pos + hi, seg - 1) - jnp.maximum(pos - lo, 0) + 1).astype(F32)
        c0, c1 = gi * gc, (gi + 1) * gc
        x = xc_ref[:, c0:c1]
        mean = jnp.dot(band, x, preferred_element_type=F32) / count
        p = (mean - x.astype(F32)).astype(BF16)
        y = jnp.dot(p, pw_ref[gi], preferred_element_type=F32) * ps_ref[:, c0:c1]
        o_ref[:, c0:c1] = y.astype(o_ref.dtype)
    z = cg_ref[...].astype(F32) * xd_ref[...].astype(F32)
    zp = jnp.where(pos == 0, 0.0, pltpu.roll(z, 1, 0))
    zn = jnp.where(pos == seg - 1, 0.0, pltpu.roll(z, tt - 1, 0))
    conv = cw_ref[0:1, :] * zp + cw_ref[1:2, :] * z + cw_ref[2:3, :] * zn
    o_ref[:, width:] = (bg_ref[...].astype(F32) * conv).astype(o_ref.dtype)


def _odd_mix(p, pool_w, pool_scale, sconv_w, seg):
    b, t, w4 = p.shape
    width = w4 // 4
    tt = min(max(256, seg), t)
    assert tt % seg == 0 and t % tt == 0
    col = lambda ci: pl.BlockSpec((None, tt, width), lambda bi, ti: (bi, ti, ci))
    full = lambda a: pl.BlockSpec(a.shape, lambda bi, ti: (0,) * a.ndim)
    return pl.pallas_call(
        functools.partial(_odd_kernel, seg=seg),
        grid=(b, t // tt),
        in_specs=[col(0), col(1), col(2), col(3), full(pool_w), full(pool_scale), full(sconv_w)],
        out_specs=pl.BlockSpec((None, tt, 2 * width), lambda bi, ti: (bi, ti, 0)),
        out_shape=jax.ShapeDtypeStruct((b, t, 2 * width), BF16),
        compiler_params=_params("parallel", "parallel"),
        name="pool_shortconv",
    )(p, p, p, p, pool_w, pool_scale, sconv_w)


def _ffn_mid_kernel(g_ref, gp_ref, gn_ref, val_ref, cw_ref, o_ref, *, shift):
    i = pl.program_id(1)
    last = pl.num_programs(1) - 1
    tt = g_ref.shape[0]
    hs = gp_ref.shape[0]
    g = g_ref[...].astype(F32)
    if hs == shift:
        prev_blk = jnp.where(i > 0, gp_ref[...].astype(F32), 0.0)
        next_blk = jnp.where(i < last, gn_ref[...].astype(F32), 0.0)
        if tt > shift:
            gp = jnp.concatenate([prev_blk, g[:tt - shift]], axis=0)
            gn = jnp.concatenate([g[shift:], next_blk], axis=0)
        else:
            gp, gn = prev_blk, next_blk
    else:
        row = lax.broadcasted_iota(jnp.int32, (tt, 1), 0)
        prev_row = jnp.where(i > 0, gp_ref[...].astype(F32)[hs - 1:hs, :], 0.0)
        next_row = jnp.where(i < last, gn_ref[...].astype(F32)[0:1, :], 0.0)
        gp = jnp.where(row == 0, prev_row, pltpu.roll(g, 1, 0))
        gn = jnp.where(row == tt - 1, next_row, pltpu.roll(g, tt - 1, 0))
    conv = cw_ref[0:1, :] * gp + cw_ref[1:2, :] * g + cw_ref[2:3, :] * gn
    o_ref[...] = (conv * _sigmoid(conv) * val_ref[...].astype(F32)).astype(o_ref.dtype)


def _ffn_mid(up, conv_w, shift):
    b, t, f2 = up.shape
    f = f2 // 2
    tt = min(512, t)
    tc = f
    for cand in (1280, 1024, 512, 256, 128):
        if f % cand == 0:
            tc = cand
            break
    ncol = f // tc
    row_tile = SUBLANES * (4 // up.dtype.itemsize)
    hs = shift if shift % row_tile == 0 else row_tile
    assert shift == 1 or (hs == shift and tt % shift == 0)
    nh = t // hs
    main = pl.BlockSpec((None, tt, tc), lambda bi, ti, ci: (bi, ti, ci))
    prev = pl.BlockSpec((None, hs, tc), lambda bi, ti, ci: (bi, jnp.maximum(ti * (tt // hs) - 1, 0), ci))
    nxt = pl.BlockSpec((None, hs, tc), lambda bi, ti, ci: (bi, jnp.minimum((ti + 1) * (tt // hs), nh - 1), ci))
    val = pl.BlockSpec((None, tt, tc), lambda bi, ti, ci: (bi, ti, ci + ncol))
    cw = pl.BlockSpec((3, tc), lambda bi, ti, ci: (0, ci))
    return pl.pallas_call(
        functools.partial(_ffn_mid_kernel, shift=shift),
        grid=(b, t // tt, ncol),
        in_specs=[main, prev, nxt, val, cw],
        out_specs=main,
        out_shape=jax.ShapeDtypeStruct((b, t, f), BF16),
        compiler_params=_params("parallel", "parallel", "parallel"),
        name="convffn_gate",
    )(up, up, up, up, conv_w)


def _pad_cols(w, n):
    return jnp.pad(w, ((0, 0),) * (w.ndim - 1) + ((0, n - w.shape[-1]),))


def _even_layout(width, r_decay, r_aaa, r_gate):
    rp = _round_up(max(r_decay, r_aaa), LANES)
    rgp = _round_up(r_gate, LANES)
    na = _round_up(3 * width + 4 * rp + rgp, 1024)
    return dict(width=width, rp=rp, rgp=rgp, na=na)


def _permute_rwkv_cols(w, lay, r_decay, r_aaa, r_gate):
    width, rp, rgp, na = lay["width"], lay["rp"], lay["rgp"], lay["na"]
    o = 3 * width
    parts = [w[..., :o]]
    for seg in (r_decay, r_decay, r_aaa, r_aaa):
        parts.append(_pad_cols(w[..., o:o + seg], rp))
        o += seg
    parts.append(_pad_cols(w[..., o:o + r_gate], rgp))
    return _pad_cols(jnp.concatenate(parts, axis=-1), na)


def _pad_rows(w, n):
    pad = [(0, 0)] * w.ndim
    pad[-2] = (0, n - w.shape[-2])
    return jnp.pad(w, pad)


def kernel(x, c, ctx, c_ctx, ada_w, ada_b, ln_g, ln_b, ffn_w_up, ffn_conv, ffn_w_down, ev_w_in, ev_w_out, ev_shift, rwkv_w0, rwkv_w2, rwkv_a0, rwkv_a2, rwkv_g2, rwkv_kk, rwkv_ka, rwkv_rk, rwkv_lnx_g, rwkv_lnx_b, sgu_ln_g, sgu_ln_b, sgu_w, sgu_b, od_w_in, od_w_out, pool_w, pool_scale, sconv_w):
    batch, seq, d = x.shape
    depth = ada_w.shape[0]
    alpha = (2 * depth) ** 0.25
    last_cross = 2 * ((depth - 1) // 2)
    heads = rwkv_rk.shape[1]
    width_a = heads * HEAD
    r_decay, r_aaa, r_gate = rwkv_w2.shape[2], rwkv_a2.shape[2], rwkv_g2.shape[1]
    in_a = 3 * width_a + 2 * r_decay + 2 * r_aaa + r_gate
    lay = _even_layout(width_a, r_decay, r_aaa, r_gate)

    rows = _round_up(batch + 1, SUBLANES)
    cc = jnp.zeros((rows, d), F32).at[:batch].set(c).at[batch].set(c_ctx)
    mod4 = _modulation_all(cc, ada_w, ada_b).reshape(depth, rows, 1, 6 * d)
    lng4 = ln_g.reshape(depth, 2, 1, d)
    lnb4 = ln_b.reshape(depth, 2, 1, d)
    lat_row = lambda b: b
    ctx_row = lambda b: batch

    ev_in = ev_w_in.astype(BF16)
    w_a = _permute_rwkv_cols(ev_in[..., :in_a], lay, r_decay, r_aaa, r_gate)
    w_b = ev_in[..., in_a:]
    ev_out = ev_w_out.astype(BF16)
    od_in = od_w_in.astype(BF16)
    od_out = od_w_out.astype(BF16)
    w_up = ffn_w_up.astype(BF16)
    w_down = ffn_w_down.astype(BF16)
    shift_all = _permute_rwkv_cols(ev_shift, lay, r_decay, r_aaa, r_gate)

    h_lat, h_ctx = x, ctx
    a_lat = _modulate(x, mod4, 0, lat_row, 1, 0)
    a_ctx = _modulate(ctx, mod4, 0, ctx_row, 1, 0)
    for i in range(depth):
        ctx_in = i <= last_cross
        ctx_out = i < last_cross
        j = i // 2
        mix_ctx = None
        if i % 2 == 0:
            rw = dict(
                shift=shift_all[j],
                w0=rwkv_w0[j], a0=rwkv_a0[j],
                w2=_pad_rows(rwkv_w2[j], lay["rp"]).astype(BF16),
                a2=_pad_rows(rwkv_a2[j], lay["rp"]).astype(BF16),
                g2=_pad_rows(rwkv_g2[j], lay["rgp"]).astype(BF16),
                kk=rwkv_kk[j].reshape(1, width_a), ka=rwkv_ka[j].reshape(1, width_a),
                rk=rwkv_rk[j].reshape(1, width_a))
            lnx_g = rwkv_lnx_g[j].reshape(1, width_a)
            lnx_b = rwkv_lnx_b[j].reshape(1, width_a)
            sgu_g = sgu_ln_g[j].reshape(1, -1)
            sgu_bb = sgu_ln_b[j].reshape(1, -1)
            sgu_ws = sgu_w[j].astype(BF16)
            sgu_bt = sgu_b[j].T

            st_lat = _rwkv_prep(_matmul([a_lat], w_a, j, BF16), lay, rw)
            st_ctx = _rwkv_prep(_matmul([a_ctx], w_a, j, BF16), lay, rw)
            s0 = jnp.zeros((batch, width_a // LANES, LANES, LANES), F32)
            yc_f, sc_f = _wkv_scan(st_ctx, 0, s0, False)
            yl_f, _ = _wkv_scan(st_lat, 0, sc_f, False)
            yc_b, sc_b = _wkv_scan(st_ctx, 1, s0, True)
            yl_b, _ = _wkv_scan(st_lat, 1, sc_b, True)
            ya_lat = _rwkv_out(yl_f, yl_b, st_lat, lnx_g, lnx_b)
            yb_lat = _sgu(_matmul([a_lat], w_b, j, BF16), sgu_g, sgu_bb, sgu_ws, sgu_bt)
            mix_lat, w_mix = [ya_lat, yb_lat], ev_out
            if ctx_out:
                ya_ctx = _rwkv_out(yc_f, yc_b, st_ctx, lnx_g, lnx_b)
                yb_ctx = _sgu(_matmul([a_ctx], w_b, j, BF16), sgu_g, sgu_bb, sgu_ws, sgu_bt)
                mix_ctx = [ya_ctx, yb_ctx]
        else:
            pw = pool_w[j].astype(BF16)
            ps = pool_scale[j].reshape(1, -1)
            mix_lat, w_mix = [_odd_mix(_matmul([a_lat], od_in, j, BF16), pw, ps, sconv_w[j], GRID_W)], od_out
            if ctx_out:
                mix_ctx = [_odd_mix(_matmul([a_ctx], od_in, j, BF16), pw, ps, sconv_w[j], ctx.shape[1])]

        nxt = (i + 1, 1, 0) if i + 1 < depth else None
        z = _matmul(mix_lat, w_mix, j, resid=(h_lat, mod4, i, 2, lat_row, alpha))
        h_lat, a2 = _ln_residual(z, mod4, lng4, lnb4, lat_row, i, 0, (i, 4, 3))
        gated = _ffn_mid(_matmul([a2], w_up, i, BF16), ffn_conv[i], GRID_W)
        z = _matmul([gated], w_down, i, resid=(h_lat, mod4, i, 5, lat_row, alpha))
        h_lat, a_lat = _ln_residual(z, mod4, lng4, lnb4, lat_row, i, 1, nxt)
        if ctx_out:
            z = _matmul(mix_ctx, w_mix, j, resid=(h_ctx, mod4, i, 2, ctx_row, alpha))
            h_ctx, a2 = _ln_residual(z, mod4, lng4, lnb4, ctx_row, i, 0, (i, 4, 3))
            gated = _ffn_mid(_matmul([a2], w_up, i, BF16), ffn_conv[i], 1)
            z = _matmul([gated], w_down, i, resid=(h_ctx, mod4, i, 5, ctx_row, alpha))
            h_ctx, a_ctx = _ln_residual(z, mod4, lng4, lnb4, ctx_row, i, 1, nxt)
    return h_lat
```

```python
import functools
import math

import jax
import jax.numpy as jnp
from jax import lax
from jax.experimental import pallas as pl
from jax.experimental.pallas import tpu as pltpu

GRID_W = 64
POOL_WINDOWS = (2, 4, 8, 16)
LN_EPS = 1e-6
GN_EPS = 64e-5
HEAD = 64
SCAN_CHUNK = 64
SCAN_CHUNKS_PER_STEP = 4
LANES = 128
SUBLANES = 8
VMEM_LIMIT_BYTES = 60 * 1024 * 1024

F32 = jnp.float32
BF16 = jnp.bfloat16


def _round_up(n, m):
    return (n + m - 1) // m * m


def _params(*sem):
    return pltpu.CompilerParams(dimension_semantics=sem, vmem_limit_bytes=VMEM_LIMIT_BYTES)


def _sigmoid(z):
    return 0.5 + 0.5 * jnp.tanh(0.5 * z)


def _mod_kernel(c_ref, w_ref, b_ref, o_ref):
    c = c_ref[...]
    s = (c * _sigmoid(c)).astype(BF16)
    o_ref[...] = jnp.dot(s, w_ref[...].astype(BF16), preferred_element_type=F32) + b_ref[...]


def _modulation_all(cc, ada_w, ada_b):
    depth, d, n = ada_w.shape
    r = cc.shape[0]
    tn = min(512, n)
    return pl.pallas_call(
        _mod_kernel,
        grid=(depth, n // tn),
        in_specs=[pl.BlockSpec((r, d), lambda l, j: (0, 0)),
                  pl.BlockSpec((None, d, tn), lambda l, j: (l, 0, j)),
                  pl.BlockSpec((None, 1, tn), lambda l, j: (l, 0, j))],
        out_specs=pl.BlockSpec((None, r, tn), lambda l, j: (l, 0, j)),
        out_shape=jax.ShapeDtypeStruct((depth, r, n), F32),
        compiler_params=_params("parallel", "parallel"),
        name="adaln_modulation",
    )(cc, ada_w, ada_b.reshape(depth, 1, n))


def _vec_spec(d, layer, part, row):
    return pl.BlockSpec((None, None, 1, d), lambda b, t: (layer, row(b), 0, part))


def _modulate_kernel(x_ref, sc_ref, sh_ref, a_ref):
    a_ref[...] = (x_ref[...] * (1.0 + sc_ref[...]) + sh_ref[...]).astype(a_ref.dtype)


def _modulate(x, mod4, layer, row, sc_part, sh_part):
    b, t, d = x.shape
    tt = min(256, t)
    return pl.pallas_call(
        _modulate_kernel,
        grid=(b, t // tt),
        in_specs=[pl.BlockSpec((None, tt, d), lambda bi, ti: (bi, ti, 0)),
                  _vec_spec(d, layer, sc_part, row), _vec_spec(d, layer, sh_part, row)],
        out_specs=pl.BlockSpec((None, tt, d), lambda bi, ti: (bi, ti, 0)),
        out_shape=jax.ShapeDtypeStruct((b, t, d), BF16),
        compiler_params=_params("parallel", "parallel"),
        name="modulate",
    )(x, mod4, mod4)


def _ln_kernel(*refs, with_next):
    if with_next:
        z_ref, g_ref, b_ref, sc_ref, sh_ref, hn_ref, a_ref = refs
    else:
        z_ref, g_ref, b_ref, hn_ref = refs
    z = z_ref[...]
    mu = jnp.mean(z, axis=-1, keepdims=True)
    zc = z - mu
    var = jnp.mean(zc * zc, axis=-1, keepdims=True)
    y = zc * lax.rsqrt(var + LN_EPS) * g_ref[...] + b_ref[...]
    hn_ref[...] = y
    if with_next:
        a_ref[...] = (y * (1.0 + sc_ref[...]) + sh_ref[...]).astype(a_ref.dtype)


def _ln_residual(z, mod4, lng4, lnb4, row, layer, ln_idx, nxt):
    b, t, d = z.shape
    tt = min(256, t)
    tok = pl.BlockSpec((None, tt, d), lambda bi, ti: (bi, ti, 0))
    ln_spec = pl.BlockSpec((None, None, 1, d), lambda bi, ti: (layer, ln_idx, 0, 0))
    in_specs = [tok, ln_spec, ln_spec]
    args = [z, lng4, lnb4]
    out_specs = [tok]
    out_shape = [jax.ShapeDtypeStruct((b, t, d), F32)]
    if nxt is not None:
        in_specs += [_vec_spec(d, nxt[0], nxt[1], row), _vec_spec(d, nxt[0], nxt[2], row)]
        args += [mod4, mod4]
        out_specs.append(tok)
        out_shape.append(jax.ShapeDtypeStruct((b, t, d), BF16))
    out = pl.pallas_call(
        functools.partial(_ln_kernel, with_next=nxt is not None),
        grid=(b, t // tt),
        in_specs=in_specs, out_specs=out_specs, out_shape=out_shape,
        compiler_params=_params("parallel", "parallel"),
        name="deepnorm_residual",
    )(*args)
    return (out[0], out[1]) if nxt is not None else (out[0], None)


def _mm_kernel(*refs, alpha):
    if alpha is None:
        *a_refs, w_ref, o_ref = refs
    else:
        *a_refs, w_ref, h_ref, gate_ref, o_ref = refs
    acc = None
    k0 = 0
    for a_ref in a_refs:
        k = a_ref.shape[1]
        part = jnp.dot(a_ref[...], w_ref[k0:k0 + k, :], preferred_element_type=F32)
        acc = part if acc is None else acc + part
        k0 += k
    if alpha is not None:
        acc = alpha * h_ref[...] + gate_ref[...] * acc
    o_ref[...] = acc.astype(o_ref.dtype)


def _matmul_tiles(m, k, n, out_bytes, max_rows, resid):
    budget = VMEM_LIMIT_BYTES - 4 * 1024 * 1024
    best = None
    for tm in (1024, 512, 256, 128, 64, 32, 16):
        if m % tm or tm > max_rows:
            continue
        for tn in (2048, 1792, 1536, 1280, 1024, 768, 512, 256, 128):
            if n % tn:
                continue
            need = 2 * tm * k * 2 + 2 * k * tn * 2 + 2 * tm * tn * out_bytes + tm * tn * 4
            if resid:
                need += 3 * tm * tn * 4
            if need <= budget and (best is None or tm * tn > best[0] * best[1]):
                best = (tm, tn)
    assert best is not None, (m, k, n)
    return best


def _matmul(a_parts, w, layer, out_dtype=F32, resid=None):
    lead = a_parts[0].shape[:-1]
    a2 = [a.reshape(-1, a.shape[-1]) for a in a_parts]
    m = a2[0].shape[0]
    _, k, n = w.shape
    assert sum(a.shape[1] for a in a2) == k
    seq = lead[-1]
    tm, tn = _matmul_tiles(m, k, n, jnp.dtype(out_dtype).itemsize,
                           seq if resid is not None else m, resid is not None)
    in_specs = [pl.BlockSpec((tm, a.shape[1]), lambda j, i: (i, 0)) for a in a2]
    in_specs.append(pl.BlockSpec((None, k, tn), lambda j, i: (layer, 0, j)))
    args = a2 + [w]
    alpha = None
    if resid is not None:
        h, mod4, mod_layer, gate_part, row, alpha = resid
        per_seq = seq // tm
        in_specs.append(pl.BlockSpec((tm, tn), lambda j, i: (i, j)))
        in_specs.append(pl.BlockSpec((None, None, 1, tn),
                                     lambda j, i: (mod_layer, row(i // per_seq), 0, gate_part * (n // tn) + j)))
        args += [h.reshape(m, n), mod4]
    out = pl.pallas_call(
        functools.partial(_mm_kernel, alpha=alpha),
        grid=(n // tn, m // tm),
        in_specs=in_specs,
        out_specs=pl.BlockSpec((tm, tn), lambda j, i: (i, j)),
        out_shape=jax.ShapeDtypeStruct((m, n), out_dtype),
        compiler_params=_params("parallel", "parallel"),
        name="projection",
    )(*args)
    return out.reshape(lead + (n,))


def _head_sum(x):
    li = (lax.broadcasted_iota(jnp.int32, (2 * LANES, LANES), 0) % LANES) // HEAD
    lj = lax.broadcasted_iota(jnp.int32, (2 * LANES, LANES), 1) // HEAD
    ones = (li == lj).astype(BF16)
    hi = x.astype(BF16)
    lo = (x - hi.astype(F32)).astype(BF16)
    return jnp.dot(jnp.concatenate([hi, lo], axis=1), ones, preferred_element_type=F32)


def _prep_kernel(p_ref, pp_ref, pn_ref, shw_ref, w0_ref, w2_ref, a0_ref, a2_ref, g2_ref,
                 kkw_ref, ka_ref, rk_ref,
                 r_o, v_o, kk_o, lwf_o, lwb_o, kf_o, kb_o, bf_o, bb_o, g_o, bv_o,
                 *, width, rp, rgp):
    i = pl.program_id(1)
    last = pl.num_programs(1) - 1
    tt = p_ref.shape[0]
    row = lax.broadcasted_iota(jnp.int32, (tt, 1), 0)

    def shifted(c0, c1):
        x = p_ref[:, c0:c1].astype(F32)
        hs = pp_ref.shape[0]
        prev_row = jnp.where(i > 0, pp_ref[:, c0:c1].astype(F32)[hs - 1:hs], 0.0)
        next_row = jnp.where(i < last, pn_ref[:, c0:c1].astype(F32)[0:1], 0.0)
        xp = jnp.where(row == 0, prev_row, pltpu.roll(x, 1, 0))
        xn = jnp.where(row == tt - 1, next_row, pltpu.roll(x, tt - 1, 0))
        return shw_ref[0:1, c0:c1] * xp + shw_ref[1:2, c0:c1] * x + shw_ref[2:3, c0:c1] * xn

    hp_count = width // LANES
    o = 3 * width
    kkw = kkw_ref[...]
    ka = ka_ref[...]
    rk = rk_ref[...]
    g = jnp.dot(_sigmoid(shifted(o + 4 * rp, o + 4 * rp + rgp)).astype(BF16), g2_ref[...],
                preferred_element_type=F32)
    g_o[...] = g.astype(g_o.dtype)
    rates = []
    for d in range(2):
        wd = shifted(o + d * rp, o + (d + 1) * rp)
        ad = shifted(o + (2 + d) * rp, o + (3 + d) * rp)
        wl = w0_ref[d:d + 1, :] + jnp.dot(jnp.tanh(wd).astype(BF16), w2_ref[d], preferred_element_type=F32)
        lw = -math.exp(-0.5) * _sigmoid(wl)
        ar = _sigmoid(a0_ref[d:d + 1, :] + jnp.dot(ad.astype(BF16), a2_ref[d], preferred_element_type=F32))
        rates.append(ar)
        lw_o = lwf_o if d == 0 else lwb_o
        for hp in range(hp_count):
            lw_o[hp] = lw[:, hp * LANES:(hp + 1) * LANES]
    for hp in range(hp_count):
        c0, c1 = hp * LANES, (hp + 1) * LANES
        r = shifted(c0, c1)
        k = shifted(width + c0, width + c1)
        v = shifted(2 * width + c0, 2 * width + c1)
        kkr = k * kkw[:, c0:c1]
        kk = kkr * lax.rsqrt(jnp.maximum(_head_sum(kkr * kkr), 1e-24))
        kd0 = k * (1.0 + (rates[0][:, c0:c1] - 1.0) * ka[:, c0:c1])
        kd1 = k * (1.0 + (rates[1][:, c0:c1] - 1.0) * ka[:, c0:c1])
        r_o[hp] = r.astype(r_o.dtype)
        v_o[hp] = v.astype(v_o.dtype)
        kk_o[hp] = kk.astype(kk_o.dtype)
        kf_o[hp] = kd0.astype(kf_o.dtype)
        kb_o[hp] = kd1.astype(kb_o.dtype)
        bf_o[hp] = (kk * rates[0][:, c0:c1]).astype(bf_o.dtype)
        bb_o[hp] = (kk * rates[1][:, c0:c1]).astype(bb_o.dtype)
        bonus = _head_sum(r * (0.5 * (kd0 + kd1)) * rk[:, c0:c1])
        bv_o[:, c0:c1] = (bonus * v).astype(bv_o.dtype)


def _rwkv_prep(pa, lay, rw):
    b, t, na = pa.shape
    width, rp, rgp = lay["width"], lay["rp"], lay["rgp"]
    hp = width // LANES
    tt = min(128, t)
    hs = SUBLANES * (4 // pa.dtype.itemsize)
    nb = t // hs
    tok = lambda w: pl.BlockSpec((None, tt, w), lambda bi, ti: (bi, ti, 0))
    halo_prev = pl.BlockSpec((None, hs, na), lambda bi, ti: (bi, jnp.maximum(ti * (tt // hs) - 1, 0), 0))
    halo_next = pl.BlockSpec((None, hs, na), lambda bi, ti: (bi, jnp.minimum((ti + 1) * (tt // hs), nb - 1), 0))
    full = lambda a: pl.BlockSpec(a.shape, lambda bi, ti: (0,) * a.ndim)
    pair = pl.BlockSpec((None, hp, tt, LANES), lambda bi, ti: (bi, 0, ti, 0))
    pair_shapes = [jax.ShapeDtypeStruct((b, hp, t, LANES), dt) for dt in (BF16, BF16, BF16, F32, F32, BF16, BF16, BF16, BF16)]
    consts = [rw["shift"], rw["w0"], rw["w2"], rw["a0"], rw["a2"], rw["g2"], rw["kk"], rw["ka"], rw["rk"]]
    outs = pl.pallas_call(
        functools.partial(_prep_kernel, width=width, rp=rp, rgp=rgp),
        grid=(b, t // tt),
        in_specs=[tok(na), halo_prev, halo_next] + [full(a) for a in consts],
        out_specs=[pair] * 9 + [tok(width), tok(width)],
        out_shape=pair_shapes + [jax.ShapeDtypeStruct((b, t, width), BF16)] * 2,
        compiler_params=_params("parallel", "parallel"),
        name="rwkv_streams",
    )(pa, pa, pa, *consts)
    names = ("r", "v", "kk", "lwf", "lwb", "kf", "kb", "bf", "bb", "g", "bv")
    return dict(zip(names, outs))


def _bdot(a, b):
    return jnp.dot(a.astype(BF16), b.astype(BF16), preferred_element_type=F32)


def _bdot_nt(a, b):
    return lax.dot_general(a.astype(BF16), b.astype(BF16), (((1,), (1,)), ((), ())), preferred_element_type=F32)


def _bdot_tn(a, b):
    return lax.dot_general(a.astype(BF16), b.astype(BF16), (((0,), (0,)), ((), ())), preferred_element_type=F32)


def _running_sum(upto, x):
    tri = upto.astype(BF16)
    hi = x.astype(BF16)
    rest = x - hi.astype(F32)
    mid = rest.astype(BF16)
    lo = (rest - mid.astype(F32)).astype(BF16)
    dot = lambda p: jnp.dot(tri, p, preferred_element_type=F32)
    return dot(hi) + dot(mid) + dot(lo)


def _wkv_kernel(r_ref, v_ref, kk_ref, lw_ref, k_ref, b_ref, s0_ref, y_ref, s_ref, *, reverse, ch):
    c = pl.program_id(1)

    @pl.when(c == 0)
    def _():
        s_ref[...] = s0_ref[...]

    hp_count, rows, _ = r_ref.shape
    nsub = rows // ch
    pairs = range(hp_count)
    order = list(range(nsub - 1, -1, -1)) if reverse else list(range(nsub))
    chains = [(p, s) for s in order for p in pairs]
    ti = lax.broadcasted_iota(jnp.int32, (ch, ch), 0)
    tj = lax.broadcasted_iota(jnp.int32, (ch, ch), 1)
    upto = ((tj >= ti) if reverse else (tj <= ti)).astype(F32)
    ei = lax.broadcasted_iota(jnp.int32, (ch, LANES), 0)
    ej = lax.broadcasted_iota(jnp.int32, (ch, LANES), 1)
    eye2 = (ei == ej % HEAD).astype(F32)
    gi = lax.broadcasted_iota(jnp.int32, (2 * ch, 2 * LANES), 0)
    gj = lax.broadcasted_iota(jnp.int32, (2 * ch, 2 * LANES), 1) % ch
    it = gi % ch
    earlier = (gj > it) if reverse else (gj < it)
    gmask = earlier | ((gj == it) & (gi >= ch))
    first = lax.broadcasted_iota(jnp.int32, (1, LANES), 1) < HEAD
    si = lax.broadcasted_iota(jnp.int32, (LANES, LANES), 0) // HEAD
    sj = lax.broadcasted_iota(jnp.int32, (LANES, LANES), 1) // HEAD
    same_head = si == sj
    doublings = int(math.log2(ch)) - 1
    zero = jnp.zeros((), BF16)

    def split(x):
        x = x.astype(BF16)
        return jnp.concatenate([jnp.where(first, x, zero), jnp.where(first, zero, x)], axis=0)

    ar, brows, bke, vs, vsplit, wcs = {}, {}, {}, {}, {}, {}
    for p, s in chains:
        sl = slice(s * ch, (s + 1) * ch)
        lw = lw_ref[p, sl, :]
        cl = _running_sum(upto, lw)
        tot = jnp.sum(lw, axis=0, keepdims=True)
        e_in = jnp.exp(cl)
        e_out = jnp.exp(-cl)
        e_end = jnp.exp(tot - cl)
        a_t = -kk_ref[p, sl, :].astype(F32) * jnp.exp(cl - lw)
        b = b_ref[p, sl, :].astype(F32)
        k = k_ref[p, sl, :].astype(F32)
        ar[p, s] = jnp.concatenate([a_t, r_ref[p, sl, :].astype(F32) * e_in], axis=0).astype(BF16)
        brows[p, s] = jnp.concatenate([split(b * e_out), split(k * e_out)], axis=0)
        bke[p, s] = jnp.concatenate([b * e_end, k * e_end], axis=0).astype(BF16)
        vs[p, s] = v_ref[p, sl, :]
        vsplit[p, s] = split(vs[p, s])
        wcs[p, s] = jnp.exp(tot)
    g = {c: jnp.where(gmask, _bdot_nt(ar[c], brows[c]), 0.0) for c in chains}
    lakv = {c: _bdot(g[c][:ch, LANES:], vsplit[c]) for c in chains}
    inv = {c: eye2 + g[c][:ch, :LANES] for c in chains}
    pw = {c: _bdot(g[c][:ch, :LANES], split(g[c][:ch, :LANES])) for c in chains}
    for _ in range(doublings - 1):
        prod = {c: _bdot(pw[c], jnp.concatenate([split(inv[c]), split(pw[c])], axis=1)) for c in chains}
        inv = {c: inv[c] + prod[c][:, :LANES] for c in chains}
        pw = {c: prod[c][:, LANES:] for c in chains}
    inv = {c: inv[c] + _bdot(pw[c], split(inv[c])) for c in chains}
    state = [s_ref[p] for p in pairs]
    for s in order:
        ars = [_bdot_nt(ar[p, s], state[p]) for p in pairs]
        u = [_bdot(inv[p, s], split(ars[p][:ch] + lakv[p, s])) for p in pairs]
        ys = [ars[p][ch:] + _bdot(g[p, s][ch:, :], jnp.concatenate([split(u[p]), vsplit[p, s]], axis=0))
              for p in pairs]
        upd = [_bdot_tn(jnp.concatenate([u[p].astype(BF16), vs[p, s].astype(BF16)], axis=0), bke[p, s])
               for p in pairs]
        state = [state[p] * wcs[p, s] + jnp.where(same_head, upd[p], 0.0) for p in pairs]
        for p in pairs:
            y_ref[p, s * ch:(s + 1) * ch, :] = ys[p]
    for p in pairs:
        s_ref[p] = state[p]


def _wkv_scan(st, d, s0, reverse):
    r = st["r"]
    b, hp, t, _ = r.shape
    ch = min(SCAN_CHUNK, t)
    rows = min(SCAN_CHUNKS_PER_STEP * ch, t)
    nc = t // rows
    cidx = (lambda c: nc - 1 - c) if reverse else (lambda c: c)
    blk = pl.BlockSpec((None, hp, rows, LANES), lambda bi, c: (bi, 0, cidx(c), 0))
    st_spec = pl.BlockSpec((None, hp, LANES, LANES), lambda bi, c: (bi, 0, 0, 0))
    lw, k, bb = (st["lwf"], st["kf"], st["bf"]) if d == 0 else (st["lwb"], st["kb"], st["bb"])
    y, s_last = pl.pallas_call(
        functools.partial(_wkv_kernel, reverse=reverse, ch=ch),
        grid=(b, nc),
        in_specs=[blk] * 6 + [st_spec],
        out_specs=[blk, st_spec],
        out_shape=[jax.ShapeDtypeStruct((b, hp, t, LANES), F32),
                   jax.ShapeDtypeStruct((b, hp, LANES, LANES), F32)],
        compiler_params=_params("parallel", "arbitrary"),
        name="wkv_scan",
    )(r, st["v"], st["kk"], lw, k, bb, s0)
    return y, s_last


def _rwkv_out_kernel(yf_ref, yb_ref, bv_ref, g_ref, lng_ref, lnb_ref, o_ref):
    hp_count = yf_ref.shape[0]
    for hp in range(hp_count):
        c0, c1 = hp * LANES, (hp + 1) * LANES
        y = yf_ref[hp] + yb_ref[hp]
        mu = _head_sum(y) * (1.0 / HEAD)
        yc = y - mu
        var = _head_sum(yc * yc) * (1.0 / HEAD)
        yn = yc * lax.rsqrt(var + GN_EPS) * lng_ref[:, c0:c1] + lnb_ref[:, c0:c1]
        o_ref[:, c0:c1] = ((yn + bv_ref[:, c0:c1].astype(F32)) * g_ref[:, c0:c1].astype(F32)).astype(o_ref.dtype)


def _rwkv_out(yf, yb, st, lnx_g, lnx_b):
    b, hp, t, _ = yf.shape
    width = hp * LANES
    tt = min(256, t)
    pair = pl.BlockSpec((None, hp, tt, LANES), lambda bi, ti: (bi, 0, ti, 0))
    tok = pl.BlockSpec((None, tt, width), lambda bi, ti: (bi, ti, 0))
    vec = pl.BlockSpec((1, width), lambda bi, ti: (0, 0))
    return pl.pallas_call(
        _rwkv_out_kernel,
        grid=(b, t // tt),
        in_specs=[pair, pair, tok, tok, vec, vec],
        out_specs=tok,
        out_shape=jax.ShapeDtypeStruct((b, t, width), BF16),
        compiler_params=_params("parallel", "parallel"),
        name="rwkv_out",
    )(yf, yb, st["bv"], st["g"], lnx_g, lnx_b)


def _gelu(z):
    return 0.5 * z * (1.0 + lax.erf(z * (2.0 ** -0.5)))


def _sgu_kernel(u_ref, v_ref, lng_ref, lnb_ref, ws_ref, bs_ref, o_ref):
    heads = ws_ref.shape[0]
    width = u_ref.shape[1]
    hw = width // heads
    v = _gelu(v_ref[...].astype(F32))
    mu = jnp.mean(v, axis=-1, keepdims=True)
    vc = v - mu
    var = jnp.mean(vc * vc, axis=-1, keepdims=True)
    vn = (vc * lax.rsqrt(var + LN_EPS) * lng_ref[...] + lnb_ref[...]).astype(BF16)
    for h in range(heads):
        c0, c1 = h * hw, (h + 1) * hw
        s = jnp.dot(ws_ref[h], vn[:, c0:c1], preferred_element_type=F32) + bs_ref[:, h:h + 1]
        o_ref[:, c0:c1] = (_gelu(u_ref[:, c0:c1].astype(F32)) * s).astype(o_ref.dtype)


def _sgu(pb, ln_g, ln_b, ws, bs_t):
    b, t, w2 = pb.shape
    width = w2 // 2
    heads, chunk, _ = ws.shape
    tok_u = pl.BlockSpec((None, chunk, width), lambda bi, ti: (bi, ti, 0))
    tok_v = pl.BlockSpec((None, chunk, width), lambda bi, ti: (bi, ti, 1))
    full = lambda a: pl.BlockSpec(a.shape, lambda bi, ti: (0,) * a.ndim)
    return pl.pallas_call(
        _sgu_kernel,
        grid=(b, t // chunk),
        in_specs=[tok_u, tok_v, full(ln_g), full(ln_b), full(ws), full(bs_t)],
        out_specs=tok_u,
        out_shape=jax.ShapeDtypeStruct((b, t, width), BF16),
        compiler_params=_params("parallel", "parallel"),
        name="spatial_gating",
    )(pb, pb, ln_g, ln_b, ws, bs_t)


def _odd_kernel(xc_ref, bg_ref, cg_ref, xd_ref, pw_ref, ps_ref, cw_ref, o_ref, *, seg):
    tt, width = xc_ref.shape
    groups = pw_ref.shape[0]
    gc = width // groups
    ti = lax.broadcasted_iota(jnp.int32, (tt, tt), 0)
    tj = lax.broadcasted_iota(jnp.int32, (tt, tt), 1)
    same_seg = (ti // seg) == (tj // seg)
    pos = lax.broadcasted_iota(jnp.int32, (tt, 1), 0) % seg
    for gi in range(groups):
        win = POOL_WINDOWS[gi]
        lo = win // 2
        hi = win - 1 - lo
        band = (same_seg & (tj >= ti - lo) & (tj <= ti + hi)).astype(BF16)
        count = (jnp.minimum(pos + hi, seg - 1) - jnp.maximum(pos - lo, 0) + 1).astype(F32)
        c0, c1 = gi * gc, (gi + 1) * gc
        x = xc_ref[:, c0:c1]
        mean = jnp.dot(band, x, preferred_element_type=F32) / count
        p = (mean - x.astype(F32)).astype(BF16)
        y = jnp.dot(p, pw_ref[gi], preferred_element_type=F32) * ps_ref[:, c0:c1]
        o_ref[:, c0:c1] = y.astype(o_ref.dtype)
    z = cg_ref[...].astype(F32) * xd_ref[...].astype(F32)
    zp = jnp.where(pos == 0, 0.0, pltpu.roll(z, 1, 0))
    zn = jnp.where(pos == seg - 1, 0.0, pltpu.roll(z, tt - 1, 0))
    conv = cw_ref[0:1, :] * zp + cw_ref[1:2, :] * z + cw_ref[2:3, :] * zn
    o_ref[:, width:] = (bg_ref[...].astype(F32) * conv).astype(o_ref.dtype)


def _odd_mix(p, pool_w, pool_scale, sconv_w, seg):
    b, t, w4 = p.shape
    width = w4 // 4
    tt = min(max(256, seg), t)
    assert tt % seg == 0 and t % tt == 0
    col = lambda ci: pl.BlockSpec((None, tt, width), lambda bi, ti: (bi, ti, ci))
    full = lambda a: pl.BlockSpec(a.shape, lambda bi, ti: (0,) * a.ndim)
    return pl.pallas_call(
        functools.partial(_odd_kernel, seg=seg),
        grid=(b, t // tt),
        in_specs=[col(0), col(1), col(2), col(3), full(pool_w), full(pool_scale), full(sconv_w)],
        out_specs=pl.BlockSpec((None, tt, 2 * width), lambda bi, ti: (bi, ti, 0)),
        out_shape=jax.ShapeDtypeStruct((b, t, 2 * width), BF16),
        compiler_params=_params("parallel", "parallel"),
        name="pool_shortconv",
    )(p, p, p, p, pool_w, pool_scale, sconv_w)


def _ffn_mid_kernel(g_ref, gp_ref, gn_ref, val_ref, cw_ref, o_ref, *, shift):
    i = pl.program_id(1)
    last = pl.num_programs(1) - 1
    tt = g_ref.shape[0]
    hs = gp_ref.shape[0]
    g = g_ref[...].astype(F32)
    if hs == shift:
        prev_blk = jnp.where(i > 0, gp_ref[...].astype(F32), 0.0)
        next_blk = jnp.where(i < last, gn_ref[...].astype(F32), 0.0)
        if tt > shift:
            gp = jnp.concatenate([prev_blk, g[:tt - shift]], axis=0)
            gn = jnp.concatenate([g[shift:], next_blk], axis=0)
        else:
            gp, gn = prev_blk, next_blk
    else:
        row = lax.broadcasted_iota(jnp.int32, (tt, 1), 0)
        prev_row = jnp.where(i > 0, gp_ref[...].astype(F32)[hs - 1:hs, :], 0.0)
        next_row = jnp.where(i < last, gn_ref[...].astype(F32)[0:1, :], 0.0)
        gp = jnp.where(row == 0, prev_row, pltpu.roll(g, 1, 0))
        gn = jnp.where(row == tt - 1, next_row, pltpu.roll(g, tt - 1, 0))
    hw = 0.5 * cw_ref[...]
    half = hw[0:1, :] * gp + hw[1:2, :] * g + hw[2:3, :] * gn
    o_ref[...] = (half * (1.0 + jnp.tanh(half)) * val_ref[...].astype(F32)).astype(o_ref.dtype)


def _ffn_mid(up, conv_w, shift):
    b, t, f2 = up.shape
    f = f2 // 2
    tt = min(512, t)
    tc = f
    for cand in (1280, 1024, 512, 256, 128):
        if f % cand == 0:
            tc = cand
            break
    ncol = f // tc
    row_tile = SUBLANES * (4 // up.dtype.itemsize)
    hs = shift if shift % row_tile == 0 else row_tile
    assert shift == 1 or (hs == shift and tt % shift == 0)
    nh = t // hs
    main = pl.BlockSpec((None, tt, tc), lambda bi, ti, ci: (bi, ti, ci))
    prev = pl.BlockSpec((None, hs, tc), lambda bi, ti, ci: (bi, jnp.maximum(ti * (tt // hs) - 1, 0), ci))
    nxt = pl.BlockSpec((None, hs, tc), lambda bi, ti, ci: (bi, jnp.minimum((ti + 1) * (tt // hs), nh - 1), ci))
    val = pl.BlockSpec((None, tt, tc), lambda bi, ti, ci: (bi, ti, ci + ncol))
    cw = pl.BlockSpec((3, tc), lambda bi, ti, ci: (0, ci))
    return pl.pallas_call(
        functools.partial(_ffn_mid_kernel, shift=shift),
        grid=(b, t // tt, ncol),
        in_specs=[main, prev, nxt, val, cw],
        out_specs=main,
        out_shape=jax.ShapeDtypeStruct((b, t, f), BF16),
        compiler_params=_params("parallel", "parallel", "parallel"),
        name="convffn_gate",
    )(up, up, up, up, conv_w)


def _pad_cols(w, n):
    return jnp.pad(w, ((0, 0),) * (w.ndim - 1) + ((0, n - w.shape[-1]),))


def _even_layout(width, r_decay, r_aaa, r_gate):
    rp = _round_up(max(r_decay, r_aaa), LANES)
    rgp = _round_up(r_gate, LANES)
    na = _round_up(3 * width + 4 * rp + rgp, 1024)
    return dict(width=width, rp=rp, rgp=rgp, na=na)


def _permute_rwkv_cols(w, lay, r_decay, r_aaa, r_gate):
    width, rp, rgp, na = lay["width"], lay["rp"], lay["rgp"], lay["na"]
    o = 3 * width
    parts = [w[..., :o]]
    for seg in (r_decay, r_decay, r_aaa, r_aaa):
        parts.append(_pad_cols(w[..., o:o + seg], rp))
        o += seg
    parts.append(_pad_cols(w[..., o:o + r_gate], rgp))
    return _pad_cols(jnp.concatenate(parts, axis=-1), na)


def _pad_rows(w, n):
    pad = [(0, 0)] * w.ndim
    pad[-2] = (0, n - w.shape[-2])
    return jnp.pad(w, pad)


def kernel(x, c, ctx, c_ctx, ada_w, ada_b, ln_g, ln_b, ffn_w_up, ffn_conv, ffn_w_down, ev_w_in, ev_w_out, ev_shift, rwkv_w0, rwkv_w2, rwkv_a0, rwkv_a2, rwkv_g2, rwkv_kk, rwkv_ka, rwkv_rk, rwkv_lnx_g, rwkv_lnx_b, sgu_ln_g, sgu_ln_b, sgu_w, sgu_b, od_w_in, od_w_out, pool_w, pool_scale, sconv_w):
    batch, seq, d = x.shape
    depth = ada_w.shape[0]
    alpha = (2 * depth) ** 0.25
    last_cross = 2 * ((depth - 1) // 2)
    heads = rwkv_rk.shape[1]
    width_a = heads * HEAD
    r_decay, r_aaa, r_gate = rwkv_w2.shape[2], rwkv_a2.shape[2], rwkv_g2.shape[1]
    in_a = 3 * width_a + 2 * r_decay + 2 * r_aaa + r_gate
    lay = _even_layout(width_a, r_decay, r_aaa, r_gate)

    rows = _round_up(batch + 1, SUBLANES)
    cc = jnp.zeros((rows, d), F32).at[:batch].set(c).at[batch].set(c_ctx)
    mod4 = _modulation_all(cc, ada_w, ada_b).reshape(depth, rows, 1, 6 * d)
    lng4 = ln_g.reshape(depth, 2, 1, d)
    lnb4 = ln_b.reshape(depth, 2, 1, d)
    lat_row = lambda b: b
    ctx_row = lambda b: batch

    ev_in = ev_w_in.astype(BF16)
    w_a = _permute_rwkv_cols(ev_in[..., :in_a], lay, r_decay, r_aaa, r_gate)
    w_b = ev_in[..., in_a:]
    ev_out = ev_w_out.astype(BF16)
    od_in = od_w_in.astype(BF16)
    od_out = od_w_out.astype(BF16)
    w_up = ffn_w_up.astype(BF16)
    w_down = ffn_w_down.astype(BF16)
    shift_all = _permute_rwkv_cols(ev_shift, lay, r_decay, r_aaa, r_gate)

    h_lat, h_ctx = x, ctx
    a_lat = _modulate(x, mod4, 0, lat_row, 1, 0)
    a_ctx = _modulate(ctx, mod4, 0, ctx_row, 1, 0)
    for i in range(depth):
        ctx_in = i <= last_cross
        ctx_out = i < last_cross
        j = i // 2
        mix_ctx = None
        if i % 2 == 0:
            rw = dict(
                shift=shift_all[j],
                w0=rwkv_w0[j], a0=rwkv_a0[j],
                w2=_pad_rows(rwkv_w2[j], lay["rp"]).astype(BF16),
                a2=_pad_rows(rwkv_a2[j], lay["rp"]).astype(BF16),
                g2=_pad_rows(rwkv_g2[j], lay["rgp"]).astype(BF16),
                kk=rwkv_kk[j].reshape(1, width_a), ka=rwkv_ka[j].reshape(1, width_a),
                rk=rwkv_rk[j].reshape(1, width_a))
            lnx_g = rwkv_lnx_g[j].reshape(1, width_a)
            lnx_b = rwkv_lnx_b[j].reshape(1, width_a)
            sgu_g = sgu_ln_g[j].reshape(1, -1)
            sgu_bb = sgu_ln_b[j].reshape(1, -1)
            sgu_ws = sgu_w[j].astype(BF16)
            sgu_bt = sgu_b[j].T

            st_lat = _rwkv_prep(_matmul([a_lat], w_a, j, BF16), lay, rw)
            st_ctx = _rwkv_prep(_matmul([a_ctx], w_a, j, BF16), lay, rw)
            s0 = jnp.zeros((batch, width_a // LANES, LANES, LANES), F32)
            yc_f, sc_f = _wkv_scan(st_ctx, 0, s0, False)
            yl_f, _ = _wkv_scan(st_lat, 0, sc_f, False)
            yc_b, sc_b = _wkv_scan(st_ctx, 1, s0, True)
            yl_b, _ = _wkv_scan(st_lat, 1, sc_b, True)
            ya_lat = _rwkv_out(yl_f, yl_b, st_lat, lnx_g, lnx_b)
            yb_lat = _sgu(_matmul([a_lat], w_b, j, BF16), sgu_g, sgu_bb, sgu_ws, sgu_bt)
            mix_lat, w_mix = [ya_lat, yb_lat], ev_out
            if ctx_out:
                ya_ctx = _rwkv_out(yc_f, yc_b, st_ctx, lnx_g, lnx_b)
                yb_ctx = _sgu(_matmul([a_ctx], w_b, j, BF16), sgu_g, sgu_bb, sgu_ws, sgu_bt)
                mix_ctx = [ya_ctx, yb_ctx]
        else:
            pw = pool_w[j].astype(BF16)
            ps = pool_scale[j].reshape(1, -1)
            mix_lat, w_mix = [_odd_mix(_matmul([a_lat], od_in, j, BF16), pw, ps, sconv_w[j], GRID_W)], od_out
            if ctx_out:
                mix_ctx = [_odd_mix(_matmul([a_ctx], od_in, j, BF16), pw, ps, sconv_w[j], ctx.shape[1])]

        nxt = (i + 1, 1, 0) if i + 1 < depth else None
        z = _matmul(mix_lat, w_mix, j, resid=(h_lat, mod4, i, 2, lat_row, alpha))
        h_lat, a2 = _ln_residual(z, mod4, lng4, lnb4, lat_row, i, 0, (i, 4, 3))
        gated = _ffn_mid(_matmul([a2], w_up, i, BF16), ffn_conv[i], GRID_W)
        z = _matmul([gated], w_down, i, resid=(h_lat, mod4, i, 5, lat_row, alpha))
        h_lat, a_lat = _ln_residual(z, mod4, lng4, lnb4, lat_row, i, 1, nxt)
        if ctx_out:
            z = _matmul(mix_ctx, w_mix, j, resid=(h_ctx, mod4, i, 2, ctx_row, alpha))
            h_ctx, a2 = _ln_residual(z, mod4, lng4, lnb4, ctx_row, i, 0, (i, 4, 3))
            gated = _ffn_mid(_matmul([a2], w_up, i, BF16), ffn_conv[i], 1)
            z = _matmul([gated], w_down, i, resid=(h_ctx, mod4, i, 5, ctx_row, alpha))
            h_ctx, a_ctx = _ln_residual(z, mod4, lng4, lnb4, ctx_row, i, 1, nxt)
    return h_lat
```

```python
import functools
import math

import jax
import jax.numpy as jnp
from jax import lax
from jax.experimental import pallas as pl
from jax.experimental.pallas import tpu as pltpu

GRID_W = 64
POOL_WINDOWS = (2, 4, 8, 16)
LN_EPS = 1e-6
GN_EPS = 64e-5
HEAD = 64
SCAN_CHUNK = 64
SCAN_CHUNKS_PER_STEP = 4
LANES = 128
SUBLANES = 8
VMEM_LIMIT_BYTES = 60 * 1024 * 1024

F32 = jnp.float32
BF16 = jnp.bfloat16


def _round_up(n, m):
    return (n + m - 1) // m * m


def _params(*sem):
    return pltpu.CompilerParams(dimension_semantics=sem, vmem_limit_bytes=VMEM_LIMIT_BYTES)


def _sigmoid(z):
    return 0.5 + 0.5 * jnp.tanh(0.5 * z)


def _mod_kernel(c_ref, w_ref, b_ref, o_ref):
    c = c_ref[...]
    s = (c * _sigmoid(c)).astype(BF16)
    o_ref[...] = jnp.dot(s, w_ref[...].astype(BF16), preferred_element_type=F32) + b_ref[...]


def _modulation_all(cc, ada_w, ada_b):
    depth, d, n = ada_w.shape
    r = cc.shape[0]
    tn = min(512, n)
    return pl.pallas_call(
        _mod_kernel,
        grid=(depth, n // tn),
        in_specs=[pl.BlockSpec((r, d), lambda l, j: (0, 0)),
                  pl.BlockSpec((None, d, tn), lambda l, j: (l, 0, j)),
                  pl.BlockSpec((None, 1, tn), lambda l, j: (l, 0, j))],
        out_specs=pl.BlockSpec((None, r, tn), lambda l, j: (l, 0, j)),
        out_shape=jax.ShapeDtypeStruct((depth, r, n), F32),
        compiler_params=_params("parallel", "parallel"),
        name="adaln_modulation",
    )(cc, ada_w, ada_b.reshape(depth, 1, n))


def _vec_spec(d, layer, part, row):
    return pl.BlockSpec((None, None, 1, d), lambda b, t: (layer, row(b), 0, part))


def _modulate_kernel(x_ref, sc_ref, sh_ref, a_ref):
    a_ref[...] = (x_ref[...] * (1.0 + sc_ref[...]) + sh_ref[...]).astype(a_ref.dtype)


def _modulate(x, mod4, layer, row, sc_part, sh_part):
    b, t, d = x.shape
    tt = min(256, t)
    return pl.pallas_call(
        _modulate_kernel,
        grid=(b, t // tt),
        in_specs=[pl.BlockSpec((None, tt, d), lambda bi, ti: (bi, ti, 0)),
                  _vec_spec(d, layer, sc_part, row), _vec_spec(d, layer, sh_part, row)],
        out_specs=pl.BlockSpec((None, tt, d), lambda bi, ti: (bi, ti, 0)),
        out_shape=jax.ShapeDtypeStruct((b, t, d), BF16),
        compiler_params=_params("parallel", "parallel"),
        name="modulate",
    )(x, mod4, mod4)


def _ln_apply(z, mu, rstd, g, b):
    return (z - mu) * rstd * g + b


def _ln_kernel(*refs, with_next, keep_h):
    z_ref, g_ref, b_ref = refs[:3]
    rest = list(refs[3:])
    sc_ref, sh_ref = (rest.pop(0), rest.pop(0)) if with_next else (None, None)
    z = z_ref[...]
    mu = jnp.mean(z, axis=-1, keepdims=True)
    zc = z - mu
    rstd = lax.rsqrt(jnp.mean(zc * zc, axis=-1, keepdims=True) + LN_EPS)
    y = _ln_apply(z, mu, rstd, g_ref[...], b_ref[...])
    if keep_h:
        rest.pop(0)[...] = y
    if with_next:
        a_ref = rest.pop(0)
        a_ref[...] = (y * (1.0 + sc_ref[...]) + sh_ref[...]).astype(a_ref.dtype)
    if not keep_h:
        mu_ref, rstd_ref = rest
        mu_ref[...] = jnp.broadcast_to(mu, mu_ref.shape)
        rstd_ref[...] = jnp.broadcast_to(rstd, rstd_ref.shape)


def _ln_residual(z, mod4, lng4, lnb4, row, layer, ln_idx, nxt, keep_h):
    b, t, d = z.shape
    tt = min(256, t)
    tok = pl.BlockSpec((None, tt, d), lambda bi, ti: (bi, ti, 0))
    stat = pl.BlockSpec((None, tt, LANES), lambda bi, ti: (bi, ti, 0))
    ln_spec = pl.BlockSpec((None, None, 1, d), lambda bi, ti: (layer, ln_idx, 0, 0))
    in_specs = [tok, ln_spec, ln_spec]
    args = [z, lng4, lnb4]
    out_specs, out_shape = [], []
    if nxt is not None:
        in_specs += [_vec_spec(d, nxt[0], nxt[1], row), _vec_spec(d, nxt[0], nxt[2], row)]
        args += [mod4, mod4]
    if keep_h:
        out_specs.append(tok)
        out_shape.append(jax.ShapeDtypeStruct((b, t, d), F32))
    if nxt is not None:
        out_specs.append(tok)
        out_shape.append(jax.ShapeDtypeStruct((b, t, d), BF16))
    if not keep_h:
        out_specs += [stat, stat]
        out_shape += [jax.ShapeDtypeStruct((b, t, LANES), F32)] * 2
    out = list(pl.pallas_call(
        functools.partial(_ln_kernel, with_next=nxt is not None, keep_h=keep_h),
        grid=(b, t // tt),
        in_specs=in_specs, out_specs=out_specs, out_shape=out_shape,
        compiler_params=_params("parallel", "parallel"),
        name="deepnorm_residual",
    )(*args))
    h = out.pop(0) if keep_h else None
    a = out.pop(0) if nxt is not None else None
    stream = dict(h=h) if keep_h else dict(z=z, mu=out[0], rstd=out[1], ln=(layer, ln_idx))
    return stream, a


def _mm_kernel(*refs, n_a, alpha, normed):
    a_refs, w_ref, o_ref = refs[:n_a], refs[n_a], refs[-1]
    acc = None
    k0 = 0
    for a_ref in a_refs:
        k = a_ref.shape[1]
        part = jnp.dot(a_ref[...], w_ref[k0:k0 + k, :], preferred_element_type=F32)
        acc = part if acc is None else acc + part
        k0 += k
    if alpha is not None:
        if normed:
            z_ref, mu_ref, rstd_ref, g_ref, b_ref, gate_ref = refs[n_a + 1:-1]
            h = _ln_apply(z_ref[...], mu_ref[:, 0:1], rstd_ref[:, 0:1], g_ref[...], b_ref[...])
        else:
            h_ref, gate_ref = refs[n_a + 1:-1]
            h = h_ref[...]
        acc = alpha * h + gate_ref[...] * acc
    o_ref[...] = acc.astype(o_ref.dtype)


def _matmul_tiles(m, k, n, out_bytes, max_rows, resid):
    budget = VMEM_LIMIT_BYTES - 2 * 1024 * 1024
    best = None
    for tm in (1024, 512, 256, 128, 64, 32, 16):
        if m % tm or tm > max_rows:
            continue
        for tn in (2048, 1792, 1536, 1280, 1024, 768, 512, 256, 128):
            if n % tn:
                continue
            need = 2 * tm * k * 2 + 2 * k * tn * 2 + 2 * tm * tn * out_bytes + tm * tn * 4
            if resid:
                need += 3 * tm * tn * 4 + 4 * tm * LANES * 4
            if need <= budget and (best is None or tm * tn > best[0] * best[1]):
                best = (tm, tn)
    assert best is not None, (m, k, n)
    return best


def _matmul(a_parts, w, layer, out_dtype=F32, resid=None):
    lead = a_parts[0].shape[:-1]
    a2 = [a.reshape(-1, a.shape[-1]) for a in a_parts]
    m = a2[0].shape[0]
    _, k, n = w.shape
    assert sum(a.shape[1] for a in a2) == k
    seq = lead[-1]
    tm, tn = _matmul_tiles(m, k, n, jnp.dtype(out_dtype).itemsize,
                           seq if resid is not None else m, resid is not None)
    in_specs = [pl.BlockSpec((tm, a.shape[1]), lambda j, i: (i, 0)) for a in a2]
    in_specs.append(pl.BlockSpec((None, k, tn), lambda j, i: (layer, 0, j)))
    args = a2 + [w]
    alpha, normed = None, False
    if resid is not None:
        stream, lng4, lnb4, mod4, mod_layer, gate_part, row, alpha = resid
        per_seq = seq // tm
        tile = pl.BlockSpec((tm, tn), lambda j, i: (i, j))
        normed = "z" in stream
        if normed:
            ln_layer, ln_idx = stream["ln"]
            stat = pl.BlockSpec((tm, LANES), lambda j, i: (i, 0))
            ln_vec = pl.BlockSpec((None, None, 1, tn), lambda j, i: (ln_layer, ln_idx, 0, j))
            in_specs += [tile, stat, stat, ln_vec, ln_vec]
            args += [stream["z"].reshape(m, n), stream["mu"].reshape(m, LANES), stream["rstd"].reshape(m, LANES),
                     lng4, lnb4]
        else:
            in_specs.append(tile)
            args.append(stream["h"].reshape(m, n))
        in_specs.append(pl.BlockSpec((None, None, 1, tn),
                                     lambda j, i: (mod_layer, row(i // per_seq), 0, gate_part * (n // tn) + j)))
        args.append(mod4)
    out = pl.pallas_call(
        functools.partial(_mm_kernel, n_a=len(a2), alpha=alpha, normed=normed),
        grid=(n // tn, m // tm),
        in_specs=in_specs,
        out_specs=pl.BlockSpec((tm, tn), lambda j, i: (i, j)),
        out_shape=jax.ShapeDtypeStruct((m, n), out_dtype),
        compiler_params=_params("parallel", "parallel"),
        name="projection",
    )(*args)
    return out.reshape(lead + (n,))


def _head_sum(x):
    li = (lax.broadcasted_iota(jnp.int32, (2 * LANES, LANES), 0) % LANES) // HEAD
    lj = lax.broadcasted_iota(jnp.int32, (2 * LANES, LANES), 1) // HEAD
    ones = (li == lj).astype(BF16)
    hi = x.astype(BF16)
    lo = (x - hi.astype(F32)).astype(BF16)
    return jnp.dot(jnp.concatenate([hi, lo], axis=1), ones, preferred_element_type=F32)


def _prep_kernel(p_ref, pp_ref, pn_ref, shw_ref, w0_ref, w2_ref, a0_ref, a2_ref, g2_ref,
                 kkw_ref, ka_ref, rk_ref,
                 r_o, v_o, kk_o, lwf_o, lwb_o, kf_o, kb_o, bf_o, bb_o, g_o, bv_o,
                 *, width, rp, rgp):
    i = pl.program_id(1)
    last = pl.num_programs(1) - 1
    tt = p_ref.shape[0]
    row = lax.broadcasted_iota(jnp.int32, (tt, 1), 0)

    def shifted(c0, c1):
        x = p_ref[:, c0:c1].astype(F32)
        hs = pp_ref.shape[0]
        prev_row = jnp.where(i > 0, pp_ref[:, c0:c1].astype(F32)[hs - 1:hs], 0.0)
        next_row = jnp.where(i < last, pn_ref[:, c0:c1].astype(F32)[0:1], 0.0)
        xp = jnp.where(row == 0, prev_row, pltpu.roll(x, 1, 0))
        xn = jnp.where(row == tt - 1, next_row, pltpu.roll(x, tt - 1, 0))
        return shw_ref[0:1, c0:c1] * xp + shw_ref[1:2, c0:c1] * x + shw_ref[2:3, c0:c1] * xn

    hp_count = width // LANES
    o = 3 * width
    kkw = kkw_ref[...]
    ka = ka_ref[...]
    rk = rk_ref[...]
    g = jnp.dot(_sigmoid(shifted(o + 4 * rp, o + 4 * rp + rgp)).astype(BF16), g2_ref[...],
                preferred_element_type=F32)
    g_o[...] = g.astype(g_o.dtype)
    rates = []
    for d in range(2):
        wd = shifted(o + d * rp, o + (d + 1) * rp)
        ad = shifted(o + (2 + d) * rp, o + (3 + d) * rp)
        wl = w0_ref[d:d + 1, :] + jnp.dot(jnp.tanh(wd).astype(BF16), w2_ref[d], preferred_element_type=F32)
        lw = -math.exp(-0.5) * _sigmoid(wl)
        ar = _sigmoid(a0_ref[d:d + 1, :] + jnp.dot(ad.astype(BF16), a2_ref[d], preferred_element_type=F32))
        rates.append(ar)
        lw_o = lwf_o if d == 0 else lwb_o
        for hp in range(hp_count):
            lw_o[hp] = lw[:, hp * LANES:(hp + 1) * LANES]
    for hp in range(hp_count):
        c0, c1 = hp * LANES, (hp + 1) * LANES
        r = shifted(c0, c1)
        k = shifted(width + c0, width + c1)
        v = shifted(2 * width + c0, 2 * width + c1)
        kkr = k * kkw[:, c0:c1]
        kk = kkr * lax.rsqrt(jnp.maximum(_head_sum(kkr * kkr), 1e-24))
        kd0 = k * (1.0 + (rates[0][:, c0:c1] - 1.0) * ka[:, c0:c1])
        kd1 = k * (1.0 + (rates[1][:, c0:c1] - 1.0) * ka[:, c0:c1])
        r_o[hp] = r.astype(r_o.dtype)
        v_o[hp] = v.astype(v_o.dtype)
        kk_o[hp] = kk.astype(kk_o.dtype)
        kf_o[hp] = kd0.astype(kf_o.dtype)
        kb_o[hp] = kd1.astype(kb_o.dtype)
        bf_o[hp] = (kk * rates[0][:, c0:c1]).astype(bf_o.dtype)
        bb_o[hp] = (kk * rates[1][:, c0:c1]).astype(bb_o.dtype)
        bonus = _head_sum(r * (0.5 * (kd0 + kd1)) * rk[:, c0:c1])
        bv_o[:, c0:c1] = (bonus * v).astype(bv_o.dtype)


def _rwkv_prep(pa, lay, rw):
    b, t, na = pa.shape
    width, rp, rgp = lay["width"], lay["rp"], lay["rgp"]
    hp = width // LANES
    tt = min(128, t)
    hs = SUBLANES * (4 // pa.dtype.itemsize)
    nb = t // hs
    tok = lambda w: pl.BlockSpec((None, tt, w), lambda bi, ti: (bi, ti, 0))
    halo_prev = pl.BlockSpec((None, hs, na), lambda bi, ti: (bi, jnp.maximum(ti * (tt // hs) - 1, 0), 0))
    halo_next = pl.BlockSpec((None, hs, na), lambda bi, ti: (bi, jnp.minimum((ti + 1) * (tt // hs), nb - 1), 0))
    full = lambda a: pl.BlockSpec(a.shape, lambda bi, ti: (0,) * a.ndim)
    pair = pl.BlockSpec((None, hp, tt, LANES), lambda bi, ti: (bi, 0, ti, 0))
    pair_shapes = [jax.ShapeDtypeStruct((b, hp, t, LANES), dt) for dt in (BF16, BF16, BF16, F32, F32, BF16, BF16, BF16, BF16)]
    consts = [rw["shift"], rw["w0"], rw["w2"], rw["a0"], rw["a2"], rw["g2"], rw["kk"], rw["ka"], rw["rk"]]
    outs = pl.pallas_call(
        functools.partial(_prep_kernel, width=width, rp=rp, rgp=rgp),
        grid=(b, t // tt),
        in_specs=[tok(na), halo_prev, halo_next] + [full(a) for a in consts],
        out_specs=[pair] * 9 + [tok(width), tok(width)],
        out_shape=pair_shapes + [jax.ShapeDtypeStruct((b, t, width), BF16)] * 2,
        compiler_params=_params("parallel", "parallel"),
        name="rwkv_streams",
    )(pa, pa, pa, *consts)
    names = ("r", "v", "kk", "lwf", "lwb", "kf", "kb", "bf", "bb", "g", "bv")
    return dict(zip(names, outs))


def _bdot(a, b):
    return jnp.dot(a.astype(BF16), b.astype(BF16), preferred_element_type=F32)


def _bdot_nt(a, b):
    return lax.dot_general(a.astype(BF16), b.astype(BF16), (((1,), (1,)), ((), ())), preferred_element_type=F32)


def _bdot_tn(a, b):
    return lax.dot_general(a.astype(BF16), b.astype(BF16), (((0,), (0,)), ((), ())), preferred_element_type=F32)


def _running_sum(upto, x):
    tri = upto.astype(BF16)
    hi = x.astype(BF16)
    rest = x - hi.astype(F32)
    mid = rest.astype(BF16)
    lo = (rest - mid.astype(F32)).astype(BF16)
    dot = lambda p: jnp.dot(tri, p, preferred_element_type=F32)
    return dot(hi) + dot(mid) + dot(lo)


def _wkv_kernel(r_ref, v_ref, kk_ref, lw_ref, k_ref, b_ref, s0_ref, y_ref, s_ref, *, reverse, ch):
    c = pl.program_id(1)

    @pl.when(c == 0)
    def _():
        s_ref[...] = s0_ref[...]

    hp_count, rows, _ = r_ref.shape
    nsub = rows // ch
    pairs = range(hp_count)
    order = list(range(nsub - 1, -1, -1)) if reverse else list(range(nsub))
    chains = [(p, s) for s in order for p in pairs]
    ti = lax.broadcasted_iota(jnp.int32, (ch, ch), 0)
    tj = lax.broadcasted_iota(jnp.int32, (ch, ch), 1)
    upto = ((tj >= ti) if reverse else (tj <= ti)).astype(F32)
    ei = lax.broadcasted_iota(jnp.int32, (ch, LANES), 0)
    ej = lax.broadcasted_iota(jnp.int32, (ch, LANES), 1)
    eye2 = (ei == ej % HEAD).astype(F32)
    gi = lax.broadcasted_iota(jnp.int32, (2 * ch, 2 * LANES), 0)
    gj = lax.broadcasted_iota(jnp.int32, (2 * ch, 2 * LANES), 1) % ch
    it = gi % ch
    earlier = (gj > it) if reverse else (gj < it)
    gmask = earlier | ((gj == it) & (gi >= ch))
    first = lax.broadcasted_iota(jnp.int32, (1, LANES), 1) < HEAD
    si = lax.broadcasted_iota(jnp.int32, (LANES, LANES), 0) // HEAD
    sj = lax.broadcasted_iota(jnp.int32, (LANES, LANES), 1) // HEAD
    same_head = si == sj
    doublings = int(math.log2(ch)) - 1
    zero = jnp.zeros((), BF16)

    def split(x):
        x = x.astype(BF16)
        return jnp.concatenate([jnp.where(first, x, zero), jnp.where(first, zero, x)], axis=0)

    ar, brows, bke, vs, vsplit, wcs = {}, {}, {}, {}, {}, {}
    for p, s in chains:
        sl = slice(s * ch, (s + 1) * ch)
        lw = lw_ref[p, sl, :]
        cl = _running_sum(upto, lw)
        tot = jnp.sum(lw, axis=0, keepdims=True)
        e_in = jnp.exp(cl)
        e_out = jnp.exp(-cl)
        e_end = jnp.exp(tot - cl)
        a_t = -kk_ref[p, sl, :].astype(F32) * jnp.exp(cl - lw)
        b = b_ref[p, sl, :].astype(F32)
        k = k_ref[p, sl, :].astype(F32)
        ar[p, s] = jnp.concatenate([a_t, r_ref[p, sl, :].astype(F32) * e_in], axis=0).astype(BF16)
        brows[p, s] = jnp.concatenate([split(b * e_out), split(k * e_out)], axis=0)
        bke[p, s] = jnp.concatenate([b * e_end, k * e_end], axis=0).astype(BF16)
        vs[p, s] = v_ref[p, sl, :]
        vsplit[p, s] = split(vs[p, s])
        wcs[p, s] = jnp.exp(tot)
    g = {c: jnp.where(gmask, _bdot_nt(ar[c], brows[c]), 0.0) for c in chains}
    lakv = {c: _bdot(g[c][:ch, LANES:], vsplit[c]) for c in chains}
    inv = {c: eye2 + g[c][:ch, :LANES] for c in chains}
    pw = {c: _bdot(g[c][:ch, :LANES], split(g[c][:ch, :LANES])) for c in chains}
    for _ in range(doublings - 1):
        prod = {c: _bdot(pw[c], jnp.concatenate([split(inv[c]), split(pw[c])], axis=1)) for c in chains}
        inv = {c: inv[c] + prod[c][:, :LANES] for c in chains}
        pw = {c: prod[c][:, LANES:] for c in chains}
    inv = {c: inv[c] + _bdot(pw[c], split(inv[c])) for c in chains}
    state = [s_ref[p] for p in pairs]
    for s in order:
        ars = [_bdot_nt(ar[p, s], state[p]) for p in pairs]
        u = [_bdot(inv[p, s], split(ars[p][:ch] + lakv[p, s])) for p in pairs]
        ys = [ars[p][ch:] + _bdot(g[p, s][ch:, :], jnp.concatenate([split(u[p]), vsplit[p, s]], axis=0))
              for p in pairs]
        upd = [_bdot_tn(jnp.concatenate([u[p].astype(BF16), vs[p, s].astype(BF16)], axis=0), bke[p, s])
               for p in pairs]
        state = [state[p] * wcs[p, s] + jnp.where(same_head, upd[p], 0.0) for p in pairs]
        for p in pairs:
            y_ref[p, s * ch:(s + 1) * ch, :] = ys[p]
    for p in pairs:
        s_ref[p] = state[p]


def _wkv_scan(st, d, s0, reverse):
    r = st["r"]
    b, hp, t, _ = r.shape
    ch = min(SCAN_CHUNK, t)
    rows = min(SCAN_CHUNKS_PER_STEP * ch, t)
    nc = t // rows
    cidx = (lambda c: nc - 1 - c) if reverse else (lambda c: c)
    blk = pl.BlockSpec((None, hp, rows, LANES), lambda bi, c: (bi, 0, cidx(c), 0))
    st_spec = pl.BlockSpec((None, hp, LANES, LANES), lambda bi, c: (bi, 0, 0, 0))
    lw, k, bb = (st["lwf"], st["kf"], st["bf"]) if d == 0 else (st["lwb"], st["kb"], st["bb"])
    y, s_last = pl.pallas_call(
        functools.partial(_wkv_kernel, reverse=reverse, ch=ch),
        grid=(b, nc),
        in_specs=[blk] * 6 + [st_spec],
        out_specs=[blk, st_spec],
        out_shape=[jax.ShapeDtypeStruct((b, hp, t, LANES), F32),
                   jax.ShapeDtypeStruct((b, hp, LANES, LANES), F32)],
        compiler_params=_params("parallel", "arbitrary"),
        name="wkv_scan",
    )(r, st["v"], st["kk"], lw, k, bb, s0)
    return y, s_last


def _rwkv_out_kernel(yf_ref, yb_ref, bv_ref, g_ref, lng_ref, lnb_ref, o_ref):
    hp_count = yf_ref.shape[0]
    for hp in range(hp_count):
        c0, c1 = hp * LANES, (hp + 1) * LANES
        y = yf_ref[hp] + yb_ref[hp]
        mu = _head_sum(y) * (1.0 / HEAD)
        yc = y - mu
        var = _head_sum(yc * yc) * (1.0 / HEAD)
        yn = yc * lax.rsqrt(var + GN_EPS) * lng_ref[:, c0:c1] + lnb_ref[:, c0:c1]
        o_ref[:, c0:c1] = ((yn + bv_ref[:, c0:c1].astype(F32)) * g_ref[:, c0:c1].astype(F32)).astype(o_ref.dtype)


def _rwkv_out(yf, yb, st, lnx_g, lnx_b):
    b, hp, t, _ = yf.shape
    width = hp * LANES
    tt = min(256, t)
    pair = pl.BlockSpec((None, hp, tt, LANES), lambda bi, ti: (bi, 0, ti, 0))
    tok = pl.BlockSpec((None, tt, width), lambda bi, ti: (bi, ti, 0))
    vec = pl.BlockSpec((1, width), lambda bi, ti: (0, 0))
    return pl.pallas_call(
        _rwkv_out_kernel,
        grid=(b, t // tt),
        in_specs=[pair, pair, tok, tok, vec, vec],
        out_specs=tok,
        out_shape=jax.ShapeDtypeStruct((b, t, width), BF16),
        compiler_params=_params("parallel", "parallel"),
        name="rwkv_out",
    )(yf, yb, st["bv"], st["g"], lnx_g, lnx_b)


def _gelu(z):
    return 0.5 * z * (1.0 + lax.erf(z * (2.0 ** -0.5)))


def _sgu_kernel(u_ref, v_ref, lng_ref, lnb_ref, ws_ref, bs_ref, o_ref):
    heads = ws_ref.shape[0]
    width = u_ref.shape[1]
    hw = width // heads
    v = _gelu(v_ref[...].astype(F32))
    mu = jnp.mean(v, axis=-1, keepdims=True)
    vc = v - mu
    var = jnp.mean(vc * vc, axis=-1, keepdims=True)
    vn = (vc * lax.rsqrt(var + LN_EPS) * lng_ref[...] + lnb_ref[...]).astype(BF16)
    for h in range(heads):
        c0, c1 = h * hw, (h + 1) * hw
        s = jnp.dot(ws_ref[h], vn[:, c0:c1], preferred_element_type=F32) + bs_ref[:, h:h + 1]
        o_ref[:, c0:c1] = (_gelu(u_ref[:, c0:c1].astype(F32)) * s).astype(o_ref.dtype)


def _sgu(pb, ln_g, ln_b, ws, bs_t):
    b, t, w2 = pb.shape
    width = w2 // 2
    heads, chunk, _ = ws.shape
    tok_u = pl.BlockSpec((None, chunk, width), lambda bi, ti: (bi, ti, 0))
    tok_v = pl.BlockSpec((None, chunk, width), lambda bi, ti: (bi, ti, 1))
    full = lambda a: pl.BlockSpec(a.shape, lambda bi, ti: (0,) * a.ndim)
    return pl.pallas_call(
        _sgu_kernel,
        grid=(b, t // chunk),
        in_specs=[tok_u, tok_v, full(ln_g), full(ln_b), full(ws), full(bs_t)],
        out_specs=tok_u,
        out_shape=jax.ShapeDtypeStruct((b, t, width), BF16),
        compiler_params=_params("parallel", "parallel"),
        name="spatial_gating",
    )(pb, pb, ln_g, ln_b, ws, bs_t)


def _odd_kernel(xc_ref, bg_ref, cg_ref, xd_ref, pw_ref, ps_ref, cw_ref, o_ref, *, seg):
    tt, width = xc_ref.shape
    groups = pw_ref.shape[0]
    gc = width // groups
    ti = lax.broadcasted_iota(jnp.int32, (tt, tt), 0)
    tj = lax.broadcasted_iota(jnp.int32, (tt, tt), 1)
    same_seg = (ti // seg) == (tj // seg)
    pos = lax.broadcasted_iota(jnp.int32, (tt, 1), 0) % seg
    for gi in range(groups):
        win = POOL_WINDOWS[gi]
        lo = win // 2
        hi = win - 1 - lo
        band = (same_seg & (tj >= ti - lo) & (tj <= ti + hi)).astype(BF16)
        count = (jnp.minimum(pos + hi, seg - 1) - jnp.maximum(pos - lo, 0) + 1).astype(F32)
        c0, c1 = gi * gc, (gi + 1) * gc
        x = xc_ref[:, c0:c1]
        mean = jnp.dot(band, x, preferred_element_type=F32) / count
        p = (mean - x.astype(F32)).astype(BF16)
        y = jnp.dot(p, pw_ref[gi], preferred_element_type=F32) * ps_ref[:, c0:c1]
        o_ref[:, c0:c1] = y.astype(o_ref.dtype)
    z = cg_ref[...].astype(F32) * xd_ref[...].astype(F32)
    zp = jnp.where(pos == 0, 0.0, pltpu.roll(z, 1, 0))
    zn = jnp.where(pos == seg - 1, 0.0, pltpu.roll(z, tt - 1, 0))
    conv = cw_ref[0:1, :] * zp + cw_ref[1:2, :] * z + cw_ref[2:3, :] * zn
    o_ref[:, width:] = (bg_ref[...].astype(F32) * conv).astype(o_ref.dtype)


def _odd_mix(p, pool_w, pool_scale, sconv_w, seg):
    b, t, w4 = p.shape
    width = w4 // 4
    tt = min(max(256, seg), t)
    assert tt % seg == 0 and t % tt == 0
    col = lambda ci: pl.BlockSpec((None, tt, width), lambda bi, ti: (bi, ti, ci))
    full = lambda a: pl.BlockSpec(a.shape, lambda bi, ti: (0,) * a.ndim)
    return pl.pallas_call(
        functools.partial(_odd_kernel, seg=seg),
        grid=(b, t // tt),
        in_specs=[col(0), col(1), col(2), col(3), full(pool_w), full(pool_scale), full(sconv_w)],
        out_specs=pl.BlockSpec((None, tt, 2 * width), lambda bi, ti: (bi, ti, 0)),
        out_shape=jax.ShapeDtypeStruct((b, t, 2 * width), BF16),
        compiler_params=_params("parallel", "parallel"),
        name="pool_shortconv",
    )(p, p, p, p, pool_w, pool_scale, sconv_w)


def _ffn_mid_kernel(g_ref, gp_ref, gn_ref, val_ref, cw_ref, o_ref, *, shift):
    i = pl.program_id(1)
    last = pl.num_programs(1) - 1
    tt = g_ref.shape[0]
    hs = gp_ref.shape[0]
    g = g_ref[...].astype(F32)
    if hs == shift:
        prev_blk = jnp.where(i > 0, gp_ref[...].astype(F32), 0.0)
        next_blk = jnp.where(i < last, gn_ref[...].astype(F32), 0.0)
        if tt > shift:
            gp = jnp.concatenate([prev_blk, g[:tt - shift]], axis=0)
            gn = jnp.concatenate([g[shift:], next_blk], axis=0)
        else:
            gp, gn = prev_blk, next_blk
    else:
        row = lax.broadcasted_iota(jnp.int32, (tt, 1), 0)
        prev_row = jnp.where(i > 0, gp_ref[...].astype(F32)[hs - 1:hs, :], 0.0)
        next_row = jnp.where(i < last, gn_ref[...].astype(F32)[0:1, :], 0.0)
        gp = jnp.where(row == 0, prev_row, pltpu.roll(g, 1, 0))
        gn = jnp.where(row == tt - 1, next_row, pltpu.roll(g, tt - 1, 0))
    hw = 0.5 * cw_ref[...]
    half = hw[0:1, :] * gp + hw[1:2, :] * g + hw[2:3, :] * gn
    o_ref[...] = (half * (1.0 + jnp.tanh(half)) * val_ref[...].astype(F32)).astype(o_ref.dtype)


def _ffn_mid(up, conv_w, shift):
    b, t, f2 = up.shape
    f = f2 // 2
    tt = min(512, t)
    tc = f
    for cand in (1280, 1024, 512, 256, 128):
        if f % cand == 0:
            tc = cand
            break
    ncol = f // tc
    row_tile = SUBLANES * (4 // up.dtype.itemsize)
    hs = shift if shift % row_tile == 0 else row_tile
    assert shift == 1 or (hs == shift and tt % shift == 0)
    nh = t // hs
    main = pl.BlockSpec((None, tt, tc), lambda bi, ti, ci: (bi, ti, ci))
    prev = pl.BlockSpec((None, hs, tc), lambda bi, ti, ci: (bi, jnp.maximum(ti * (tt // hs) - 1, 0), ci))
    nxt = pl.BlockSpec((None, hs, tc), lambda bi, ti, ci: (bi, jnp.minimum((ti + 1) * (tt // hs), nh - 1), ci))
    val = pl.BlockSpec((None, tt, tc), lambda bi, ti, ci: (bi, ti, ci + ncol))
    cw = pl.BlockSpec((3, tc), lambda bi, ti, ci: (0, ci))
    return pl.pallas_call(
        functools.partial(_ffn_mid_kernel, shift=shift),
        grid=(b, t // tt, ncol),
        in_specs=[main, prev, nxt, val, cw],
        out_specs=main,
        out_shape=jax.ShapeDtypeStruct((b, t, f), BF16),
        compiler_params=_params("parallel", "parallel", "parallel"),
        name="convffn_gate",
    )(up, up, up, up, conv_w)


def _pad_cols(w, n):
    return jnp.pad(w, ((0, 0),) * (w.ndim - 1) + ((0, n - w.shape[-1]),))


def _even_layout(width, r_decay, r_aaa, r_gate):
    rp = _round_up(max(r_decay, r_aaa), LANES)
    rgp = _round_up(r_gate, LANES)
    na = _round_up(3 * width + 4 * rp + rgp, 1024)
    return dict(width=width, rp=rp, rgp=rgp, na=na)


def _permute_rwkv_cols(w, lay, r_decay, r_aaa, r_gate):
    width, rp, rgp, na = lay["width"], lay["rp"], lay["rgp"], lay["na"]
    o = 3 * width
    parts = [w[..., :o]]
    for seg in (r_decay, r_decay, r_aaa, r_aaa):
        parts.append(_pad_cols(w[..., o:o + seg], rp))
        o += seg
    parts.append(_pad_cols(w[..., o:o + r_gate], rgp))
    return _pad_cols(jnp.concatenate(parts, axis=-1), na)


def _pad_rows(w, n):
    pad = [(0, 0)] * w.ndim
    pad[-2] = (0, n - w.shape[-2])
    return jnp.pad(w, pad)


def kernel(x, c, ctx, c_ctx, ada_w, ada_b, ln_g, ln_b, ffn_w_up, ffn_conv, ffn_w_down, ev_w_in, ev_w_out, ev_shift, rwkv_w0, rwkv_w2, rwkv_a0, rwkv_a2, rwkv_g2, rwkv_kk, rwkv_ka, rwkv_rk, rwkv_lnx_g, rwkv_lnx_b, sgu_ln_g, sgu_ln_b, sgu_w, sgu_b, od_w_in, od_w_out, pool_w, pool_scale, sconv_w):
    batch, seq, d = x.shape
    depth = ada_w.shape[0]
    alpha = (2 * depth) ** 0.25
    last_cross = 2 * ((depth - 1) // 2)
    heads = rwkv_rk.shape[1]
    width_a = heads * HEAD
    r_decay, r_aaa, r_gate = rwkv_w2.shape[2], rwkv_a2.shape[2], rwkv_g2.shape[1]
    in_a = 3 * width_a + 2 * r_decay + 2 * r_aaa + r_gate
    lay = _even_layout(width_a, r_decay, r_aaa, r_gate)

    rows = _round_up(batch + 1, SUBLANES)
    cc = jnp.zeros((rows, d), F32).at[:batch].set(c).at[batch].set(c_ctx)
    mod4 = _modulation_all(cc, ada_w, ada_b).reshape(depth, rows, 1, 6 * d)
    lng4 = ln_g.reshape(depth, 2, 1, d)
    lnb4 = ln_b.reshape(depth, 2, 1, d)
    lat_row = lambda b: b
    ctx_row = lambda b: batch

    ev_in = ev_w_in.astype(BF16)
    w_a = _permute_rwkv_cols(ev_in[..., :in_a], lay, r_decay, r_aaa, r_gate)
    w_b = ev_in[..., in_a:]
    ev_out = ev_w_out.astype(BF16)
    od_in = od_w_in.astype(BF16)
    od_out = od_w_out.astype(BF16)
    w_up = ffn_w_up.astype(BF16)
    w_down = ffn_w_down.astype(BF16)
    shift_all = _permute_rwkv_cols(ev_shift, lay, r_decay, r_aaa, r_gate)

    h_lat, h_ctx = dict(h=x), dict(h=ctx)
    a_lat = _modulate(x, mod4, 0, lat_row, 1, 0)
    a_ctx = _modulate(ctx, mod4, 0, ctx_row, 1, 0)
    for i in range(depth):
        ctx_in = i <= last_cross
        ctx_out = i < last_cross
        j = i // 2
        mix_ctx = None
        if i % 2 == 0:
            rw = dict(
                shift=shift_all[j],
                w0=rwkv_w0[j], a0=rwkv_a0[j],
                w2=_pad_rows(rwkv_w2[j], lay["rp"]).astype(BF16),
                a2=_pad_rows(rwkv_a2[j], lay["rp"]).astype(BF16),
                g2=_pad_rows(rwkv_g2[j], lay["rgp"]).astype(BF16),
                kk=rwkv_kk[j].reshape(1, width_a), ka=rwkv_ka[j].reshape(1, width_a),
                rk=rwkv_rk[j].reshape(1, width_a))
            lnx_g = rwkv_lnx_g[j].reshape(1, width_a)
            lnx_b = rwkv_lnx_b[j].reshape(1, width_a)
            sgu_g = sgu_ln_g[j].reshape(1, -1)
            sgu_bb = sgu_ln_b[j].reshape(1, -1)
            sgu_ws = sgu_w[j].astype(BF16)
            sgu_bt = sgu_b[j].T

            st_lat = _rwkv_prep(_matmul([a_lat], w_a, j, BF16), lay, rw)
            st_ctx = _rwkv_prep(_matmul([a_ctx], w_a, j, BF16), lay, rw)
            s0 = jnp.zeros((batch, width_a // LANES, LANES, LANES), F32)
            yc_f, sc_f = _wkv_scan(st_ctx, 0, s0, False)
            yl_f, _ = _wkv_scan(st_lat, 0, sc_f, False)
            yc_b, sc_b = _wkv_scan(st_ctx, 1, s0, True)
            yl_b, _ = _wkv_scan(st_lat, 1, sc_b, True)
            ya_lat = _rwkv_out(yl_f, yl_b, st_lat, lnx_g, lnx_b)
            yb_lat = _sgu(_matmul([a_lat], w_b, j, BF16), sgu_g, sgu_bb, sgu_ws, sgu_bt)
            mix_lat, w_mix = [ya_lat, yb_lat], ev_out
            if ctx_out:
                ya_ctx = _rwkv_out(yc_f, yc_b, st_ctx, lnx_g, lnx_b)
                yb_ctx = _sgu(_matmul([a_ctx], w_b, j, BF16), sgu_g, sgu_bb, sgu_ws, sgu_bt)
                mix_ctx = [ya_ctx, yb_ctx]
        else:
            pw = pool_w[j].astype(BF16)
            ps = pool_scale[j].reshape(1, -1)
            mix_lat, w_mix = [_odd_mix(_matmul([a_lat], od_in, j, BF16), pw, ps, sconv_w[j], GRID_W)], od_out
            if ctx_out:
                mix_ctx = [_odd_mix(_matmul([a_ctx], od_in, j, BF16), pw, ps, sconv_w[j], ctx.shape[1])]

        nxt = (i + 1, 1, 0) if i + 1 < depth else None
        res = lambda stream, part, row: (stream, lng4, lnb4, mod4, i, part, row, alpha)
        z = _matmul(mix_lat, w_mix, j, resid=res(h_lat, 2, lat_row))
        h_lat, a2 = _ln_residual(z, mod4, lng4, lnb4, lat_row, i, 0, (i, 4, 3), False)
        gated = _ffn_mid(_matmul([a2], w_up, i, BF16), ffn_conv[i], GRID_W)
        z = _matmul([gated], w_down, i, resid=res(h_lat, 5, lat_row))
        h_lat, a_lat = _ln_residual(z, mod4, lng4, lnb4, lat_row, i, 1, nxt, nxt is None)
        if ctx_out:
            z = _matmul(mix_ctx, w_mix, j, resid=res(h_ctx, 2, ctx_row))
            h_ctx, a2 = _ln_residual(z, mod4, lng4, lnb4, ctx_row, i, 0, (i, 4, 3), False)
            gated = _ffn_mid(_matmul([a2], w_up, i, BF16), ffn_conv[i], 1)
            z = _matmul([gated], w_down, i, resid=res(h_ctx, 5, ctx_row))
            h_ctx, a_ctx = _ln_residual(z, mod4, lng4, lnb4, ctx_row, i, 1, nxt, False)
    return h_lat["h"]
```

```python
import functools
import math

import jax
import jax.numpy as jnp
from jax import lax
from jax.experimental import pallas as pl
from jax.experimental.pallas import tpu as pltpu

GRID_W = 64
POOL_WINDOWS = (2, 4, 8, 16)
LN_EPS = 1e-6
GN_EPS = 64e-5
HEAD = 64
SCAN_CHUNK = 64
SCAN_CHUNKS_PER_STEP = 4
LANES = 128
SUBLANES = 8
VMEM_LIMIT_BYTES = 60 * 1024 * 1024

F32 = jnp.float32
BF16 = jnp.bfloat16


def _round_up(n, m):
    return (n + m - 1) // m * m


def _params(*sem):
    return pltpu.CompilerParams(dimension_semantics=sem, vmem_limit_bytes=VMEM_LIMIT_BYTES)


def _sigmoid(z):
    return 0.5 + 0.5 * jnp.tanh(0.5 * z)


def _mod_kernel(c_ref, w_ref, b_ref, o_ref):
    c = c_ref[...]
    s = (c * _sigmoid(c)).astype(BF16)
    o_ref[...] = jnp.dot(s, w_ref[...].astype(BF16), preferred_element_type=F32) + b_ref[...]


def _modulation_all(cc, ada_w, ada_b):
    depth, d, n = ada_w.shape
    r = cc.shape[0]
    tn = min(512, n)
    return pl.pallas_call(
        _mod_kernel,
        grid=(depth, n // tn),
        in_specs=[pl.BlockSpec((r, d), lambda l, j: (0, 0)),
                  pl.BlockSpec((None, d, tn), lambda l, j: (l, 0, j)),
                  pl.BlockSpec((None, 1, tn), lambda l, j: (l, 0, j))],
        out_specs=pl.BlockSpec((None, r, tn), lambda l, j: (l, 0, j)),
        out_shape=jax.ShapeDtypeStruct((depth, r, n), F32),
        compiler_params=_params("parallel", "parallel"),
        name="adaln_modulation",
    )(cc, ada_w, ada_b.reshape(depth, 1, n))


def _vec_spec(d, layer, part, row):
    return pl.BlockSpec((None, None, 1, d), lambda b, t: (layer, row(b), 0, part))


def _modulate_kernel(x_ref, sc_ref, sh_ref, a_ref):
    a_ref[...] = (x_ref[...] * (1.0 + sc_ref[...]) + sh_ref[...]).astype(a_ref.dtype)


def _modulate(x, mod4, layer, row, sc_part, sh_part):
    b, t, d = x.shape
    tt = min(256, t)
    return pl.pallas_call(
        _modulate_kernel,
        grid=(b, t // tt),
        in_specs=[pl.BlockSpec((None, tt, d), lambda bi, ti: (bi, ti, 0)),
                  _vec_spec(d, layer, sc_part, row), _vec_spec(d, layer, sh_part, row)],
        out_specs=pl.BlockSpec((None, tt, d), lambda bi, ti: (bi, ti, 0)),
        out_shape=jax.ShapeDtypeStruct((b, t, d), BF16),
        compiler_params=_params("parallel", "parallel"),
        name="modulate",
    )(x, mod4, mod4)


def _ln_apply(z, mu, rstd, g, b):
    return (z - mu) * rstd * g + b


def _ln_kernel(*refs, with_next, keep_h):
    z_ref, g_ref, b_ref = refs[:3]
    rest = list(refs[3:])
    sc_ref, sh_ref = (rest.pop(0), rest.pop(0)) if with_next else (None, None)
    z = z_ref[...]
    mu = jnp.mean(z, axis=-1, keepdims=True)
    zc = z - mu
    rstd = lax.rsqrt(jnp.mean(zc * zc, axis=-1, keepdims=True) + LN_EPS)
    if keep_h:
        y = _ln_apply(z, mu, rstd, g_ref[...], b_ref[...])
        rest.pop(0)[...] = y
        if with_next:
            a_ref = rest.pop(0)
            a_ref[...] = (y * (1.0 + sc_ref[...]) + sh_ref[...]).astype(a_ref.dtype)
    else:
        if with_next:
            scale = 1.0 + sc_ref[...]
            a_ref = rest.pop(0)
            a_ref[...] = _ln_apply(z, mu, rstd, g_ref[...] * scale, b_ref[...] * scale + sh_ref[...]).astype(a_ref.dtype)
        mu_ref, rstd_ref = rest
        mu_ref[...] = jnp.broadcast_to(mu, mu_ref.shape)
        rstd_ref[...] = jnp.broadcast_to(rstd, rstd_ref.shape)


def _ln_residual(z, mod4, lng4, lnb4, row, layer, ln_idx, nxt, keep_h):
    b, t, d = z.shape
    tt = min(512, t)
    tok = pl.BlockSpec((None, tt, d), lambda bi, ti: (bi, ti, 0))
    stat = pl.BlockSpec((None, tt, LANES), lambda bi, ti: (bi, ti, 0))
    ln_spec = pl.BlockSpec((None, None, 1, d), lambda bi, ti: (layer, ln_idx, 0, 0))
    in_specs = [tok, ln_spec, ln_spec]
    args = [z, lng4, lnb4]
    out_specs, out_shape = [], []
    if nxt is not None:
        in_specs += [_vec_spec(d, nxt[0], nxt[1], row), _vec_spec(d, nxt[0], nxt[2], row)]
        args += [mod4, mod4]
    if keep_h:
        out_specs.append(tok)
        out_shape.append(jax.ShapeDtypeStruct((b, t, d), F32))
    if nxt is not None:
        out_specs.append(tok)
        out_shape.append(jax.ShapeDtypeStruct((b, t, d), BF16))
    if not keep_h:
        out_specs += [stat, stat]
        out_shape += [jax.ShapeDtypeStruct((b, t, LANES), F32)] * 2
    out = list(pl.pallas_call(
        functools.partial(_ln_kernel, with_next=nxt is not None, keep_h=keep_h),
        grid=(b, t // tt),
        in_specs=in_specs, out_specs=out_specs, out_shape=out_shape,
        compiler_params=_params("parallel", "parallel"),
        name="deepnorm_residual",
    )(*args))
    h = out.pop(0) if keep_h else None
    a = out.pop(0) if nxt is not None else None
    stream = dict(h=h) if keep_h else dict(z=z, mu=out[0], rstd=out[1], ln=(layer, ln_idx))
    return stream, a


def _mm_kernel(*refs, n_a, alpha, normed):
    a_refs, w_ref, o_ref = refs[:n_a], refs[n_a], refs[-1]
    acc = None
    k0 = 0
    for a_ref in a_refs:
        k = a_ref.shape[1]
        part = jnp.dot(a_ref[...], w_ref[k0:k0 + k, :], preferred_element_type=F32)
        acc = part if acc is None else acc + part
        k0 += k
    if alpha is not None:
        if normed:
            z_ref, mu_ref, rstd_ref, g_ref, b_ref, gate_ref = refs[n_a + 1:-1]
            acc = _ln_apply(z_ref[...], mu_ref[:, 0:1], rstd_ref[:, 0:1], alpha * g_ref[...], alpha * b_ref[...]) \
                + gate_ref[...] * acc
        else:
            h_ref, gate_ref = refs[n_a + 1:-1]
            acc = alpha * h_ref[...] + gate_ref[...] * acc
    o_ref[...] = acc.astype(o_ref.dtype)


def _matmul_tiles(m, k, n, out_bytes, max_rows, resid):
    budget = VMEM_LIMIT_BYTES - 2 * 1024 * 1024
    best = None
    for wbuf in ((2, 1) if resid else (2,)):
        for tm in (1024, 512, 256, 128, 64, 32, 16):
            if m % tm or tm > max_rows:
                continue
            for tn in (2048, 1792, 1536, 1280, 1024, 768, 512, 256, 128):
                if n % tn:
                    continue
                need = 2 * tm * k * 2 + wbuf * k * tn * 2 + 2 * tm * tn * out_bytes + tm * tn * 4
                if resid:
                    need += 3 * tm * tn * 4 + 4 * tm * LANES * 4
                if need <= budget and (best is None or tm * tn > best[0] * best[1]):
                    best = (tm, tn, wbuf)
    assert best is not None, (m, k, n)
    return best


def _matmul(a_parts, w, layer, out_dtype=F32, resid=None):
    lead = a_parts[0].shape[:-1]
    a2 = [a.reshape(-1, a.shape[-1]) for a in a_parts]
    m = a2[0].shape[0]
    _, k, n = w.shape
    assert sum(a.shape[1] for a in a2) == k
    seq = lead[-1]
    tm, tn, wbuf = _matmul_tiles(m, k, n, jnp.dtype(out_dtype).itemsize,
                                 seq if resid is not None else m, resid is not None)
    in_specs = [pl.BlockSpec((tm, a.shape[1]), lambda j, i: (i, 0)) for a in a2]
    w_mode = dict(pipeline_mode=pl.Buffered(1)) if wbuf == 1 else {}
    in_specs.append(pl.BlockSpec((None, k, tn), lambda j, i: (layer, 0, j), **w_mode))
    args = a2 + [w]
    alpha, normed = None, False
    if resid is not None:
        stream, lng4, lnb4, mod4, mod_layer, gate_part, row, alpha = resid
        per_seq = seq // tm
        tile = pl.BlockSpec((tm, tn), lambda j, i: (i, j))
        normed = "z" in stream
        if normed:
            ln_layer, ln_idx = stream["ln"]
            stat = pl.BlockSpec((tm, LANES), lambda j, i: (i, 0))
            ln_vec = pl.BlockSpec((None, None, 1, tn), lambda j, i: (ln_layer, ln_idx, 0, j))
            in_specs += [tile, stat, stat, ln_vec, ln_vec]
            args += [stream["z"].reshape(m, n), stream["mu"].reshape(m, LANES), stream["rstd"].reshape(m, LANES),
                     lng4, lnb4]
        else:
            in_specs.append(tile)
            args.append(stream["h"].reshape(m, n))
        in_specs.append(pl.BlockSpec((None, None, 1, tn),
                                     lambda j, i: (mod_layer, row(i // per_seq), 0, gate_part * (n // tn) + j)))
        args.append(mod4)
    out = pl.pallas_call(
        functools.partial(_mm_kernel, n_a=len(a2), alpha=alpha, normed=normed),
        grid=(n // tn, m // tm),
        in_specs=in_specs,
        out_specs=pl.BlockSpec((tm, tn), lambda j, i: (i, j)),
        out_shape=jax.ShapeDtypeStruct((m, n), out_dtype),
        compiler_params=_params("parallel", "parallel"),
        name="projection",
    )(*args)
    return out.reshape(lead + (n,))


def _head_sum(x):
    li = (lax.broadcasted_iota(jnp.int32, (2 * LANES, LANES), 0) % LANES) // HEAD
    lj = lax.broadcasted_iota(jnp.int32, (2 * LANES, LANES), 1) // HEAD
    ones = (li == lj).astype(BF16)
    hi = x.astype(BF16)
    lo = (x - hi.astype(F32)).astype(BF16)
    return jnp.dot(jnp.concatenate([hi, lo], axis=1), ones, preferred_element_type=F32)


def _prep_kernel(p_ref, pp_ref, pn_ref, shw_ref, w0_ref, w2_ref, a0_ref, a2_ref, g2_ref,
                 kkw_ref, ka_ref, rk_ref,
                 r_o, v_o, kk_o, lwf_o, lwb_o, kf_o, kb_o, bf_o, bb_o, g_o, bv_o,
                 *, width, rp, rgp):
    i = pl.program_id(1)
    last = pl.num_programs(1) - 1
    tt = p_ref.shape[0]
    row = lax.broadcasted_iota(jnp.int32, (tt, 1), 0)

    def shifted(c0, c1):
        x = p_ref[:, c0:c1].astype(F32)
        hs = pp_ref.shape[0]
        prev_row = jnp.where(i > 0, pp_ref[:, c0:c1].astype(F32)[hs - 1:hs], 0.0)
        next_row = jnp.where(i < last, pn_ref[:, c0:c1].astype(F32)[0:1], 0.0)
        xp = jnp.where(row == 0, prev_row, pltpu.roll(x, 1, 0))
        xn = jnp.where(row == tt - 1, next_row, pltpu.roll(x, tt - 1, 0))
        return shw_ref[0:1, c0:c1] * xp + shw_ref[1:2, c0:c1] * x + shw_ref[2:3, c0:c1] * xn

    hp_count = width // LANES
    o = 3 * width
    kkw = kkw_ref[...]
    ka = ka_ref[...]
    rk = rk_ref[...]
    g = jnp.dot(_sigmoid(shifted(o + 4 * rp, o + 4 * rp + rgp)).astype(BF16), g2_ref[...],
                preferred_element_type=F32)
    g_o[...] = g.astype(g_o.dtype)
    rates = []
    for d in range(2):
        wd = shifted(o + d * rp, o + (d + 1) * rp)
        ad = shifted(o + (2 + d) * rp, o + (3 + d) * rp)
        wl = w0_ref[d:d + 1, :] + jnp.dot(jnp.tanh(wd).astype(BF16), w2_ref[d], preferred_element_type=F32)
        lw = -math.exp(-0.5) * _sigmoid(wl)
        ar = _sigmoid(a0_ref[d:d + 1, :] + jnp.dot(ad.astype(BF16), a2_ref[d], preferred_element_type=F32))
        rates.append(ar)
        lw_o = lwf_o if d == 0 else lwb_o
        for hp in range(hp_count):
            lw_o[hp] = lw[:, hp * LANES:(hp + 1) * LANES]
    for hp in range(hp_count):
        c0, c1 = hp * LANES, (hp + 1) * LANES
        r = shifted(c0, c1)
        k = shifted(width + c0, width + c1)
        v = shifted(2 * width + c0, 2 * width + c1)
        kkr = k * kkw[:, c0:c1]
        kk = kkr * lax.rsqrt(jnp.maximum(_head_sum(kkr * kkr), 1e-24))
        kd0 = k * (1.0 + (rates[0][:, c0:c1] - 1.0) * ka[:, c0:c1])
        kd1 = k * (1.0 + (rates[1][:, c0:c1] - 1.0) * ka[:, c0:c1])
        r_o[hp] = r.astype(r_o.dtype)
        v_o[hp] = v.astype(v_o.dtype)
        kk_o[hp] = kk.astype(kk_o.dtype)
        kf_o[hp] = kd0.astype(kf_o.dtype)
        kb_o[hp] = kd1.astype(kb_o.dtype)
        bf_o[hp] = (kk * rates[0][:, c0:c1]).astype(bf_o.dtype)
        bb_o[hp] = (kk * rates[1][:, c0:c1]).astype(bb_o.dtype)
        bonus = _head_sum(r * (0.5 * (kd0 + kd1)) * rk[:, c0:c1])
        bv_o[:, c0:c1] = (bonus * v).astype(bv_o.dtype)


def _rwkv_prep(pa, lay, rw):
    b, t, na = pa.shape
    width, rp, rgp = lay["width"], lay["rp"], lay["rgp"]
    hp = width // LANES
    tt = min(128, t)
    hs = SUBLANES * (4 // pa.dtype.itemsize)
    nb = t // hs
    tok = lambda w: pl.BlockSpec((None, tt, w), lambda bi, ti: (bi, ti, 0))
    halo_prev = pl.BlockSpec((None, hs, na), lambda bi, ti: (bi, jnp.maximum(ti * (tt // hs) - 1, 0), 0))
    halo_next = pl.BlockSpec((None, hs, na), lambda bi, ti: (bi, jnp.minimum((ti + 1) * (tt // hs), nb - 1), 0))
    full = lambda a: pl.BlockSpec(a.shape, lambda bi, ti: (0,) * a.ndim)
    pair = pl.BlockSpec((None, hp, tt, LANES), lambda bi, ti: (bi, 0, ti, 0))
    pair_shapes = [jax.ShapeDtypeStruct((b, hp, t, LANES), dt) for dt in (BF16, BF16, BF16, F32, F32, BF16, BF16, BF16, BF16)]
    consts = [rw["shift"], rw["w0"], rw["w2"], rw["a0"], rw["a2"], rw["g2"], rw["kk"], rw["ka"], rw["rk"]]
    outs = pl.pallas_call(
        functools.partial(_prep_kernel, width=width, rp=rp, rgp=rgp),
        grid=(b, t // tt),
        in_specs=[tok(na), halo_prev, halo_next] + [full(a) for a in consts],
        out_specs=[pair] * 9 + [tok(width), tok(width)],
        out_shape=pair_shapes + [jax.ShapeDtypeStruct((b, t, width), BF16)] * 2,
        compiler_params=_params("parallel", "parallel"),
        name="rwkv_streams",
    )(pa, pa, pa, *consts)
    names = ("r", "v", "kk", "lwf", "lwb", "kf", "kb", "bf", "bb", "g", "bv")
    return dict(zip(names, outs))


def _bdot(a, b):
    return jnp.dot(a.astype(BF16), b.astype(BF16), preferred_element_type=F32)


def _bdot_nt(a, b):
    return lax.dot_general(a.astype(BF16), b.astype(BF16), (((1,), (1,)), ((), ())), preferred_element_type=F32)


def _bdot_tn(a, b):
    return lax.dot_general(a.astype(BF16), b.astype(BF16), (((0,), (0,)), ((), ())), preferred_element_type=F32)


def _running_sum(upto, x):
    tri = upto.astype(BF16)
    hi = x.astype(BF16)
    rest = x - hi.astype(F32)
    mid = rest.astype(BF16)
    lo = (rest - mid.astype(F32)).astype(BF16)
    dot = lambda p: jnp.dot(tri, p, preferred_element_type=F32)
    return dot(hi) + dot(mid) + dot(lo)


def _wkv_kernel(r_ref, v_ref, kk_ref, lw_ref, k_ref, b_ref, s0_ref, y_ref, s_ref, *, reverse, ch):
    c = pl.program_id(1)

    @pl.when(c == 0)
    def _():
        s_ref[...] = s0_ref[...]

    hp_count, rows, _ = r_ref.shape
    nsub = rows // ch
    pairs = range(hp_count)
    order = list(range(nsub - 1, -1, -1)) if reverse else list(range(nsub))
    chains = [(p, s) for s in order for p in pairs]
    ti = lax.broadcasted_iota(jnp.int32, (ch, ch), 0)
    tj = lax.broadcasted_iota(jnp.int32, (ch, ch), 1)
    upto = ((tj >= ti) if reverse else (tj <= ti)).astype(F32)
    ei = lax.broadcasted_iota(jnp.int32, (ch, LANES), 0)
    ej = lax.broadcasted_iota(jnp.int32, (ch, LANES), 1)
    eye2 = (ei == ej % HEAD).astype(F32)
    gi = lax.broadcasted_iota(jnp.int32, (2 * ch, 2 * LANES), 0)
    gj = lax.broadcasted_iota(jnp.int32, (2 * ch, 2 * LANES), 1) % ch
    it = gi % ch
    earlier = (gj > it) if reverse else (gj < it)
    gmask = earlier | ((gj == it) & (gi >= ch))
    first = lax.broadcasted_iota(jnp.int32, (1, LANES), 1) < HEAD
    si = lax.broadcasted_iota(jnp.int32, (LANES, LANES), 0) // HEAD
    sj = lax.broadcasted_iota(jnp.int32, (LANES, LANES), 1) // HEAD
    same_head = si == sj
    doublings = int(math.log2(ch)) - 1
    zero = jnp.zeros((), BF16)

    def split(x):
        x = x.astype(BF16)
        return jnp.concatenate([jnp.where(first, x, zero), jnp.where(first, zero, x)], axis=0)

    ar, brows, bke, vs, vsplit, wcs = {}, {}, {}, {}, {}, {}
    for p, s in chains:
        sl = slice(s * ch, (s + 1) * ch)
        lw = lw_ref[p, sl, :]
        cl = _running_sum(upto, lw)
        tot = jnp.sum(lw, axis=0, keepdims=True)
        e_in = jnp.exp(cl)
        e_out = jnp.exp(-cl)
        e_end = jnp.exp(tot - cl)
        a_t = -kk_ref[p, sl, :].astype(F32) * jnp.exp(cl - lw)
        b = b_ref[p, sl, :].astype(F32)
        k = k_ref[p, sl, :].astype(F32)
        ar[p, s] = jnp.concatenate([a_t, r_ref[p, sl, :].astype(F32) * e_in], axis=0).astype(BF16)
        brows[p, s] = jnp.concatenate([split(b * e_out), split(k * e_out)], axis=0)
        bke[p, s] = jnp.concatenate([b * e_end, k * e_end], axis=0).astype(BF16)
        vs[p, s] = v_ref[p, sl, :]
        vsplit[p, s] = split(vs[p, s])
        wcs[p, s] = jnp.exp(tot)
    g = {c: jnp.where(gmask, _bdot_nt(ar[c], brows[c]), 0.0) for c in chains}
    lakv = {c: _bdot(g[c][:ch, LANES:], vsplit[c]) for c in chains}
    inv = {c: eye2 + g[c][:ch, :LANES] for c in chains}
    pw = {c: _bdot(g[c][:ch, :LANES], split(g[c][:ch, :LANES])) for c in chains}
    for _ in range(doublings - 1):
        prod = {c: _bdot(pw[c], jnp.concatenate([split(inv[c]), split(pw[c])], axis=1)) for c in chains}
        inv = {c: inv[c] + prod[c][:, :LANES] for c in chains}
        pw = {c: prod[c][:, LANES:] for c in chains}
    inv = {c: inv[c] + _bdot(pw[c], split(inv[c])) for c in chains}
    state = [s_ref[p] for p in pairs]
    for s in order:
        ars = [_bdot_nt(ar[p, s], state[p]) for p in pairs]
        u = [_bdot(inv[p, s], split(ars[p][:ch] + lakv[p, s])) for p in pairs]
        ys = [ars[p][ch:] + _bdot(g[p, s][ch:, :], jnp.concatenate([split(u[p]), vsplit[p, s]], axis=0))
              for p in pairs]
        upd = [_bdot_tn(jnp.concatenate([u[p].astype(BF16), vs[p, s].astype(BF16)], axis=0), bke[p, s])
               for p in pairs]
        state = [state[p] * wcs[p, s] + jnp.where(same_head, upd[p], 0.0) for p in pairs]
        for p in pairs:
            y_ref[p, s * ch:(s + 1) * ch, :] = ys[p]
    for p in pairs:
        s_ref[p] = state[p]


def _wkv_scan(st, d, s0, reverse):
    r = st["r"]
    b, hp, t, _ = r.shape
    ch = min(SCAN_CHUNK, t)
    rows = min(SCAN_CHUNKS_PER_STEP * ch, t)
    nc = t // rows
    cidx = (lambda c: nc - 1 - c) if reverse else (lambda c: c)
    blk = pl.BlockSpec((None, hp, rows, LANES), lambda bi, c: (bi, 0, cidx(c), 0))
    st_spec = pl.BlockSpec((None, hp, LANES, LANES), lambda bi, c: (bi, 0, 0, 0))
    lw, k, bb = (st["lwf"], st["kf"], st["bf"]) if d == 0 else (st["lwb"], st["kb"], st["bb"])
    y, s_last = pl.pallas_call(
        functools.partial(_wkv_kernel, reverse=reverse, ch=ch),
        grid=(b, nc),
        in_specs=[blk] * 6 + [st_spec],
        out_specs=[blk, st_spec],
        out_shape=[jax.ShapeDtypeStruct((b, hp, t, LANES), F32),
                   jax.ShapeDtypeStruct((b, hp, LANES, LANES), F32)],
        compiler_params=_params("parallel", "arbitrary"),
        name="wkv_scan",
    )(r, st["v"], st["kk"], lw, k, bb, s0)
    return y, s_last


def _rwkv_out_kernel(yf_ref, yb_ref, bv_ref, g_ref, lng_ref, lnb_ref, o_ref):
    hp_count = yf_ref.shape[0]
    for hp in range(hp_count):
        c0, c1 = hp * LANES, (hp + 1) * LANES
        y = yf_ref[hp] + yb_ref[hp]
        mu = _head_sum(y) * (1.0 / HEAD)
        yc = y - mu
        var = _head_sum(yc * yc) * (1.0 / HEAD)
        yn = yc * lax.rsqrt(var + GN_EPS) * lng_ref[:, c0:c1] + lnb_ref[:, c0:c1]
        o_ref[:, c0:c1] = ((yn + bv_ref[:, c0:c1].astype(F32)) * g_ref[:, c0:c1].astype(F32)).astype(o_ref.dtype)


def _rwkv_out(yf, yb, st, lnx_g, lnx_b):
    b, hp, t, _ = yf.shape
    width = hp * LANES
    tt = min(256, t)
    pair = pl.BlockSpec((None, hp, tt, LANES), lambda bi, ti: (bi, 0, ti, 0))
    tok = pl.BlockSpec((None, tt, width), lambda bi, ti: (bi, ti, 0))
    vec = pl.BlockSpec((1, width), lambda bi, ti: (0, 0))
    return pl.pallas_call(
        _rwkv_out_kernel,
        grid=(b, t // tt),
        in_specs=[pair, pair, tok, tok, vec, vec],
        out_specs=tok,
        out_shape=jax.ShapeDtypeStruct((b, t, width), BF16),
        compiler_params=_params("parallel", "parallel"),
        name="rwkv_out",
    )(yf, yb, st["bv"], st["g"], lnx_g, lnx_b)


def _gelu(z):
    return 0.5 * z * (1.0 + lax.erf(z * (2.0 ** -0.5)))


def _sgu_kernel(u_ref, v_ref, lng_ref, lnb_ref, ws_ref, bs_ref, o_ref):
    heads = ws_ref.shape[0]
    width = u_ref.shape[1]
    hw = width // heads
    v = _gelu(v_ref[...].astype(F32))
    mu = jnp.mean(v, axis=-1, keepdims=True)
    vc = v - mu
    var = jnp.mean(vc * vc, axis=-1, keepdims=True)
    vn = (vc * lax.rsqrt(var + LN_EPS) * lng_ref[...] + lnb_ref[...]).astype(BF16)
    for h in range(heads):
        c0, c1 = h * hw, (h + 1) * hw
        s = jnp.dot(ws_ref[h], vn[:, c0:c1], preferred_element_type=F32) + bs_ref[:, h:h + 1]
        o_ref[:, c0:c1] = (_gelu(u_ref[:, c0:c1].astype(F32)) * s).astype(o_ref.dtype)


def _sgu(pb, ln_g, ln_b, ws, bs_t):
    b, t, w2 = pb.shape
    width = w2 // 2
    heads, chunk, _ = ws.shape
    tok_u = pl.BlockSpec((None, chunk, width), lambda bi, ti: (bi, ti, 0))
    tok_v = pl.BlockSpec((None, chunk, width), lambda bi, ti: (bi, ti, 1))
    full = lambda a: pl.BlockSpec(a.shape, lambda bi, ti: (0,) * a.ndim)
    return pl.pallas_call(
        _sgu_kernel,
        grid=(b, t // chunk),
        in_specs=[tok_u, tok_v, full(ln_g), full(ln_b), full(ws), full(bs_t)],
        out_specs=tok_u,
        out_shape=jax.ShapeDtypeStruct((b, t, width), BF16),
        compiler_params=_params("parallel", "parallel"),
        name="spatial_gating",
    )(pb, pb, ln_g, ln_b, ws, bs_t)


def _odd_kernel(xc_ref, bg_ref, cg_ref, xd_ref, pw_ref, ps_ref, cw_ref, o_ref, *, seg):
    tt, width = xc_ref.shape
    groups = pw_ref.shape[0]
    gc = width // groups
    ti = lax.broadcasted_iota(jnp.int32, (tt, tt), 0)
    tj = lax.broadcasted_iota(jnp.int32, (tt, tt), 1)
    same_seg = (ti // seg) == (tj // seg)
    pos = lax.broadcasted_iota(jnp.int32, (tt, 1), 0) % seg
    cols = [slice(gi * gc, (gi + 1) * gc) for gi in range(groups)]
    sums, inv_count = [], []
    for gi in range(groups):
        win = POOL_WINDOWS[gi]
        lo = win // 2
        hi = win - 1 - lo
        band = (same_seg & (tj >= ti - lo) & (tj <= ti + hi)).astype(BF16)
        count = (jnp.minimum(pos + hi, seg - 1) - jnp.maximum(pos - lo, 0) + 1).astype(F32)
        inv_count.append(1.0 / count)
        sums.append(jnp.dot(band, xc_ref[:, cols[gi]], preferred_element_type=F32))
    p = [(sums[gi] * inv_count[gi] - xc_ref[:, cols[gi]].astype(F32)).astype(BF16) for gi in range(groups)]
    ys = [jnp.dot(p[gi], pw_ref[gi], preferred_element_type=F32) for gi in range(groups)]
    for gi in range(groups):
        o_ref[:, cols[gi]] = (ys[gi] * ps_ref[:, cols[gi]]).astype(o_ref.dtype)
    z = cg_ref[...].astype(F32) * xd_ref[...].astype(F32)
    zp = jnp.where(pos == 0, 0.0, pltpu.roll(z, 1, 0))
    zn = jnp.where(pos == seg - 1, 0.0, pltpu.roll(z, tt - 1, 0))
    conv = cw_ref[0:1, :] * zp + cw_ref[1:2, :] * z + cw_ref[2:3, :] * zn
    o_ref[:, width:] = (bg_ref[...].astype(F32) * conv).astype(o_ref.dtype)


def _odd_mix(p, pool_w, pool_scale, sconv_w, seg):
    b, t, w4 = p.shape
    width = w4 // 4
    tt = min(max(256, seg), t)
    assert tt % seg == 0 and t % tt == 0
    col = lambda ci: pl.BlockSpec((None, tt, width), lambda bi, ti: (bi, ti, ci))
    full = lambda a: pl.BlockSpec(a.shape, lambda bi, ti: (0,) * a.ndim)
    return pl.pallas_call(
        functools.partial(_odd_kernel, seg=seg),
        grid=(b, t // tt),
        in_specs=[col(0), col(1), col(2), col(3), full(pool_w), full(pool_scale), full(sconv_w)],
        out_specs=pl.BlockSpec((None, tt, 2 * width), lambda bi, ti: (bi, ti, 0)),
        out_shape=jax.ShapeDtypeStruct((b, t, 2 * width), BF16),
        compiler_params=_params("parallel", "parallel"),
        name="pool_shortconv",
    )(p, p, p, p, pool_w, pool_scale, sconv_w)


def _ffn_mid_kernel(g_ref, gp_ref, gn_ref, val_ref, cw_ref, o_ref, *, shift):
    i = pl.program_id(1)
    last = pl.num_programs(1) - 1
    tt = g_ref.shape[0]
    hs = gp_ref.shape[0]
    g = g_ref[...].astype(F32)
    if hs == shift:
        prev_blk = jnp.where(i > 0, gp_ref[...].astype(F32), 0.0)
        next_blk = jnp.where(i < last, gn_ref[...].astype(F32), 0.0)
        if tt > shift:
            gp = jnp.concatenate([prev_blk, g[:tt - shift]], axis=0)
            gn = jnp.concatenate([g[shift:], next_blk], axis=0)
        else:
            gp, gn = prev_blk, next_blk
    else:
        row = lax.broadcasted_iota(jnp.int32, (tt, 1), 0)
        prev_row = jnp.where(i > 0, gp_ref[...].astype(F32)[hs - 1:hs, :], 0.0)
        next_row = jnp.where(i < last, gn_ref[...].astype(F32)[0:1, :], 0.0)
        gp = jnp.where(row == 0, prev_row, pltpu.roll(g, 1, 0))
        gn = jnp.where(row == tt - 1, next_row, pltpu.roll(g, tt - 1, 0))
    hw = 0.5 * cw_ref[...]
    half = hw[0:1, :] * gp + hw[1:2, :] * g + hw[2:3, :] * gn
    o_ref[...] = (half * (1.0 + jnp.tanh(half)) * val_ref[...].astype(F32)).astype(o_ref.dtype)


def _ffn_mid(up, conv_w, shift):
    b, t, f2 = up.shape
    f = f2 // 2
    tt = min(512, t)
    tc = f
    for cand in (1280, 1024, 512, 256, 128):
        if f % cand == 0:
            tc = cand
            break
    ncol = f // tc
    row_tile = SUBLANES * (4 // up.dtype.itemsize)
    hs = shift if shift % row_tile == 0 else row_tile
    assert shift == 1 or (hs == shift and tt % shift == 0)
    nh = t // hs
    main = pl.BlockSpec((None, tt, tc), lambda bi, ti, ci: (bi, ti, ci))
    prev = pl.BlockSpec((None, hs, tc), lambda bi, ti, ci: (bi, jnp.maximum(ti * (tt // hs) - 1, 0), ci))
    nxt = pl.BlockSpec((None, hs, tc), lambda bi, ti, ci: (bi, jnp.minimum((ti + 1) * (tt // hs), nh - 1), ci))
    val = pl.BlockSpec((None, tt, tc), lambda bi, ti, ci: (bi, ti, ci + ncol))
    cw = pl.BlockSpec((3, tc), lambda bi, ti, ci: (0, ci))
    return pl.pallas_call(
        functools.partial(_ffn_mid_kernel, shift=shift),
        grid=(b, t // tt, ncol),
        in_specs=[main, prev, nxt, val, cw],
        out_specs=main,
        out_shape=jax.ShapeDtypeStruct((b, t, f), BF16),
        compiler_params=_params("parallel", "parallel", "parallel"),
        name="convffn_gate",
    )(up, up, up, up, conv_w)


def _pad_cols(w, n):
    return jnp.pad(w, ((0, 0),) * (w.ndim - 1) + ((0, n - w.shape[-1]),))


def _even_layout(width, r_decay, r_aaa, r_gate):
    rp = _round_up(max(r_decay, r_aaa), LANES)
    rgp = _round_up(r_gate, LANES)
    na = _round_up(3 * width + 4 * rp + rgp, 1024)
    return dict(width=width, rp=rp, rgp=rgp, na=na)


def _permute_rwkv_cols(w, lay, r_decay, r_aaa, r_gate):
    width, rp, rgp, na = lay["width"], lay["rp"], lay["rgp"], lay["na"]
    o = 3 * width
    parts = [w[..., :o]]
    for seg in (r_decay, r_decay, r_aaa, r_aaa):
        parts.append(_pad_cols(w[..., o:o + seg], rp))
        o += seg
    parts.append(_pad_cols(w[..., o:o + r_gate], rgp))
    return _pad_cols(jnp.concatenate(parts, axis=-1), na)


def _pad_rows(w, n):
    pad = [(0, 0)] * w.ndim
    pad[-2] = (0, n - w.shape[-2])
    return jnp.pad(w, pad)


def kernel(x, c, ctx, c_ctx, ada_w, ada_b, ln_g, ln_b, ffn_w_up, ffn_conv, ffn_w_down, ev_w_in, ev_w_out, ev_shift, rwkv_w0, rwkv_w2, rwkv_a0, rwkv_a2, rwkv_g2, rwkv_kk, rwkv_ka, rwkv_rk, rwkv_lnx_g, rwkv_lnx_b, sgu_ln_g, sgu_ln_b, sgu_w, sgu_b, od_w_in, od_w_out, pool_w, pool_scale, sconv_w):
    batch, seq, d = x.shape
    depth = ada_w.shape[0]
    alpha = (2 * depth) ** 0.25
    last_cross = 2 * ((depth - 1) // 2)
    heads = rwkv_rk.shape[1]
    width_a = heads * HEAD
    r_decay, r_aaa, r_gate = rwkv_w2.shape[2], rwkv_a2.shape[2], rwkv_g2.shape[1]
    in_a = 3 * width_a + 2 * r_decay + 2 * r_aaa + r_gate
    lay = _even_layout(width_a, r_decay, r_aaa, r_gate)

    rows = _round_up(batch + 1, SUBLANES)
    cc = jnp.zeros((rows, d), F32).at[:batch].set(c).at[batch].set(c_ctx)
    mod4 = _modulation_all(cc, ada_w, ada_b).reshape(depth, rows, 1, 6 * d)
    lng4 = ln_g.reshape(depth, 2, 1, d)
    lnb4 = ln_b.reshape(depth, 2, 1, d)
    lat_row = lambda b: b
    ctx_row = lambda b: batch

    ev_in = ev_w_in.astype(BF16)
    w_a = _permute_rwkv_cols(ev_in[..., :in_a], lay, r_decay, r_aaa, r_gate)
    w_b = ev_in[..., in_a:]
    ev_out = ev_w_out.astype(BF16)
    od_in = od_w_in.astype(BF16)
    od_out = od_w_out.astype(BF16)
    w_up = ffn_w_up.astype(BF16)
    w_down = ffn_w_down.astype(BF16)
    shift_all = _permute_rwkv_cols(ev_shift, lay, r_decay, r_aaa, r_gate)

    h_lat, h_ctx = dict(h=x), dict(h=ctx)
    a_lat = _modulate(x, mod4, 0, lat_row, 1, 0)
    a_ctx = _modulate(ctx, mod4, 0, ctx_row, 1, 0)
    for i in range(depth):
        ctx_in = i <= last_cross
        ctx_out = i < last_cross
        j = i // 2
        mix_ctx = None
        if i % 2 == 0:
            rw = dict(
                shift=shift_all[j],
                w0=rwkv_w0[j], a0=rwkv_a0[j],
                w2=_pad_rows(rwkv_w2[j], lay["rp"]).astype(BF16),
                a2=_pad_rows(rwkv_a2[j], lay["rp"]).astype(BF16),
                g2=_pad_rows(rwkv_g2[j], lay["rgp"]).astype(BF16),
                kk=rwkv_kk[j].reshape(1, width_a), ka=rwkv_ka[j].reshape(1, width_a),
                rk=rwkv_rk[j].reshape(1, width_a))
            lnx_g = rwkv_lnx_g[j].reshape(1, width_a)
            lnx_b = rwkv_lnx_b[j].reshape(1, width_a)
            sgu_g = sgu_ln_g[j].reshape(1, -1)
            sgu_bb = sgu_ln_b[j].reshape(1, -1)
            sgu_ws = sgu_w[j].astype(BF16)
            sgu_bt = sgu_b[j].T

            st_lat = _rwkv_prep(_matmul([a_lat], w_a, j, BF16), lay, rw)
            st_ctx = _rwkv_prep(_matmul([a_ctx], w_a, j, BF16), lay, rw)
            s0 = jnp.zeros((batch, width_a // LANES, LANES, LANES), F32)
            yc_f, sc_f = _wkv_scan(st_ctx, 0, s0, False)
            yl_f, _ = _wkv_scan(st_lat, 0, sc_f, False)
            yc_b, sc_b = _wkv_scan(st_ctx, 1, s0, True)
            yl_b, _ = _wkv_scan(st_lat, 1, sc_b, True)
            ya_lat = _rwkv_out(yl_f, yl_b, st_lat, lnx_g, lnx_b)
            yb_lat = _sgu(_matmul([a_lat], w_b, j, BF16), sgu_g, sgu_bb, sgu_ws, sgu_bt)
            mix_lat, w_mix = [ya_lat, yb_lat], ev_out
            if ctx_out:
                ya_ctx = _rwkv_out(yc_f, yc_b, st_ctx, lnx_g, lnx_b)
                yb_ctx = _sgu(_matmul([a_ctx], w_b, j, BF16), sgu_g, sgu_bb, sgu_ws, sgu_bt)
                mix_ctx = [ya_ctx, yb_ctx]
        else:
            pw = pool_w[j].astype(BF16)
            ps = pool_scale[j].reshape(1, -1)
            mix_lat, w_mix = [_odd_mix(_matmul([a_lat], od_in, j, BF16), pw, ps, sconv_w[j], GRID_W)], od_out
            if ctx_out:
                mix_ctx = [_odd_mix(_matmul([a_ctx], od_in, j, BF16), pw, ps, sconv_w[j], ctx.shape[1])]

        nxt = (i + 1, 1, 0) if i + 1 < depth else None
        res = lambda stream, part, row: (stream, lng4, lnb4, mod4, i, part, row, alpha)
        z = _matmul(mix_lat, w_mix, j, resid=res(h_lat, 2, lat_row))
        h_lat, a2 = _ln_residual(z, mod4, lng4, lnb4, lat_row, i, 0, (i, 4, 3), False)
        gated = _ffn_mid(_matmul([a2], w_up, i, BF16), ffn_conv[i], GRID_W)
        z = _matmul([gated], w_down, i, resid=res(h_lat, 5, lat_row))
        h_lat, a_lat = _ln_residual(z, mod4, lng4, lnb4, lat_row, i, 1, nxt, nxt is None)
        if ctx_out:
            z = _matmul(mix_ctx, w_mix, j, resid=res(h_ctx, 2, ctx_row))
            h_ctx, a2 = _ln_residual(z, mod4, lng4, lnb4, ctx_row, i, 0, (i, 4, 3), False)
            gated = _ffn_mid(_matmul([a2], w_up, i, BF16), ffn_conv[i], 1)
            z = _matmul([gated], w_down, i, resid=res(h_ctx, 5, ctx_row))
            h_ctx, a_ctx = _ln_residual(z, mod4, lng4, lnb4, ctx_row, i, 1, nxt, False)
    return h_lat["h"]
```

```python
import functools
import math

import jax
import jax.numpy as jnp
from jax import lax
from jax.experimental import pallas as pl
from jax.experimental.pallas import tpu as pltpu

GRID_W = 64
POOL_WINDOWS = (2, 4, 8, 16)
LN_EPS = 1e-6
GN_EPS = 64e-5
HEAD = 64
SCAN_CHUNK = 64
SCAN_CHUNKS_PER_STEP = 4
LANES = 128
SUBLANES = 8
VMEM_LIMIT_BYTES = 60 * 1024 * 1024

F32 = jnp.float32
BF16 = jnp.bfloat16


def _round_up(n, m):
    return (n + m - 1) // m * m


def _params(*sem):
    return pltpu.CompilerParams(dimension_semantics=sem, vmem_limit_bytes=VMEM_LIMIT_BYTES)


def _sigmoid(z):
    return 0.5 + 0.5 * jnp.tanh(0.5 * z)


def _mod_kernel(c_ref, w_ref, b_ref, o_ref):
    c = c_ref[...]
    s = (c * _sigmoid(c)).astype(BF16)
    o_ref[...] = jnp.dot(s, w_ref[...].astype(BF16), preferred_element_type=F32) + b_ref[...]


def _modulation_all(cc, ada_w, ada_b):
    depth, d, n = ada_w.shape
    r = cc.shape[0]
    tn = min(512, n)
    return pl.pallas_call(
        _mod_kernel,
        grid=(depth, n // tn),
        in_specs=[pl.BlockSpec((r, d), lambda l, j: (0, 0)),
                  pl.BlockSpec((None, d, tn), lambda l, j: (l, 0, j)),
                  pl.BlockSpec((None, 1, tn), lambda l, j: (l, 0, j))],
        out_specs=pl.BlockSpec((None, r, tn), lambda l, j: (l, 0, j)),
        out_shape=jax.ShapeDtypeStruct((depth, r, n), F32),
        compiler_params=_params("parallel", "parallel"),
        name="adaln_modulation",
    )(cc, ada_w, ada_b.reshape(depth, 1, n))


def _vec_spec(d, layer, part, row):
    return pl.BlockSpec((None, None, 1, d), lambda b, t: (layer, row(b), 0, part))


def _modulate_kernel(x_ref, sc_ref, sh_ref, a_ref):
    a_ref[...] = (x_ref[...] * (1.0 + sc_ref[...]) + sh_ref[...]).astype(a_ref.dtype)


def _modulate(x, mod4, layer, row, sc_part, sh_part):
    b, t, d = x.shape
    tt = min(512, t)
    return pl.pallas_call(
        _modulate_kernel,
        grid=(b, t // tt),
        in_specs=[pl.BlockSpec((None, tt, d), lambda bi, ti: (bi, ti, 0)),
                  _vec_spec(d, layer, sc_part, row), _vec_spec(d, layer, sh_part, row)],
        out_specs=pl.BlockSpec((None, tt, d), lambda bi, ti: (bi, ti, 0)),
        out_shape=jax.ShapeDtypeStruct((b, t, d), BF16),
        compiler_params=_params("parallel", "parallel"),
        name="modulate",
    )(x, mod4, mod4)


def _ln_apply(z, mu, rstd, g, b):
    return (z - mu) * rstd * g + b


def _ln_kernel(*refs, with_next, keep_h):
    z_ref, g_ref, b_ref = refs[:3]
    rest = list(refs[3:])
    sc_ref, sh_ref = (rest.pop(0), rest.pop(0)) if with_next else (None, None)
    z = z_ref[...]
    mu = jnp.mean(z, axis=-1, keepdims=True)
    zc = z - mu
    rstd = lax.rsqrt(jnp.mean(zc * zc, axis=-1, keepdims=True) + LN_EPS)
    if keep_h:
        y = _ln_apply(z, mu, rstd, g_ref[...], b_ref[...])
        rest.pop(0)[...] = y
        if with_next:
            a_ref = rest.pop(0)
            a_ref[...] = (y * (1.0 + sc_ref[...]) + sh_ref[...]).astype(a_ref.dtype)
    else:
        if with_next:
            scale = 1.0 + sc_ref[...]
            a_ref = rest.pop(0)
            a_ref[...] = _ln_apply(z, mu, rstd, g_ref[...] * scale, b_ref[...] * scale + sh_ref[...]).astype(a_ref.dtype)
        mu_ref, rstd_ref = rest
        mu_ref[...] = jnp.broadcast_to(mu, mu_ref.shape)
        rstd_ref[...] = jnp.broadcast_to(rstd, rstd_ref.shape)


def _ln_residual(z, mod4, lng4, lnb4, row, layer, ln_idx, nxt, keep_h):
    b, t, d = z.shape
    tt = min(512, t)
    tok = pl.BlockSpec((None, tt, d), lambda bi, ti: (bi, ti, 0))
    stat = pl.BlockSpec((None, tt, LANES), lambda bi, ti: (bi, ti, 0))
    ln_spec = pl.BlockSpec((None, None, 1, d), lambda bi, ti: (layer, ln_idx, 0, 0))
    in_specs = [tok, ln_spec, ln_spec]
    args = [z, lng4, lnb4]
    out_specs, out_shape = [], []
    if nxt is not None:
        in_specs += [_vec_spec(d, nxt[0], nxt[1], row), _vec_spec(d, nxt[0], nxt[2], row)]
        args += [mod4, mod4]
    if keep_h:
        out_specs.append(tok)
        out_shape.append(jax.ShapeDtypeStruct((b, t, d), F32))
    if nxt is not None:
        out_specs.append(tok)
        out_shape.append(jax.ShapeDtypeStruct((b, t, d), BF16))
    if not keep_h:
        out_specs += [stat, stat]
        out_shape += [jax.ShapeDtypeStruct((b, t, LANES), F32)] * 2
    out = list(pl.pallas_call(
        functools.partial(_ln_kernel, with_next=nxt is not None, keep_h=keep_h),
        grid=(b, t // tt),
        in_specs=in_specs, out_specs=out_specs, out_shape=out_shape,
        compiler_params=_params("parallel", "parallel"),
        name="deepnorm_residual",
    )(*args))
    h = out.pop(0) if keep_h else None
    a = out.pop(0) if nxt is not None else None
    stream = dict(h=h) if keep_h else dict(z=z, mu=out[0], rstd=out[1], ln=(layer, ln_idx))
    return stream, a


def _mm_kernel(*refs, n_a, alpha, normed):
    a_refs, w_ref, o_ref = refs[:n_a], refs[n_a], refs[-1]
    acc = None
    k0 = 0
    for a_ref in a_refs:
        k = a_ref.shape[1]
        part = jnp.dot(a_ref[...], w_ref[k0:k0 + k, :], preferred_element_type=F32)
        acc = part if acc is None else acc + part
        k0 += k
    if alpha is not None:
        if normed:
            z_ref, mu_ref, rstd_ref, g_ref, b_ref, gate_ref = refs[n_a + 1:-1]
            acc = _ln_apply(z_ref[...], mu_ref[:, 0:1], rstd_ref[:, 0:1], alpha * g_ref[...], alpha * b_ref[...]) \
                + gate_ref[...] * acc
        else:
            h_ref, gate_ref = refs[n_a + 1:-1]
            acc = alpha * h_ref[...] + gate_ref[...] * acc
    o_ref[...] = acc.astype(o_ref.dtype)


def _matmul_tiles(m, k, n, out_bytes, max_rows, resid):
    budget = VMEM_LIMIT_BYTES - 2 * 1024 * 1024
    best = None
    for wbuf in ((2, 1) if resid else (2,)):
        for tm in (1024, 512, 256, 128, 64, 32, 16):
            if m % tm or tm > max_rows:
                continue
            for tn in (2048, 1792, 1536, 1280, 1024, 768, 512, 256, 128):
                if n % tn:
                    continue
                need = 2 * tm * k * 2 + wbuf * k * tn * 2 + 2 * tm * tn * out_bytes + tm * tn * 4
                if resid:
                    need += 3 * tm * tn * 4 + 4 * tm * LANES * 4
                if need <= budget and (best is None or tm * tn > best[0] * best[1]):
                    best = (tm, tn, wbuf)
    assert best is not None, (m, k, n)
    return best


def _matmul(a_parts, w, layer, out_dtype=F32, resid=None):
    lead = a_parts[0].shape[:-1]
    a2 = [a.reshape(-1, a.shape[-1]) for a in a_parts]
    m = a2[0].shape[0]
    _, k, n = w.shape
    assert sum(a.shape[1] for a in a2) == k
    seq = lead[-1]
    tm, tn, wbuf = _matmul_tiles(m, k, n, jnp.dtype(out_dtype).itemsize,
                                 seq if resid is not None else m, resid is not None)
    in_specs = [pl.BlockSpec((tm, a.shape[1]), lambda j, i: (i, 0)) for a in a2]
    w_mode = dict(pipeline_mode=pl.Buffered(1)) if wbuf == 1 else {}
    in_specs.append(pl.BlockSpec((None, k, tn), lambda j, i: (layer, 0, j), **w_mode))
    args = a2 + [w]
    alpha, normed = None, False
    if resid is not None:
        stream, lng4, lnb4, mod4, mod_layer, gate_part, row, alpha = resid
        per_seq = seq // tm
        tile = pl.BlockSpec((tm, tn), lambda j, i: (i, j))
        normed = "z" in stream
        if normed:
            ln_layer, ln_idx = stream["ln"]
            stat = pl.BlockSpec((tm, LANES), lambda j, i: (i, 0))
            ln_vec = pl.BlockSpec((None, None, 1, tn), lambda j, i: (ln_layer, ln_idx, 0, j))
            in_specs += [tile, stat, stat, ln_vec, ln_vec]
            args += [stream["z"].reshape(m, n), stream["mu"].reshape(m, LANES), stream["rstd"].reshape(m, LANES),
                     lng4, lnb4]
        else:
            in_specs.append(tile)
            args.append(stream["h"].reshape(m, n))
        in_specs.append(pl.BlockSpec((None, None, 1, tn),
                                     lambda j, i: (mod_layer, row(i // per_seq), 0, gate_part * (n // tn) + j)))
        args.append(mod4)
    out = pl.pallas_call(
        functools.partial(_mm_kernel, n_a=len(a2), alpha=alpha, normed=normed),
        grid=(n // tn, m // tm),
        in_specs=in_specs,
        out_specs=pl.BlockSpec((tm, tn), lambda j, i: (i, j)),
        out_shape=jax.ShapeDtypeStruct((m, n), out_dtype),
        compiler_params=_params("parallel", "parallel"),
        name="projection",
    )(*args)
    return out.reshape(lead + (n,))


def _head_sum(x):
    li = (lax.broadcasted_iota(jnp.int32, (2 * LANES, LANES), 0) % LANES) // HEAD
    lj = lax.broadcasted_iota(jnp.int32, (2 * LANES, LANES), 1) // HEAD
    ones = (li == lj).astype(BF16)
    hi = x.astype(BF16)
    lo = (x - hi.astype(F32)).astype(BF16)
    return jnp.dot(jnp.concatenate([hi, lo], axis=1), ones, preferred_element_type=F32)


def _prep_kernel(p_ref, pp_ref, pn_ref, shw_ref, w0_ref, w2_ref, a0_ref, a2_ref, g2_ref,
                 kkw_ref, ka_ref, rk_ref,
                 r_o, v_o, kk_o, lwf_o, lwb_o, kf_o, kb_o, bf_o, bb_o, g_o, bv_o,
                 *, width, rp, rgp):
    i = pl.program_id(1)
    last = pl.num_programs(1) - 1
    tt = p_ref.shape[0]
    row = lax.broadcasted_iota(jnp.int32, (tt, 1), 0)

    def shifted(c0, c1):
        x = p_ref[:, c0:c1].astype(F32)
        hs = pp_ref.shape[0]
        prev_row = jnp.where(i > 0, pp_ref[:, c0:c1].astype(F32)[hs - 1:hs], 0.0)
        next_row = jnp.where(i < last, pn_ref[:, c0:c1].astype(F32)[0:1], 0.0)
        xp = jnp.where(row == 0, prev_row, pltpu.roll(x, 1, 0))
        xn = jnp.where(row == tt - 1, next_row, pltpu.roll(x, tt - 1, 0))
        return shw_ref[0:1, c0:c1] * xp + shw_ref[1:2, c0:c1] * x + shw_ref[2:3, c0:c1] * xn

    hp_count = width // LANES
    o = 3 * width
    kkw = kkw_ref[...]
    ka = ka_ref[...]
    rk = rk_ref[...]
    g = jnp.dot(_sigmoid(shifted(o + 4 * rp, o + 4 * rp + rgp)).astype(BF16), g2_ref[...],
                preferred_element_type=F32)
    g_o[...] = g.astype(g_o.dtype)
    rates = []
    for d in range(2):
        wd = shifted(o + d * rp, o + (d + 1) * rp)
        ad = shifted(o + (2 + d) * rp, o + (3 + d) * rp)
        wl = w0_ref[d:d + 1, :] + jnp.dot(jnp.tanh(wd).astype(BF16), w2_ref[d], preferred_element_type=F32)
        lw = -math.exp(-0.5) * _sigmoid(wl)
        ar = _sigmoid(a0_ref[d:d + 1, :] + jnp.dot(ad.astype(BF16), a2_ref[d], preferred_element_type=F32))
        rates.append(ar)
        lw_o = lwf_o if d == 0 else lwb_o
        for hp in range(hp_count):
            lw_o[hp] = lw[:, hp * LANES:(hp + 1) * LANES]
    for hp in range(hp_count):
        c0, c1 = hp * LANES, (hp + 1) * LANES
        r = shifted(c0, c1)
        k = shifted(width + c0, width + c1)
        v = shifted(2 * width + c0, 2 * width + c1)
        kkr = k * kkw[:, c0:c1]
        kk = kkr * lax.rsqrt(jnp.maximum(_head_sum(kkr * kkr), 1e-24))
        kd0 = k * (1.0 + (rates[0][:, c0:c1] - 1.0) * ka[:, c0:c1])
        kd1 = k * (1.0 + (rates[1][:, c0:c1] - 1.0) * ka[:, c0:c1])
        r_o[hp] = r.astype(r_o.dtype)
        v_o[hp] = v.astype(v_o.dtype)
        kk_o[hp] = kk.astype(kk_o.dtype)
        kf_o[hp] = kd0.astype(kf_o.dtype)
        kb_o[hp] = kd1.astype(kb_o.dtype)
        bf_o[hp] = (kk * rates[0][:, c0:c1]).astype(bf_o.dtype)
        bb_o[hp] = (kk * rates[1][:, c0:c1]).astype(bb_o.dtype)
        bonus = _head_sum(r * (0.5 * (kd0 + kd1)) * rk[:, c0:c1])
        bv_o[:, c0:c1] = (bonus * v).astype(bv_o.dtype)


def _rwkv_prep(pa, lay, rw):
    b, t, na = pa.shape
    width, rp, rgp = lay["width"], lay["rp"], lay["rgp"]
    hp = width // LANES
    tt = min(256, t)
    hs = SUBLANES * (4 // pa.dtype.itemsize)
    nb = t // hs
    tok = lambda w: pl.BlockSpec((None, tt, w), lambda bi, ti: (bi, ti, 0))
    halo_prev = pl.BlockSpec((None, hs, na), lambda bi, ti: (bi, jnp.maximum(ti * (tt // hs) - 1, 0), 0))
    halo_next = pl.BlockSpec((None, hs, na), lambda bi, ti: (bi, jnp.minimum((ti + 1) * (tt // hs), nb - 1), 0))
    full = lambda a: pl.BlockSpec(a.shape, lambda bi, ti: (0,) * a.ndim)
    pair = pl.BlockSpec((None, hp, tt, LANES), lambda bi, ti: (bi, 0, ti, 0))
    pair_shapes = [jax.ShapeDtypeStruct((b, hp, t, LANES), dt) for dt in (BF16, BF16, BF16, F32, F32, BF16, BF16, BF16, BF16)]
    consts = [rw["shift"], rw["w0"], rw["w2"], rw["a0"], rw["a2"], rw["g2"], rw["kk"], rw["ka"], rw["rk"]]
    outs = pl.pallas_call(
        functools.partial(_prep_kernel, width=width, rp=rp, rgp=rgp),
        grid=(b, t // tt),
        in_specs=[tok(na), halo_prev, halo_next] + [full(a) for a in consts],
        out_specs=[pair] * 9 + [tok(width), tok(width)],
        out_shape=pair_shapes + [jax.ShapeDtypeStruct((b, t, width), BF16)] * 2,
        compiler_params=_params("parallel", "parallel"),
        name="rwkv_streams",
    )(pa, pa, pa, *consts)
    names = ("r", "v", "kk", "lwf", "lwb", "kf", "kb", "bf", "bb", "g", "bv")
    return dict(zip(names, outs))


def _bdot(a, b):
    return jnp.dot(a.astype(BF16), b.astype(BF16), preferred_element_type=F32)


def _bdot_nt(a, b):
    return lax.dot_general(a.astype(BF16), b.astype(BF16), (((1,), (1,)), ((), ())), preferred_element_type=F32)


def _bdot_tn(a, b):
    return lax.dot_general(a.astype(BF16), b.astype(BF16), (((0,), (0,)), ((), ())), preferred_element_type=F32)


def _running_sum(upto, x):
    tri = upto.astype(BF16)
    hi = x.astype(BF16)
    rest = x - hi.astype(F32)
    mid = rest.astype(BF16)
    lo = (rest - mid.astype(F32)).astype(BF16)
    dot = lambda p: jnp.dot(tri, p, preferred_element_type=F32)
    return dot(hi) + dot(mid) + dot(lo)


def _wkv_kernel(r_ref, v_ref, kk_ref, lw_ref, k_ref, b_ref, s0_ref, y_ref, s_ref, *, reverse, ch):
    c = pl.program_id(1)

    @pl.when(c == 0)
    def _():
        s_ref[...] = s0_ref[...]

    hp_count, rows, _ = r_ref.shape
    nsub = rows // ch
    pairs = range(hp_count)
    order = list(range(nsub - 1, -1, -1)) if reverse else list(range(nsub))
    chains = [(p, s) for s in order for p in pairs]
    ti = lax.broadcasted_iota(jnp.int32, (ch, ch), 0)
    tj = lax.broadcasted_iota(jnp.int32, (ch, ch), 1)
    upto = ((tj >= ti) if reverse else (tj <= ti)).astype(F32)
    ei = lax.broadcasted_iota(jnp.int32, (ch, LANES), 0)
    ej = lax.broadcasted_iota(jnp.int32, (ch, LANES), 1)
    eye2 = (ei == ej % HEAD).astype(F32)
    gi = lax.broadcasted_iota(jnp.int32, (2 * ch, 2 * LANES), 0)
    gj = lax.broadcasted_iota(jnp.int32, (2 * ch, 2 * LANES), 1) % ch
    it = gi % ch
    earlier = (gj > it) if reverse else (gj < it)
    gmask = earlier | ((gj == it) & (gi >= ch))
    first = lax.broadcasted_iota(jnp.int32, (1, LANES), 1) < HEAD
    si = lax.broadcasted_iota(jnp.int32, (LANES, LANES), 0) // HEAD
    sj = lax.broadcasted_iota(jnp.int32, (LANES, LANES), 1) // HEAD
    same_head = si == sj
    doublings = int(math.log2(ch)) - 1
    zero = jnp.zeros((), BF16)

    def split(x):
        x = x.astype(BF16)
        return jnp.concatenate([jnp.where(first, x, zero), jnp.where(first, zero, x)], axis=0)

    ar, brows, bke, vs, vsplit, wcs = {}, {}, {}, {}, {}, {}
    for p, s in chains:
        sl = slice(s * ch, (s + 1) * ch)
        lw = lw_ref[p, sl, :]
        cl = _running_sum(upto, lw)
        tot = jnp.sum(lw, axis=0, keepdims=True)
        e_in = jnp.exp(cl)
        e_out = jnp.exp(-cl)
        e_end = jnp.exp(tot - cl)
        a_t = -kk_ref[p, sl, :].astype(F32) * jnp.exp(cl - lw)
        b = b_ref[p, sl, :].astype(F32)
        k = k_ref[p, sl, :].astype(F32)
        ar[p, s] = jnp.concatenate([a_t, r_ref[p, sl, :].astype(F32) * e_in], axis=0).astype(BF16)
        brows[p, s] = jnp.concatenate([split(b * e_out), split(k * e_out)], axis=0)
        bke[p, s] = jnp.concatenate([b * e_end, k * e_end], axis=0).astype(BF16)
        vs[p, s] = v_ref[p, sl, :]
        vsplit[p, s] = split(vs[p, s])
        wcs[p, s] = jnp.exp(tot)
    g = {c: jnp.where(gmask, _bdot_nt(ar[c], brows[c]), 0.0) for c in chains}
    lakv = {c: _bdot(g[c][:ch, LANES:], vsplit[c]) for c in chains}
    inv = {c: eye2 + g[c][:ch, :LANES] for c in chains}
    pw = {c: _bdot(g[c][:ch, :LANES], split(g[c][:ch, :LANES])) for c in chains}
    for _ in range(doublings - 1):
        prod = {c: _bdot(pw[c], jnp.concatenate([split(inv[c]), split(pw[c])], axis=1)) for c in chains}
        inv = {c: inv[c] + prod[c][:, :LANES] for c in chains}
        pw = {c: prod[c][:, LANES:] for c in chains}
    inv = {c: inv[c] + _bdot(pw[c], split(inv[c])) for c in chains}
    state = [s_ref[p] for p in pairs]
    for s in order:
        ars = [_bdot_nt(ar[p, s], state[p]) for p in pairs]
        u = [_bdot(inv[p, s], split(ars[p][:ch] + lakv[p, s])) for p in pairs]
        ys = [ars[p][ch:] + _bdot(g[p, s][ch:, :], jnp.concatenate([split(u[p]), vsplit[p, s]], axis=0))
              for p in pairs]
        upd = [_bdot_tn(jnp.concatenate([u[p].astype(BF16), vs[p, s].astype(BF16)], axis=0), bke[p, s])
               for p in pairs]
        state = [state[p] * wcs[p, s] + jnp.where(same_head, upd[p], 0.0) for p in pairs]
        for p in pairs:
            y_ref[p, s * ch:(s + 1) * ch, :] = ys[p]
    for p in pairs:
        s_ref[p] = state[p]


def _wkv_scan(st, d, s0, reverse):
    r = st["r"]
    b, hp, t, _ = r.shape
    ch = min(SCAN_CHUNK, t)
    rows = min(SCAN_CHUNKS_PER_STEP * ch, t)
    nc = t // rows
    cidx = (lambda c: nc - 1 - c) if reverse else (lambda c: c)
    blk = pl.BlockSpec((None, hp, rows, LANES), lambda bi, c: (bi, 0, cidx(c), 0))
    st_spec = pl.BlockSpec((None, hp, LANES, LANES), lambda bi, c: (bi, 0, 0, 0))
    lw, k, bb = (st["lwf"], st["kf"], st["bf"]) if d == 0 else (st["lwb"], st["kb"], st["bb"])
    y, s_last = pl.pallas_call(
        functools.partial(_wkv_kernel, reverse=reverse, ch=ch),
        grid=(b, nc),
        in_specs=[blk] * 6 + [st_spec],
        out_specs=[blk, st_spec],
        out_shape=[jax.ShapeDtypeStruct((b, hp, t, LANES), F32),
                   jax.ShapeDtypeStruct((b, hp, LANES, LANES), F32)],
        compiler_params=_params("parallel", "arbitrary"),
        name="wkv_scan",
    )(r, st["v"], st["kk"], lw, k, bb, s0)
    return y, s_last


def _rwkv_out_kernel(yf_ref, yb_ref, bv_ref, g_ref, lng_ref, lnb_ref, o_ref):
    hp_count = yf_ref.shape[0]
    for hp in range(hp_count):
        c0, c1 = hp * LANES, (hp + 1) * LANES
        y = yf_ref[hp] + yb_ref[hp]
        mu = _head_sum(y) * (1.0 / HEAD)
        yc = y - mu
        var = _head_sum(yc * yc) * (1.0 / HEAD)
        yn = yc * lax.rsqrt(var + GN_EPS) * lng_ref[:, c0:c1] + lnb_ref[:, c0:c1]
        o_ref[:, c0:c1] = ((yn + bv_ref[:, c0:c1].astype(F32)) * g_ref[:, c0:c1].astype(F32)).astype(o_ref.dtype)


def _rwkv_out(yf, yb, st, lnx_g, lnx_b):
    b, hp, t, _ = yf.shape
    width = hp * LANES
    tt = min(512, t)
    pair = pl.BlockSpec((None, hp, tt, LANES), lambda bi, ti: (bi, 0, ti, 0))
    tok = pl.BlockSpec((None, tt, width), lambda bi, ti: (bi, ti, 0))
    vec = pl.BlockSpec((1, width), lambda bi, ti: (0, 0))
    return pl.pallas_call(
        _rwkv_out_kernel,
        grid=(b, t // tt),
        in_specs=[pair, pair, tok, tok, vec, vec],
        out_specs=tok,
        out_shape=jax.ShapeDtypeStruct((b, t, width), BF16),
        compiler_params=_params("parallel", "parallel"),
        name="rwkv_out",
    )(yf, yb, st["bv"], st["g"], lnx_g, lnx_b)


def _gelu(z):
    return 0.5 * z * (1.0 + lax.erf(z * (2.0 ** -0.5)))


def _sgu_kernel(u_ref, v_ref, lng_ref, lnb_ref, ws_ref, bs_ref, o_ref):
    heads, chunk, _ = ws_ref.shape
    tt, width = u_ref.shape
    hw = width // heads
    v = _gelu(v_ref[...].astype(F32))
    mu = jnp.mean(v, axis=-1, keepdims=True)
    vc = v - mu
    var = jnp.mean(vc * vc, axis=-1, keepdims=True)
    vn = (vc * lax.rsqrt(var + LN_EPS) * lng_ref[...] + lnb_ref[...]).astype(BF16)
    for r0 in range(0, tt, chunk):
        rows = slice(r0, r0 + chunk)
        for h in range(heads):
            c0, c1 = h * hw, (h + 1) * hw
            s = jnp.dot(ws_ref[h], vn[rows, c0:c1], preferred_element_type=F32) + bs_ref[:, h:h + 1]
            o_ref[rows, c0:c1] = (_gelu(u_ref[rows, c0:c1].astype(F32)) * s).astype(o_ref.dtype)


def _sgu(pb, ln_g, ln_b, ws, bs_t):
    b, t, w2 = pb.shape
    width = w2 // 2
    heads, chunk, _ = ws.shape
    tt = min(4 * chunk, t)
    tok_u = pl.BlockSpec((None, tt, width), lambda bi, ti: (bi, ti, 0))
    tok_v = pl.BlockSpec((None, tt, width), lambda bi, ti: (bi, ti, 1))
    full = lambda a: pl.BlockSpec(a.shape, lambda bi, ti: (0,) * a.ndim)
    return pl.pallas_call(
        _sgu_kernel,
        grid=(b, t // tt),
        in_specs=[tok_u, tok_v, full(ln_g), full(ln_b), full(ws), full(bs_t)],
        out_specs=tok_u,
        out_shape=jax.ShapeDtypeStruct((b, t, width), BF16),
        compiler_params=_params("parallel", "parallel"),
        name="spatial_gating",
    )(pb, pb, ln_g, ln_b, ws, bs_t)


def _odd_kernel(xc_ref, bg_ref, cg_ref, xd_ref, pw_ref, ps_ref, cw_ref, o_ref, *, seg):
    tt, width = xc_ref.shape
    groups = pw_ref.shape[0]
    gc = width // groups
    ti = lax.broadcasted_iota(jnp.int32, (tt, tt), 0)
    tj = lax.broadcasted_iota(jnp.int32, (tt, tt), 1)
    same_seg = (ti // seg) == (tj // seg)
    pos = lax.broadcasted_iota(jnp.int32, (tt, 1), 0) % seg
    cols = [slice(gi * gc, (gi + 1) * gc) for gi in range(groups)]
    sums, inv_count = [], []
    for gi in range(groups):
        win = POOL_WINDOWS[gi]
        lo = win // 2
        hi = win - 1 - lo
        band = (same_seg & (tj >= ti - lo) & (tj <= ti + hi)).astype(BF16)
        count = (jnp.minimum(pos + hi, seg - 1) - jnp.maximum(pos - lo, 0) + 1).astype(F32)
        inv_count.append(1.0 / count)
        sums.append(jnp.dot(band, xc_ref[:, cols[gi]], preferred_element_type=F32))
    p = [(sums[gi] * inv_count[gi] - xc_ref[:, cols[gi]].astype(F32)).astype(BF16) for gi in range(groups)]
    ys = [jnp.dot(p[gi], pw_ref[gi], preferred_element_type=F32) for gi in range(groups)]
    for gi in range(groups):
        o_ref[:, cols[gi]] = (ys[gi] * ps_ref[:, cols[gi]]).astype(o_ref.dtype)
    z = cg_ref[...].astype(F32) * xd_ref[...].astype(F32)
    zp = jnp.where(pos == 0, 0.0, pltpu.roll(z, 1, 0))
    zn = jnp.where(pos == seg - 1, 0.0, pltpu.roll(z, tt - 1, 0))
    conv = cw_ref[0:1, :] * zp + cw_ref[1:2, :] * z + cw_ref[2:3, :] * zn
    o_ref[:, width:] = (bg_ref[...].astype(F32) * conv).astype(o_ref.dtype)


def _odd_mix(p, pool_w, pool_scale, sconv_w, seg):
    b, t, w4 = p.shape
    width = w4 // 4
    tt = min(max(256, seg), t)
    assert tt % seg == 0 and t % tt == 0
    col = lambda ci: pl.BlockSpec((None, tt, width), lambda bi, ti: (bi, ti, ci))
    full = lambda a: pl.BlockSpec(a.shape, lambda bi, ti: (0,) * a.ndim)
    return pl.pallas_call(
        functools.partial(_odd_kernel, seg=seg),
        grid=(b, t // tt),
        in_specs=[col(0), col(1), col(2), col(3), full(pool_w), full(pool_scale), full(sconv_w)],
        out_specs=pl.BlockSpec((None, tt, 2 * width), lambda bi, ti: (bi, ti, 0)),
        out_shape=jax.ShapeDtypeStruct((b, t, 2 * width), BF16),
        compiler_params=_params("parallel", "parallel"),
        name="pool_shortconv",
    )(p, p, p, p, pool_w, pool_scale, sconv_w)


def _ffn_mid_kernel(g_ref, gp_ref, gn_ref, val_ref, cw_ref, o_ref, *, shift):
    i = pl.program_id(1)
    last = pl.num_programs(1) - 1
    tt = g_ref.shape[0]
    hs = gp_ref.shape[0]
    g = g_ref[...].astype(F32)
    if hs == shift:
        prev_blk = jnp.where(i > 0, gp_ref[...].astype(F32), 0.0)
        next_blk = jnp.where(i < last, gn_ref[...].astype(F32), 0.0)
        if tt > shift:
            gp = jnp.concatenate([prev_blk, g[:tt - shift]], axis=0)
            gn = jnp.concatenate([g[shift:], next_blk], axis=0)
        else:
            gp, gn = prev_blk, next_blk
    else:
        row = lax.broadcasted_iota(jnp.int32, (tt, 1), 0)
        prev_row = jnp.where(i > 0, gp_ref[...].astype(F32)[hs - 1:hs, :], 0.0)
        next_row = jnp.where(i < last, gn_ref[...].astype(F32)[0:1, :], 0.0)
        gp = jnp.where(row == 0, prev_row, pltpu.roll(g, 1, 0))
        gn = jnp.where(row == tt - 1, next_row, pltpu.roll(g, tt - 1, 0))
    hw = 0.5 * cw_ref[...]
    half = hw[0:1, :] * gp + hw[1:2, :] * g + hw[2:3, :] * gn
    o_ref[...] = (half * (1.0 + jnp.tanh(half)) * val_ref[...].astype(F32)).astype(o_ref.dtype)


def _ffn_mid(up, conv_w, shift):
    b, t, f2 = up.shape
    f = f2 // 2
    tt = min(512, t)
    tc = f
    for cand in (2560, 1280, 1024, 512, 256, 128):
        if f % cand == 0:
            tc = cand
            break
    ncol = f // tc
    row_tile = SUBLANES * (4 // up.dtype.itemsize)
    hs = shift if shift % row_tile == 0 else row_tile
    assert shift == 1 or (hs == shift and tt % shift == 0)
    nh = t // hs
    main = pl.BlockSpec((None, tt, tc), lambda bi, ti, ci: (bi, ti, ci))
    prev = pl.BlockSpec((None, hs, tc), lambda bi, ti, ci: (bi, jnp.maximum(ti * (tt // hs) - 1, 0), ci))
    nxt = pl.BlockSpec((None, hs, tc), lambda bi, ti, ci: (bi, jnp.minimum((ti + 1) * (tt // hs), nh - 1), ci))
    val = pl.BlockSpec((None, tt, tc), lambda bi, ti, ci: (bi, ti, ci + ncol))
    cw = pl.BlockSpec((3, tc), lambda bi, ti, ci: (0, ci))
    return pl.pallas_call(
        functools.partial(_ffn_mid_kernel, shift=shift),
        grid=(b, t // tt, ncol),
        in_specs=[main, prev, nxt, val, cw],
        out_specs=main,
        out_shape=jax.ShapeDtypeStruct((b, t, f), BF16),
        compiler_params=_params("parallel", "parallel", "parallel"),
        name="convffn_gate",
    )(up, up, up, up, conv_w)


def _pad_cols(w, n):
    return jnp.pad(w, ((0, 0),) * (w.ndim - 1) + ((0, n - w.shape[-1]),))


def _even_layout(width, r_decay, r_aaa, r_gate):
    rp = _round_up(max(r_decay, r_aaa), LANES)
    rgp = _round_up(r_gate, LANES)
    na = _round_up(3 * width + 4 * rp + rgp, 1024)
    return dict(width=width, rp=rp, rgp=rgp, na=na)


def _permute_rwkv_cols(w, lay, r_decay, r_aaa, r_gate):
    width, rp, rgp, na = lay["width"], lay["rp"], lay["rgp"], lay["na"]
    o = 3 * width
    parts = [w[..., :o]]
    for seg in (r_decay, r_decay, r_aaa, r_aaa):
        parts.append(_pad_cols(w[..., o:o + seg], rp))
        o += seg
    parts.append(_pad_cols(w[..., o:o + r_gate], rgp))
    return _pad_cols(jnp.concatenate(parts, axis=-1), na)


def _pad_rows(w, n):
    pad = [(0, 0)] * w.ndim
    pad[-2] = (0, n - w.shape[-2])
    return jnp.pad(w, pad)


def kernel(x, c, ctx, c_ctx, ada_w, ada_b, ln_g, ln_b, ffn_w_up, ffn_conv, ffn_w_down, ev_w_in, ev_w_out, ev_shift, rwkv_w0, rwkv_w2, rwkv_a0, rwkv_a2, rwkv_g2, rwkv_kk, rwkv_ka, rwkv_rk, rwkv_lnx_g, rwkv_lnx_b, sgu_ln_g, sgu_ln_b, sgu_w, sgu_b, od_w_in, od_w_out, pool_w, pool_scale, sconv_w):
    batch, seq, d = x.shape
    depth = ada_w.shape[0]
    alpha = (2 * depth) ** 0.25
    last_cross = 2 * ((depth - 1) // 2)
    heads = rwkv_rk.shape[1]
    width_a = heads * HEAD
    r_decay, r_aaa, r_gate = rwkv_w2.shape[2], rwkv_a2.shape[2], rwkv_g2.shape[1]
    in_a = 3 * width_a + 2 * r_decay + 2 * r_aaa + r_gate
    lay = _even_layout(width_a, r_decay, r_aaa, r_gate)

    rows = _round_up(batch + 1, SUBLANES)
    cc = jnp.zeros((rows, d), F32).at[:batch].set(c).at[batch].set(c_ctx)
    mod4 = _modulation_all(cc, ada_w, ada_b).reshape(depth, rows, 1, 6 * d)
    lng4 = ln_g.reshape(depth, 2, 1, d)
    lnb4 = ln_b.reshape(depth, 2, 1, d)
    lat_row = lambda b: b
    ctx_row = lambda b: batch

    ev_in = ev_w_in.astype(BF16)
    w_a = _permute_rwkv_cols(ev_in[..., :in_a], lay, r_decay, r_aaa, r_gate)
    w_b = ev_in[..., in_a:]
    ev_out = ev_w_out.astype(BF16)
    od_in = od_w_in.astype(BF16)
    od_out = od_w_out.astype(BF16)
    w_up = ffn_w_up.astype(BF16)
    w_down = ffn_w_down.astype(BF16)
    shift_all = _permute_rwkv_cols(ev_shift, lay, r_decay, r_aaa, r_gate)

    h_lat, h_ctx = dict(h=x), dict(h=ctx)
    a_lat = _modulate(x, mod4, 0, lat_row, 1, 0)
    a_ctx = _modulate(ctx, mod4, 0, ctx_row, 1, 0)
    for i in range(depth):
        ctx_in = i <= last_cross
        ctx_out = i < last_cross
        j = i // 2
        mix_ctx = None
        if i % 2 == 0:
            rw = dict(
                shift=shift_all[j],
                w0=rwkv_w0[j], a0=rwkv_a0[j],
                w2=_pad_rows(rwkv_w2[j], lay["rp"]).astype(BF16),
                a2=_pad_rows(rwkv_a2[j], lay["rp"]).astype(BF16),
                g2=_pad_rows(rwkv_g2[j], lay["rgp"]).astype(BF16),
                kk=rwkv_kk[j].reshape(1, width_a), ka=rwkv_ka[j].reshape(1, width_a),
                rk=rwkv_rk[j].reshape(1, width_a))
            lnx_g = rwkv_lnx_g[j].reshape(1, width_a)
            lnx_b = rwkv_lnx_b[j].reshape(1, width_a)
            sgu_g = sgu_ln_g[j].reshape(1, -1)
            sgu_bb = sgu_ln_b[j].reshape(1, -1)
            sgu_ws = sgu_w[j].astype(BF16)
            sgu_bt = sgu_b[j].T

            st_lat = _rwkv_prep(_matmul([a_lat], w_a, j, BF16), lay, rw)
            st_ctx = _rwkv_prep(_matmul([a_ctx], w_a, j, BF16), lay, rw)
            s0 = jnp.zeros((batch, width_a // LANES, LANES, LANES), F32)
            yc_f, sc_f = _wkv_scan(st_ctx, 0, s0, False)
            yl_f, _ = _wkv_scan(st_lat, 0, sc_f, False)
            yc_b, sc_b = _wkv_scan(st_ctx, 1, s0, True)
            yl_b, _ = _wkv_scan(st_lat, 1, sc_b, True)
            ya_lat = _rwkv_out(yl_f, yl_b, st_lat, lnx_g, lnx_b)
            yb_lat = _sgu(_matmul([a_lat], w_b, j, BF16), sgu_g, sgu_bb, sgu_ws, sgu_bt)
            mix_lat, w_mix = [ya_lat, yb_lat], ev_out
            if ctx_out:
                ya_ctx = _rwkv_out(yc_f, yc_b, st_ctx, lnx_g, lnx_b)
                yb_ctx = _sgu(_matmul([a_ctx], w_b, j, BF16), sgu_g, sgu_bb, sgu_ws, sgu_bt)
                mix_ctx = [ya_ctx, yb_ctx]
        else:
            pw = pool_w[j].astype(BF16)
            ps = pool_scale[j].reshape(1, -1)
            mix_lat, w_mix = [_odd_mix(_matmul([a_lat], od_in, j, BF16), pw, ps, sconv_w[j], GRID_W)], od_out
            if ctx_out:
                mix_ctx = [_odd_mix(_matmul([a_ctx], od_in, j, BF16), pw, ps, sconv_w[j], ctx.shape[1])]

        nxt = (i + 1, 1, 0) if i + 1 < depth else None
        res = lambda stream, part, row: (stream, lng4, lnb4, mod4, i, part, row, alpha)
        z = _matmul(mix_lat, w_mix, j, resid=res(h_lat, 2, lat_row))
        h_lat, a2 = _ln_residual(z, mod4, lng4, lnb4, lat_row, i, 0, (i, 4, 3), False)
        gated = _ffn_mid(_matmul([a2], w_up, i, BF16), ffn_conv[i], GRID_W)
        z = _matmul([gated], w_down, i, resid=res(h_lat, 5, lat_row))
        h_lat, a_lat = _ln_residual(z, mod4, lng4, lnb4, lat_row, i, 1, nxt, nxt is None)
        if ctx_out:
            z = _matmul(mix_ctx, w_mix, j, resid=res(h_ctx, 2, ctx_row))
            h_ctx, a2 = _ln_residual(z, mod4, lng4, lnb4, ctx_row, i, 0, (i, 4, 3), False)
            gated = _ffn_mid(_matmul([a2], w_up, i, BF16), ffn_conv[i], 1)
            z = _matmul([gated], w_down, i, resid=res(h_ctx, 5, ctx_row))
            h_ctx, a_ctx = _ln_residual(z, mod4, lng4, lnb4, ctx_row, i, 1, nxt, False)
    return h_lat["h"]
```

```python
import functools
import math

import jax
import jax.numpy as jnp
from jax import lax
from jax.experimental import pallas as pl
from jax.experimental.pallas import tpu as pltpu

GRID_W = 64
POOL_WINDOWS = (2, 4, 8, 16)
LN_EPS = 1e-6
GN_EPS = 64e-5
HEAD = 64
SCAN_CHUNK = 64
SCAN_CHUNKS_PER_STEP = 4
LANES = 128
SUBLANES = 8
VMEM_LIMIT_BYTES = 60 * 1024 * 1024

F32 = jnp.float32
BF16 = jnp.bfloat16


def _round_up(n, m):
    return (n + m - 1) // m * m


def _params(*sem):
    return pltpu.CompilerParams(dimension_semantics=sem, vmem_limit_bytes=VMEM_LIMIT_BYTES)


def _sigmoid(z):
    return 0.5 + 0.5 * jnp.tanh(0.5 * z)


def _mod_kernel(c_ref, w_ref, b_ref, o_ref):
    c = c_ref[...]
    s = (c * _sigmoid(c)).astype(BF16)
    o_ref[...] = jnp.dot(s, w_ref[...].astype(BF16), preferred_element_type=F32) + b_ref[...]


def _modulation_all(cc, ada_w, ada_b):
    depth, d, n = ada_w.shape
    r = cc.shape[0]
    tn = min(1024, n)
    return pl.pallas_call(
        _mod_kernel,
        grid=(depth, n // tn),
        in_specs=[pl.BlockSpec((r, d), lambda l, j: (0, 0)),
                  pl.BlockSpec((None, d, tn), lambda l, j: (l, 0, j)),
                  pl.BlockSpec((None, 1, tn), lambda l, j: (l, 0, j))],
        out_specs=pl.BlockSpec((None, r, tn), lambda l, j: (l, 0, j)),
        out_shape=jax.ShapeDtypeStruct((depth, r, n), F32),
        compiler_params=_params("parallel", "parallel"),
        name="adaln_modulation",
    )(cc, ada_w, ada_b.reshape(depth, 1, n))


def _vec_spec(d, layer, part, row):
    return pl.BlockSpec((None, None, 1, d), lambda b, t: (layer, row(b), 0, part))


def _modulate_kernel(x_ref, sc_ref, sh_ref, a_ref):
    a_ref[...] = (x_ref[...] * (1.0 + sc_ref[...]) + sh_ref[...]).astype(a_ref.dtype)


def _modulate(x, mod4, layer, row, sc_part, sh_part):
    b, t, d = x.shape
    tt = min(512, t)
    return pl.pallas_call(
        _modulate_kernel,
        grid=(b, t // tt),
        in_specs=[pl.BlockSpec((None, tt, d), lambda bi, ti: (bi, ti, 0)),
                  _vec_spec(d, layer, sc_part, row), _vec_spec(d, layer, sh_part, row)],
        out_specs=pl.BlockSpec((None, tt, d), lambda bi, ti: (bi, ti, 0)),
        out_shape=jax.ShapeDtypeStruct((b, t, d), BF16),
        compiler_params=_params("parallel", "parallel"),
        name="modulate",
    )(x, mod4, mod4)


def _ln_apply(z, mu, rstd, g, b):
    return (z - mu) * rstd * g + b


def _ln_kernel(*refs, with_next, keep_h):
    z_ref, g_ref, b_ref = refs[:3]
    rest = list(refs[3:])
    sc_ref, sh_ref = (rest.pop(0), rest.pop(0)) if with_next else (None, None)
    z = z_ref[...]
    mu = jnp.mean(z, axis=-1, keepdims=True)
    zc = z - mu
    rstd = lax.rsqrt(jnp.mean(zc * zc, axis=-1, keepdims=True) + LN_EPS)
    if keep_h:
        y = _ln_apply(z, mu, rstd, g_ref[...], b_ref[...])
        rest.pop(0)[...] = y
        if with_next:
            a_ref = rest.pop(0)
            a_ref[...] = (y * (1.0 + sc_ref[...]) + sh_ref[...]).astype(a_ref.dtype)
    else:
        if with_next:
            scale = 1.0 + sc_ref[...]
            a_ref = rest.pop(0)
            a_ref[...] = _ln_apply(z, mu, rstd, g_ref[...] * scale, b_ref[...] * scale + sh_ref[...]).astype(a_ref.dtype)
        mu_ref, rstd_ref = rest
        mu_ref[...] = jnp.broadcast_to(mu, mu_ref.shape)
        rstd_ref[...] = jnp.broadcast_to(rstd, rstd_ref.shape)


def _ln_residual(z, mod4, lng4, lnb4, row, layer, ln_idx, nxt, keep_h):
    b, t, d = z.shape
    tt = min(512, t)
    tok = pl.BlockSpec((None, tt, d), lambda bi, ti: (bi, ti, 0))
    stat = pl.BlockSpec((None, tt, LANES), lambda bi, ti: (bi, ti, 0))
    ln_spec = pl.BlockSpec((None, None, 1, d), lambda bi, ti: (layer, ln_idx, 0, 0))
    in_specs = [tok, ln_spec, ln_spec]
    args = [z, lng4, lnb4]
    out_specs, out_shape = [], []
    if nxt is not None:
        in_specs += [_vec_spec(d, nxt[0], nxt[1], row), _vec_spec(d, nxt[0], nxt[2], row)]
        args += [mod4, mod4]
    if keep_h:
        out_specs.append(tok)
        out_shape.append(jax.ShapeDtypeStruct((b, t, d), F32))
    if nxt is not None:
        out_specs.append(tok)
        out_shape.append(jax.ShapeDtypeStruct((b, t, d), BF16))
    if not keep_h:
        out_specs += [stat, stat]
        out_shape += [jax.ShapeDtypeStruct((b, t, LANES), F32)] * 2
    out = list(pl.pallas_call(
        functools.partial(_ln_kernel, with_next=nxt is not None, keep_h=keep_h),
        grid=(b, t // tt),
        in_specs=in_specs, out_specs=out_specs, out_shape=out_shape,
        compiler_params=_params("parallel", "parallel"),
        name="deepnorm_residual",
    )(*args))
    h = out.pop(0) if keep_h else None
    a = out.pop(0) if nxt is not None else None
    stream = dict(h=h) if keep_h else dict(z=z, mu=out[0], rstd=out[1], ln=(layer, ln_idx))
    return stream, a


def _mm_kernel(*refs, n_a, alpha, normed):
    a_refs, w_ref, o_ref = refs[:n_a], refs[n_a], refs[-1]
    acc = None
    k0 = 0
    for a_ref in a_refs:
        k = a_ref.shape[1]
        part = jnp.dot(a_ref[...], w_ref[k0:k0 + k, :], preferred_element_type=F32)
        acc = part if acc is None else acc + part
        k0 += k
    if alpha is not None:
        if normed:
            z_ref, mu_ref, rstd_ref, g_ref, b_ref, gate_ref = refs[n_a + 1:-1]
            acc = _ln_apply(z_ref[...], mu_ref[:, 0:1], rstd_ref[:, 0:1], alpha * g_ref[...], alpha * b_ref[...]) \
                + gate_ref[...] * acc
        else:
            h_ref, gate_ref = refs[n_a + 1:-1]
            acc = alpha * h_ref[...] + gate_ref[...] * acc
    o_ref[...] = acc.astype(o_ref.dtype)


def _matmul_tiles(m, k, n, out_bytes, max_rows, resid):
    budget = VMEM_LIMIT_BYTES - 2 * 1024 * 1024
    best = None
    for wbuf in ((2, 1) if resid else (2,)):
        for tm in (1024, 512, 256, 128, 64, 32, 16):
            if m % tm or tm > max_rows:
                continue
            for tn in (2048, 1792, 1536, 1280, 1024, 768, 512, 256, 128):
                if n % tn:
                    continue
                need = 2 * tm * k * 2 + wbuf * k * tn * 2 + 2 * tm * tn * out_bytes + tm * tn * 4
                if resid:
                    need += 3 * tm * tn * 4 + 4 * tm * LANES * 4
                if need <= budget and (best is None or tm * tn > best[0] * best[1]):
                    best = (tm, tn, wbuf)
    assert best is not None, (m, k, n)
    return best


def _matmul(a_parts, w, layer, out_dtype=F32, resid=None):
    lead = a_parts[0].shape[:-1]
    a2 = [a.reshape(-1, a.shape[-1]) for a in a_parts]
    m = a2[0].shape[0]
    _, k, n = w.shape
    assert sum(a.shape[1] for a in a2) == k
    seq = lead[-1]
    tm, tn, wbuf = _matmul_tiles(m, k, n, jnp.dtype(out_dtype).itemsize,
                                 seq if resid is not None else m, resid is not None)
    in_specs = [pl.BlockSpec((tm, a.shape[1]), lambda j, i: (i, 0)) for a in a2]
    w_mode = dict(pipeline_mode=pl.Buffered(1)) if wbuf == 1 else {}
    in_specs.append(pl.BlockSpec((None, k, tn), lambda j, i: (layer, 0, j), **w_mode))
    args = a2 + [w]
    alpha, normed = None, False
    if resid is not None:
        stream, lng4, lnb4, mod4, mod_layer, gate_part, row, alpha = resid
        per_seq = seq // tm
        tile = pl.BlockSpec((tm, tn), lambda j, i: (i, j))
        normed = "z" in stream
        if normed:
            ln_layer, ln_idx = stream["ln"]
            stat = pl.BlockSpec((tm, LANES), lambda j, i: (i, 0))
            ln_vec = pl.BlockSpec((None, None, 1, tn), lambda j, i: (ln_layer, ln_idx, 0, j))
            in_specs += [tile, stat, stat, ln_vec, ln_vec]
            args += [stream["z"].reshape(m, n), stream["mu"].reshape(m, LANES), stream["rstd"].reshape(m, LANES),
                     lng4, lnb4]
        else:
            in_specs.append(tile)
            args.append(stream["h"].reshape(m, n))
        in_specs.append(pl.BlockSpec((None, None, 1, tn),
                                     lambda j, i: (mod_layer, row(i // per_seq), 0, gate_part * (n // tn) + j)))
        args.append(mod4)
    out = pl.pallas_call(
        functools.partial(_mm_kernel, n_a=len(a2), alpha=alpha, normed=normed),
        grid=(n // tn, m // tm),
        in_specs=in_specs,
        out_specs=pl.BlockSpec((tm, tn), lambda j, i: (i, j)),
        out_shape=jax.ShapeDtypeStruct((m, n), out_dtype),
        compiler_params=_params("parallel", "parallel"),
        name="projection",
    )(*args)
    return out.reshape(lead + (n,))


def _head_sum(x):
    li = (lax.broadcasted_iota(jnp.int32, (2 * LANES, LANES), 0) % LANES) // HEAD
    lj = lax.broadcasted_iota(jnp.int32, (2 * LANES, LANES), 1) // HEAD
    ones = (li == lj).astype(BF16)
    hi = x.astype(BF16)
    lo = (x - hi.astype(F32)).astype(BF16)
    return jnp.dot(jnp.concatenate([hi, lo], axis=1), ones, preferred_element_type=F32)


def _prep_kernel(p_ref, pp_ref, pn_ref, shw_ref, w0_ref, w2_ref, a0_ref, a2_ref, g2_ref,
                 kkw_ref, ka_ref, rk_ref,
                 r_o, v_o, kk_o, lwf_o, lwb_o, kf_o, kb_o, bf_o, bb_o, g_o, bv_o,
                 *, width, rp, rgp):
    i = pl.program_id(1)
    last = pl.num_programs(1) - 1
    tt = p_ref.shape[0]
    row = lax.broadcasted_iota(jnp.int32, (tt, 1), 0)

    def shifted(c0, c1):
        x = p_ref[:, c0:c1].astype(F32)
        hs = pp_ref.shape[0]
        prev_row = jnp.where(i > 0, pp_ref[:, c0:c1].astype(F32)[hs - 1:hs], 0.0)
        next_row = jnp.where(i < last, pn_ref[:, c0:c1].astype(F32)[0:1], 0.0)
        xp = jnp.where(row == 0, prev_row, pltpu.roll(x, 1, 0))
        xn = jnp.where(row == tt - 1, next_row, pltpu.roll(x, tt - 1, 0))
        return shw_ref[0:1, c0:c1] * xp + shw_ref[1:2, c0:c1] * x + shw_ref[2:3, c0:c1] * xn

    hp_count = width // LANES
    o = 3 * width
    kkw = kkw_ref[...]
    ka = ka_ref[...]
    rk = rk_ref[...]
    g = jnp.dot(_sigmoid(shifted(o + 4 * rp, o + 4 * rp + rgp)).astype(BF16), g2_ref[...],
                preferred_element_type=F32)
    g_o[...] = g.astype(g_o.dtype)
    rates = []
    for d in range(2):
        wd = shifted(o + d * rp, o + (d + 1) * rp)
        ad = shifted(o + (2 + d) * rp, o + (3 + d) * rp)
        wl = w0_ref[d:d + 1, :] + jnp.dot(jnp.tanh(wd).astype(BF16), w2_ref[d], preferred_element_type=F32)
        lw = -math.exp(-0.5) * _sigmoid(wl)
        ar = _sigmoid(a0_ref[d:d + 1, :] + jnp.dot(ad.astype(BF16), a2_ref[d], preferred_element_type=F32))
        rates.append(ar)
        lw_o = lwf_o if d == 0 else lwb_o
        for hp in range(hp_count):
            lw_o[hp] = lw[:, hp * LANES:(hp + 1) * LANES]
    for hp in range(hp_count):
        c0, c1 = hp * LANES, (hp + 1) * LANES
        r = shifted(c0, c1)
        k = shifted(width + c0, width + c1)
        v = shifted(2 * width + c0, 2 * width + c1)
        kkr = k * kkw[:, c0:c1]
        kk = kkr * lax.rsqrt(jnp.maximum(_head_sum(kkr * kkr), 1e-24))
        kd0 = k * (1.0 + (rates[0][:, c0:c1] - 1.0) * ka[:, c0:c1])
        kd1 = k * (1.0 + (rates[1][:, c0:c1] - 1.0) * ka[:, c0:c1])
        r_o[hp] = r.astype(r_o.dtype)
        v_o[hp] = v.astype(v_o.dtype)
        kk_o[hp] = kk.astype(kk_o.dtype)
        kf_o[hp] = kd0.astype(kf_o.dtype)
        kb_o[hp] = kd1.astype(kb_o.dtype)
        bf_o[hp] = (kk * rates[0][:, c0:c1]).astype(bf_o.dtype)
        bb_o[hp] = (kk * rates[1][:, c0:c1]).astype(bb_o.dtype)
        bonus = _head_sum(r * (0.5 * (kd0 + kd1)) * rk[:, c0:c1])
        bv_o[:, c0:c1] = (bonus * v).astype(bv_o.dtype)


def _rwkv_prep(pa, lay, rw):
    b, t, na = pa.shape
    width, rp, rgp = lay["width"], lay["rp"], lay["rgp"]
    hp = width // LANES
    tt = min(256, t)
    hs = SUBLANES * (4 // pa.dtype.itemsize)
    nb = t // hs
    tok = lambda w: pl.BlockSpec((None, tt, w), lambda bi, ti: (bi, ti, 0))
    halo_prev = pl.BlockSpec((None, hs, na), lambda bi, ti: (bi, jnp.maximum(ti * (tt // hs) - 1, 0), 0))
    halo_next = pl.BlockSpec((None, hs, na), lambda bi, ti: (bi, jnp.minimum((ti + 1) * (tt // hs), nb - 1), 0))
    full = lambda a: pl.BlockSpec(a.shape, lambda bi, ti: (0,) * a.ndim)
    pair = pl.BlockSpec((None, hp, tt, LANES), lambda bi, ti: (bi, 0, ti, 0))
    pair_shapes = [jax.ShapeDtypeStruct((b, hp, t, LANES), dt) for dt in (BF16, BF16, BF16, F32, F32, BF16, BF16, BF16, BF16)]
    consts = [rw["shift"], rw["w0"], rw["w2"], rw["a0"], rw["a2"], rw["g2"], rw["kk"], rw["ka"], rw["rk"]]
    outs = pl.pallas_call(
        functools.partial(_prep_kernel, width=width, rp=rp, rgp=rgp),
        grid=(b, t // tt),
        in_specs=[tok(na), halo_prev, halo_next] + [full(a) for a in consts],
        out_specs=[pair] * 9 + [tok(width), tok(width)],
        out_shape=pair_shapes + [jax.ShapeDtypeStruct((b, t, width), BF16)] * 2,
        compiler_params=_params("parallel", "parallel"),
        name="rwkv_streams",
    )(pa, pa, pa, *consts)
    names = ("r", "v", "kk", "lwf", "lwb", "kf", "kb", "bf", "bb", "g", "bv")
    return dict(zip(names, outs))


def _bdot(a, b):
    return jnp.dot(a.astype(BF16), b.astype(BF16), preferred_element_type=F32)


def _bdot_nt(a, b):
    return lax.dot_general(a.astype(BF16), b.astype(BF16), (((1,), (1,)), ((), ())), preferred_element_type=F32)


def _bdot_tn(a, b):
    return lax.dot_general(a.astype(BF16), b.astype(BF16), (((0,), (0,)), ((), ())), preferred_element_type=F32)


def _running_sum(upto, x):
    tri = upto.astype(BF16)
    hi = x.astype(BF16)
    rest = x - hi.astype(F32)
    mid = rest.astype(BF16)
    lo = (rest - mid.astype(F32)).astype(BF16)
    dot = lambda p: jnp.dot(tri, p, preferred_element_type=F32)
    return dot(hi) + dot(mid) + dot(lo)


def _wkv_kernel(r_ref, v_ref, kk_ref, lw_ref, k_ref, b_ref, s0_ref, y_ref, s_ref, *, reverse, ch):
    c = pl.program_id(1)

    @pl.when(c == 0)
    def _():
        s_ref[...] = s0_ref[...]

    hp_count, rows, _ = r_ref.shape
    nsub = rows // ch
    pairs = range(hp_count)
    order = list(range(nsub - 1, -1, -1)) if reverse else list(range(nsub))
    chains = [(p, s) for s in order for p in pairs]
    ti = lax.broadcasted_iota(jnp.int32, (ch, ch), 0)
    tj = lax.broadcasted_iota(jnp.int32, (ch, ch), 1)
    upto = ((tj >= ti) if reverse else (tj <= ti)).astype(F32)
    ei = lax.broadcasted_iota(jnp.int32, (ch, LANES), 0)
    ej = lax.broadcasted_iota(jnp.int32, (ch, LANES), 1)
    eye2 = (ei == ej % HEAD).astype(F32)
    gi = lax.broadcasted_iota(jnp.int32, (2 * ch, 2 * LANES), 0)
    gj = lax.broadcasted_iota(jnp.int32, (2 * ch, 2 * LANES), 1) % ch
    it = gi % ch
    earlier = (gj > it) if reverse else (gj < it)
    gmask = earlier | ((gj == it) & (gi >= ch))
    first = lax.broadcasted_iota(jnp.int32, (1, LANES), 1) < HEAD
    si = lax.broadcasted_iota(jnp.int32, (LANES, LANES), 0) // HEAD
    sj = lax.broadcasted_iota(jnp.int32, (LANES, LANES), 1) // HEAD
    same_head = si == sj
    doublings = int(math.log2(ch)) - 1
    zero = jnp.zeros((), BF16)

    def split(x):
        x = x.astype(BF16)
        return jnp.concatenate([jnp.where(first, x, zero), jnp.where(first, zero, x)], axis=0)

    ar, brows, bke, vs, vsplit, wcs = {}, {}, {}, {}, {}, {}
    for p, s in chains:
        sl = slice(s * ch, (s + 1) * ch)
        lw = lw_ref[p, sl, :]
        cl = _running_sum(upto, lw)
        tot = jnp.sum(lw, axis=0, keepdims=True)
        e_in = jnp.exp(cl)
        e_out = jnp.exp(-cl)
        e_end = jnp.exp(tot - cl)
        a_t = -kk_ref[p, sl, :].astype(F32) * jnp.exp(cl - lw)
        b = b_ref[p, sl, :].astype(F32)
        k = k_ref[p, sl, :].astype(F32)
        ar[p, s] = jnp.concatenate([a_t, r_ref[p, sl, :].astype(F32) * e_in], axis=0).astype(BF16)
        brows[p, s] = jnp.concatenate([split(b * e_out), split(k * e_out)], axis=0)
        bke[p, s] = jnp.concatenate([b * e_end, k * e_end], axis=0).astype(BF16)
        vs[p, s] = v_ref[p, sl, :]
        vsplit[p, s] = split(vs[p, s])
        wcs[p, s] = jnp.exp(tot)
    g = {c: jnp.where(gmask, _bdot_nt(ar[c], brows[c]), 0.0) for c in chains}
    lakv = {c: _bdot(g[c][:ch, LANES:], vsplit[c]) for c in chains}
    inv = {c: eye2 + g[c][:ch, :LANES] for c in chains}
    pw = {c: _bdot(g[c][:ch, :LANES], split(g[c][:ch, :LANES])) for c in chains}
    for _ in range(doublings - 1):
        prod = {c: _bdot(pw[c], jnp.concatenate([split(inv[c]), split(pw[c])], axis=1)) for c in chains}
        inv = {c: inv[c] + prod[c][:, :LANES] for c in chains}
        pw = {c: prod[c][:, LANES:] for c in chains}
    inv = {c: inv[c] + _bdot(pw[c], split(inv[c])) for c in chains}
    state = [s_ref[p] for p in pairs]
    for s in order:
        ars = [_bdot_nt(ar[p, s], state[p]) for p in pairs]
        u = [_bdot(inv[p, s], split(ars[p][:ch] + lakv[p, s])) for p in pairs]
        ys = [ars[p][ch:] + _bdot(g[p, s][ch:, :], jnp.concatenate([split(u[p]), vsplit[p, s]], axis=0))
              for p in pairs]
        upd = [_bdot_tn(jnp.concatenate([u[p].astype(BF16), vs[p, s].astype(BF16)], axis=0), bke[p, s])
               for p in pairs]
        state = [state[p] * wcs[p, s] + jnp.where(same_head, upd[p], 0.0) for p in pairs]
        for p in pairs:
            y_ref[p, s * ch:(s + 1) * ch, :] = ys[p]
    for p in pairs:
        s_ref[p] = state[p]


def _wkv_scan(st, d, s0, reverse):
    r = st["r"]
    b, hp, t, _ = r.shape
    ch = min(SCAN_CHUNK, t)
    rows = min(SCAN_CHUNKS_PER_STEP * ch, t)
    nc = t // rows
    cidx = (lambda c: nc - 1 - c) if reverse else (lambda c: c)
    blk = pl.BlockSpec((None, hp, rows, LANES), lambda bi, c: (bi, 0, cidx(c), 0))
    st_spec = pl.BlockSpec((None, hp, LANES, LANES), lambda bi, c: (bi, 0, 0, 0))
    lw, k, bb = (st["lwf"], st["kf"], st["bf"]) if d == 0 else (st["lwb"], st["kb"], st["bb"])
    y, s_last = pl.pallas_call(
        functools.partial(_wkv_kernel, reverse=reverse, ch=ch),
        grid=(b, nc),
        in_specs=[blk] * 6 + [st_spec],
        out_specs=[blk, st_spec],
        out_shape=[jax.ShapeDtypeStruct((b, hp, t, LANES), F32),
                   jax.ShapeDtypeStruct((b, hp, LANES, LANES), F32)],
        compiler_params=_params("parallel", "arbitrary"),
        name="wkv_scan",
    )(r, st["v"], st["kk"], lw, k, bb, s0)
    return y, s_last


def _rwkv_out_kernel(yf_ref, yb_ref, bv_ref, g_ref, lng_ref, lnb_ref, o_ref):
    hp_count = yf_ref.shape[0]
    for hp in range(hp_count):
        c0, c1 = hp * LANES, (hp + 1) * LANES
        y = yf_ref[hp] + yb_ref[hp]
        mu = _head_sum(y) * (1.0 / HEAD)
        yc = y - mu
        var = _head_sum(yc * yc) * (1.0 / HEAD)
        yn = yc * lax.rsqrt(var + GN_EPS) * lng_ref[:, c0:c1] + lnb_ref[:, c0:c1]
        o_ref[:, c0:c1] = ((yn + bv_ref[:, c0:c1].astype(F32)) * g_ref[:, c0:c1].astype(F32)).astype(o_ref.dtype)


def _rwkv_out(yf, yb, st, lnx_g, lnx_b):
    b, hp, t, _ = yf.shape
    width = hp * LANES
    tt = min(512, t)
    pair = pl.BlockSpec((None, hp, tt, LANES), lambda bi, ti: (bi, 0, ti, 0))
    tok = pl.BlockSpec((None, tt, width), lambda bi, ti: (bi, ti, 0))
    vec = pl.BlockSpec((1, width), lambda bi, ti: (0, 0))
    return pl.pallas_call(
        _rwkv_out_kernel,
        grid=(b, t // tt),
        in_specs=[pair, pair, tok, tok, vec, vec],
        out_specs=tok,
        out_shape=jax.ShapeDtypeStruct((b, t, width), BF16),
        compiler_params=_params("parallel", "parallel"),
        name="rwkv_out",
    )(yf, yb, st["bv"], st["g"], lnx_g, lnx_b)


def _gelu(z):
    return 0.5 * z * (1.0 + lax.erf(z * (2.0 ** -0.5)))


def _sgu_kernel(u_ref, v_ref, lng_ref, lnb_ref, ws_ref, bs_ref, o_ref):
    heads, chunk, _ = ws_ref.shape
    tt, width = u_ref.shape
    hw = width // heads
    v = _gelu(v_ref[...].astype(F32))
    mu = jnp.mean(v, axis=-1, keepdims=True)
    vc = v - mu
    var = jnp.mean(vc * vc, axis=-1, keepdims=True)
    vn = (vc * lax.rsqrt(var + LN_EPS) * lng_ref[...] + lnb_ref[...]).astype(BF16)
    for r0 in range(0, tt, chunk):
        rows = slice(r0, r0 + chunk)
        for h in range(heads):
            c0, c1 = h * hw, (h + 1) * hw
            s = jnp.dot(ws_ref[h], vn[rows, c0:c1], preferred_element_type=F32) + bs_ref[:, h:h + 1]
            o_ref[rows, c0:c1] = (_gelu(u_ref[rows, c0:c1].astype(F32)) * s).astype(o_ref.dtype)


def _sgu(pb, ln_g, ln_b, ws, bs_t):
    b, t, w2 = pb.shape
    width = w2 // 2
    heads, chunk, _ = ws.shape
    tt = min(4 * chunk, t)
    tok_u = pl.BlockSpec((None, tt, width), lambda bi, ti: (bi, ti, 0))
    tok_v = pl.BlockSpec((None, tt, width), lambda bi, ti: (bi, ti, 1))
    full = lambda a: pl.BlockSpec(a.shape, lambda bi, ti: (0,) * a.ndim)
    return pl.pallas_call(
        _sgu_kernel,
        grid=(b, t // tt),
        in_specs=[tok_u, tok_v, full(ln_g), full(ln_b), full(ws), full(bs_t)],
        out_specs=tok_u,
        out_shape=jax.ShapeDtypeStruct((b, t, width), BF16),
        compiler_params=_params("parallel", "parallel"),
        name="spatial_gating",
    )(pb, pb, ln_g, ln_b, ws, bs_t)


def _odd_kernel(xc_ref, bg_ref, cg_ref, xd_ref, pw_ref, ps_ref, cw_ref, o_ref, *, seg):
    tt, width = xc_ref.shape
    groups = pw_ref.shape[0]
    gc = width // groups
    ti = lax.broadcasted_iota(jnp.int32, (tt, tt), 0)
    tj = lax.broadcasted_iota(jnp.int32, (tt, tt), 1)
    same_seg = (ti // seg) == (tj // seg)
    pos = lax.broadcasted_iota(jnp.int32, (tt, 1), 0) % seg
    cols = [slice(gi * gc, (gi + 1) * gc) for gi in range(groups)]
    sums, inv_count = [], []
    for gi in range(groups):
        win = POOL_WINDOWS[gi]
        lo = win // 2
        hi = win - 1 - lo
        band = (same_seg & (tj >= ti - lo) & (tj <= ti + hi)).astype(BF16)
        count = (jnp.minimum(pos + hi, seg - 1) - jnp.maximum(pos - lo, 0) + 1).astype(F32)
        inv_count.append(1.0 / count)
        sums.append(jnp.dot(band, xc_ref[:, cols[gi]], preferred_element_type=F32))
    p = [(sums[gi] * inv_count[gi] - xc_ref[:, cols[gi]].astype(F32)).astype(BF16) for gi in range(groups)]
    ys = [jnp.dot(p[gi], pw_ref[gi], preferred_element_type=F32) for gi in range(groups)]
    for gi in range(groups):
        o_ref[:, cols[gi]] = (ys[gi] * ps_ref[:, cols[gi]]).astype(o_ref.dtype)
    z = cg_ref[...].astype(F32) * xd_ref[...].astype(F32)
    zp = jnp.where(pos == 0, 0.0, pltpu.roll(z, 1, 0))
    zn = jnp.where(pos == seg - 1, 0.0, pltpu.roll(z, tt - 1, 0))
    conv = cw_ref[0:1, :] * zp + cw_ref[1:2, :] * z + cw_ref[2:3, :] * zn
    o_ref[:, width:] = (bg_ref[...].astype(F32) * conv).astype(o_ref.dtype)


def _odd_mix(p, pool_w, pool_scale, sconv_w, seg):
    b, t, w4 = p.shape
    width = w4 // 4
    tt = min(max(256, seg), t)
    assert tt % seg == 0 and t % tt == 0
    col = lambda ci: pl.BlockSpec((None, tt, width), lambda bi, ti: (bi, ti, ci))
    full = lambda a: pl.BlockSpec(a.shape, lambda bi, ti: (0,) * a.ndim)
    return pl.pallas_call(
        functools.partial(_odd_kernel, seg=seg),
        grid=(b, t // tt),
        in_specs=[col(0), col(1), col(2), col(3), full(pool_w), full(pool_scale), full(sconv_w)],
        out_specs=pl.BlockSpec((None, tt, 2 * width), lambda bi, ti: (bi, ti, 0)),
        out_shape=jax.ShapeDtypeStruct((b, t, 2 * width), BF16),
        compiler_params=_params("parallel", "parallel"),
        name="pool_shortconv",
    )(p, p, p, p, pool_w, pool_scale, sconv_w)


def _ffn_mid_kernel(g_ref, gp_ref, gn_ref, val_ref, cw_ref, o_ref, *, shift):
    i = pl.program_id(1)
    last = pl.num_programs(1) - 1
    tt = g_ref.shape[0]
    hs = gp_ref.shape[0]
    g = g_ref[...].astype(F32)
    if hs == shift:
        prev_blk = jnp.where(i > 0, gp_ref[...].astype(F32), 0.0)
        next_blk = jnp.where(i < last, gn_ref[...].astype(F32), 0.0)
        if tt > shift:
            gp = jnp.concatenate([prev_blk, g[:tt - shift]], axis=0)
            gn = jnp.concatenate([g[shift:], next_blk], axis=0)
        else:
            gp, gn = prev_blk, next_blk
    else:
        row = lax.broadcasted_iota(jnp.int32, (tt, 1), 0)
        prev_row = jnp.where(i > 0, gp_ref[...].astype(F32)[hs - 1:hs, :], 0.0)
        next_row = jnp.where(i < last, gn_ref[...].astype(F32)[0:1, :], 0.0)
        gp = jnp.where(row == 0, prev_row, pltpu.roll(g, 1, 0))
        gn = jnp.where(row == tt - 1, next_row, pltpu.roll(g, tt - 1, 0))
    hw = 0.5 * cw_ref[...]
    half = hw[0:1, :] * gp + hw[1:2, :] * g + hw[2:3, :] * gn
    o_ref[...] = (half * (1.0 + jnp.tanh(half)) * val_ref[...].astype(F32)).astype(o_ref.dtype)


def _ffn_mid(up, conv_w, shift):
    b, t, f2 = up.shape
    f = f2 // 2
    tt = min(512, t)
    tc = f
    for cand in (2560, 1280, 1024, 512, 256, 128):
        if f % cand == 0:
            tc = cand
            break
    ncol = f // tc
    row_tile = SUBLANES * (4 // up.dtype.itemsize)
    hs = shift if shift % row_tile == 0 else row_tile
    assert shift == 1 or (hs == shift and tt % shift == 0)
    nh = t // hs
    main = pl.BlockSpec((None, tt, tc), lambda bi, ti, ci: (bi, ti, ci))
    prev = pl.BlockSpec((None, hs, tc), lambda bi, ti, ci: (bi, jnp.maximum(ti * (tt // hs) - 1, 0), ci))
    nxt = pl.BlockSpec((None, hs, tc), lambda bi, ti, ci: (bi, jnp.minimum((ti + 1) * (tt // hs), nh - 1), ci))
    val = pl.BlockSpec((None, tt, tc), lambda bi, ti, ci: (bi, ti, ci + ncol))
    cw = pl.BlockSpec((3, tc), lambda bi, ti, ci: (0, ci))
    return pl.pallas_call(
        functools.partial(_ffn_mid_kernel, shift=shift),
        grid=(b, t // tt, ncol),
        in_specs=[main, prev, nxt, val, cw],
        out_specs=main,
        out_shape=jax.ShapeDtypeStruct((b, t, f), BF16),
        compiler_params=_params("parallel", "parallel", "parallel"),
        name="convffn_gate",
    )(up, up, up, up, conv_w)


def _pad_cols(w, n):
    return jnp.pad(w, ((0, 0),) * (w.ndim - 1) + ((0, n - w.shape[-1]),))


def _even_layout(width, r_decay, r_aaa, r_gate):
    rp = _round_up(max(r_decay, r_aaa), LANES)
    rgp = _round_up(r_gate, LANES)
    na = _round_up(3 * width + 4 * rp + rgp, 1024)
    return dict(width=width, rp=rp, rgp=rgp, na=na)


def _permute_rwkv_cols(w, lay, r_decay, r_aaa, r_gate):
    width, rp, rgp, na = lay["width"], lay["rp"], lay["rgp"], lay["na"]
    o = 3 * width
    parts = [w[..., :o]]
    for seg in (r_decay, r_decay, r_aaa, r_aaa):
        parts.append(_pad_cols(w[..., o:o + seg], rp))
        o += seg
    parts.append(_pad_cols(w[..., o:o + r_gate], rgp))
    return _pad_cols(jnp.concatenate(parts, axis=-1), na)


def _pad_rows(w, n):
    pad = [(0, 0)] * w.ndim
    pad[-2] = (0, n - w.shape[-2])
    return jnp.pad(w, pad)


def kernel(x, c, ctx, c_ctx, ada_w, ada_b, ln_g, ln_b, ffn_w_up, ffn_conv, ffn_w_down, ev_w_in, ev_w_out, ev_shift, rwkv_w0, rwkv_w2, rwkv_a0, rwkv_a2, rwkv_g2, rwkv_kk, rwkv_ka, rwkv_rk, rwkv_lnx_g, rwkv_lnx_b, sgu_ln_g, sgu_ln_b, sgu_w, sgu_b, od_w_in, od_w_out, pool_w, pool_scale, sconv_w):
    batch, seq, d = x.shape
    depth = ada_w.shape[0]
    alpha = (2 * depth) ** 0.25
    last_cross = 2 * ((depth - 1) // 2)
    heads = rwkv_rk.shape[1]
    width_a = heads * HEAD
    r_decay, r_aaa, r_gate = rwkv_w2.shape[2], rwkv_a2.shape[2], rwkv_g2.shape[1]
    in_a = 3 * width_a + 2 * r_decay + 2 * r_aaa + r_gate
    lay = _even_layout(width_a, r_decay, r_aaa, r_gate)

    rows = _round_up(batch + 1, SUBLANES)
    cc = jnp.zeros((rows, d), F32).at[:batch].set(c).at[batch].set(c_ctx)
    mod4 = _modulation_all(cc, ada_w, ada_b).reshape(depth, rows, 1, 6 * d)
    lng4 = ln_g.reshape(depth, 2, 1, d)
    lnb4 = ln_b.reshape(depth, 2, 1, d)
    lat_row = lambda b: b
    ctx_row = lambda b: batch

    ev_in = ev_w_in.astype(BF16)
    w_a = _permute_rwkv_cols(ev_in[..., :in_a], lay, r_decay, r_aaa, r_gate)
    w_b = ev_in[..., in_a:]
    ev_out = ev_w_out.astype(BF16)
    od_in = od_w_in.astype(BF16)
    od_out = od_w_out.astype(BF16)
    w_up = ffn_w_up.astype(BF16)
    w_down = ffn_w_down.astype(BF16)
    shift_all = _permute_rwkv_cols(ev_shift, lay, r_decay, r_aaa, r_gate)

    h_lat, h_ctx = dict(h=x), dict(h=ctx)
    a_lat = _modulate(x, mod4, 0, lat_row, 1, 0)
    a_ctx = _modulate(ctx, mod4, 0, ctx_row, 1, 0)
    for i in range(depth):
        ctx_out = i < last_cross
        j = i // 2
        mix_ctx = None
        if i % 2 == 0:
            rw = dict(
                shift=shift_all[j],
                w0=rwkv_w0[j], a0=rwkv_a0[j],
                w2=_pad_rows(rwkv_w2[j], lay["rp"]).astype(BF16),
                a2=_pad_rows(rwkv_a2[j], lay["rp"]).astype(BF16),
                g2=_pad_rows(rwkv_g2[j], lay["rgp"]).astype(BF16),
                kk=rwkv_kk[j].reshape(1, width_a), ka=rwkv_ka[j].reshape(1, width_a),
                rk=rwkv_rk[j].reshape(1, width_a))
            lnx_g = rwkv_lnx_g[j].reshape(1, width_a)
            lnx_b = rwkv_lnx_b[j].reshape(1, width_a)
            sgu_g = sgu_ln_g[j].reshape(1, -1)
            sgu_bb = sgu_ln_b[j].reshape(1, -1)
            sgu_ws = sgu_w[j].astype(BF16)
            sgu_bt = sgu_b[j].T

            st_lat = _rwkv_prep(_matmul([a_lat], w_a, j, BF16), lay, rw)
            st_ctx = _rwkv_prep(_matmul([a_ctx], w_a, j, BF16), lay, rw)
            s0 = jnp.zeros((batch, width_a // LANES, LANES, LANES), F32)
            yc_f, sc_f = _wkv_scan(st_ctx, 0, s0, False)
            yl_f, _ = _wkv_scan(st_lat, 0, sc_f, False)
            yc_b, sc_b = _wkv_scan(st_ctx, 1, s0, True)
            yl_b, _ = _wkv_scan(st_lat, 1, sc_b, True)
            ya_lat = _rwkv_out(yl_f, yl_b, st_lat, lnx_g, lnx_b)
            yb_lat = _sgu(_matmul([a_lat], w_b, j, BF16), sgu_g, sgu_bb, sgu_ws, sgu_bt)
            mix_lat, w_mix = [ya_lat, yb_lat], ev_out
            if ctx_out:
                ya_ctx = _rwkv_out(yc_f, yc_b, st_ctx, lnx_g, lnx_b)
                yb_ctx = _sgu(_matmul([a_ctx], w_b, j, BF16), sgu_g, sgu_bb, sgu_ws, sgu_bt)
                mix_ctx = [ya_ctx, yb_ctx]
        else:
            pw = pool_w[j].astype(BF16)
            ps = pool_scale[j].reshape(1, -1)
            mix_lat, w_mix = [_odd_mix(_matmul([a_lat], od_in, j, BF16), pw, ps, sconv_w[j], GRID_W)], od_out
            if ctx_out:
                mix_ctx = [_odd_mix(_matmul([a_ctx], od_in, j, BF16), pw, ps, sconv_w[j], ctx.shape[1])]

        nxt = (i + 1, 1, 0) if i + 1 < depth else None
        res = lambda stream, part, row: (stream, lng4, lnb4, mod4, i, part, row, alpha)
        z = _matmul(mix_lat, w_mix, j, resid=res(h_lat, 2, lat_row))
        h_lat, a2 = _ln_residual(z, mod4, lng4, lnb4, lat_row, i, 0, (i, 4, 3), False)
        gated = _ffn_mid(_matmul([a2], w_up, i, BF16), ffn_conv[i], GRID_W)
        z = _matmul([gated], w_down, i, resid=res(h_lat, 5, lat_row))
        h_lat, a_lat = _ln_residual(z, mod4, lng4, lnb4, lat_row, i, 1, nxt, nxt is None)
        if ctx_out:
            z = _matmul(mix_ctx, w_mix, j, resid=res(h_ctx, 2, ctx_row))
            h_ctx, a2 = _ln_residual(z, mod4, lng4, lnb4, ctx_row, i, 0, (i, 4, 3), False)
            gated = _ffn_mid(_matmul([a2], w_up, i, BF16), ffn_conv[i], 1)
            z = _matmul([gated], w_down, i, resid=res(h_ctx, 5, ctx_row))
            h_ctx, a_ctx = _ln_residual(z, mod4, lng4, lnb4, ctx_row, i, 1, nxt, False)
    return h_lat["h"]
```

```python
import functools
import math

import jax
import jax.numpy as jnp
from jax import lax
from jax.experimental import pallas as pl
from jax.experimental.pallas import tpu as pltpu

GRID_W = 64
POOL_WINDOWS = (2, 4, 8, 16)
LN_EPS = 1e-6
GN_EPS = 64e-5
HEAD = 64
SCAN_CHUNK = 64
SCAN_CHUNKS_PER_STEP = 4
LANES = 128
SUBLANES = 8
VMEM_LIMIT_BYTES = 60 * 1024 * 1024

F32 = jnp.float32
BF16 = jnp.bfloat16


def _round_up(n, m):
    return (n + m - 1) // m * m


def _params(*sem):
    return pltpu.CompilerParams(dimension_semantics=sem, vmem_limit_bytes=VMEM_LIMIT_BYTES)


def _sigmoid(z):
    return 0.5 + 0.5 * jnp.tanh(0.5 * z)


def _mod_kernel(c_ref, w_ref, b_ref, o_ref):
    c = c_ref[...]
    s = (c * _sigmoid(c)).astype(BF16)
    o_ref[...] = jnp.dot(s, w_ref[...].astype(BF16), preferred_element_type=F32) + b_ref[...]


def _modulation_all(cc, ada_w, ada_b):
    depth, d, n = ada_w.shape
    r = cc.shape[0]
    tn = min(1024, n)
    return pl.pallas_call(
        _mod_kernel,
        grid=(depth, n // tn),
        in_specs=[pl.BlockSpec((r, d), lambda l, j: (0, 0)),
                  pl.BlockSpec((None, d, tn), lambda l, j: (l, 0, j)),
                  pl.BlockSpec((None, 1, tn), lambda l, j: (l, 0, j))],
        out_specs=pl.BlockSpec((None, r, tn), lambda l, j: (l, 0, j)),
        out_shape=jax.ShapeDtypeStruct((depth, r, n), F32),
        compiler_params=_params("parallel", "parallel"),
        name="adaln_modulation",
    )(cc, ada_w, ada_b.reshape(depth, 1, n))


def _vec_spec(d, layer, part, row):
    return pl.BlockSpec((None, None, 1, d), lambda b, t: (layer, row(b), 0, part))


def _modulate_kernel(x_ref, sc_ref, sh_ref, a_ref):
    a_ref[...] = (x_ref[...] * (1.0 + sc_ref[...]) + sh_ref[...]).astype(a_ref.dtype)


def _modulate(x, mod4, layer, row, sc_part, sh_part):
    b, t, d = x.shape
    tt = min(512, t)
    return pl.pallas_call(
        _modulate_kernel,
        grid=(b, t // tt),
        in_specs=[pl.BlockSpec((None, tt, d), lambda bi, ti: (bi, ti, 0)),
                  _vec_spec(d, layer, sc_part, row), _vec_spec(d, layer, sh_part, row)],
        out_specs=pl.BlockSpec((None, tt, d), lambda bi, ti: (bi, ti, 0)),
        out_shape=jax.ShapeDtypeStruct((b, t, d), BF16),
        compiler_params=_params("parallel", "parallel"),
        name="modulate",
    )(x, mod4, mod4)


def _ln_apply(z, mu, rstd, g, b):
    return (z - mu) * rstd * g + b


def _ln_kernel(*refs, with_next, keep_h):
    z_ref, g_ref, b_ref = refs[:3]
    rest = list(refs[3:])
    sc_ref, sh_ref = (rest.pop(0), rest.pop(0)) if with_next else (None, None)
    z = z_ref[...]
    mu = jnp.mean(z, axis=-1, keepdims=True)
    zc = z - mu
    rstd = lax.rsqrt(jnp.mean(zc * zc, axis=-1, keepdims=True) + LN_EPS)
    if keep_h:
        y = _ln_apply(z, mu, rstd, g_ref[...], b_ref[...])
        rest.pop(0)[...] = y
        if with_next:
            a_ref = rest.pop(0)
            a_ref[...] = (y * (1.0 + sc_ref[...]) + sh_ref[...]).astype(a_ref.dtype)
    else:
        if with_next:
            scale = 1.0 + sc_ref[...]
            a_ref = rest.pop(0)
            a_ref[...] = _ln_apply(z, mu, rstd, g_ref[...] * scale, b_ref[...] * scale + sh_ref[...]).astype(a_ref.dtype)
        mu_ref, rstd_ref = rest
        mu_ref[...] = jnp.broadcast_to(mu, mu_ref.shape)
        rstd_ref[...] = jnp.broadcast_to(rstd, rstd_ref.shape)


def _ln_residual(z, mod4, lng4, lnb4, row, layer, ln_idx, nxt, keep_h):
    b, t, d = z.shape
    tt = min(512, t)
    tok = pl.BlockSpec((None, tt, d), lambda bi, ti: (bi, ti, 0))
    stat = pl.BlockSpec((None, tt, LANES), lambda bi, ti: (bi, ti, 0))
    ln_spec = pl.BlockSpec((None, None, 1, d), lambda bi, ti: (layer, ln_idx, 0, 0))
    in_specs = [tok, ln_spec, ln_spec]
    args = [z, lng4, lnb4]
    out_specs, out_shape = [], []
    if nxt is not None:
        in_specs += [_vec_spec(d, nxt[0], nxt[1], row), _vec_spec(d, nxt[0], nxt[2], row)]
        args += [mod4, mod4]
    if keep_h:
        out_specs.append(tok)
        out_shape.append(jax.ShapeDtypeStruct((b, t, d), F32))
    if nxt is not None:
        out_specs.append(tok)
        out_shape.append(jax.ShapeDtypeStruct((b, t, d), BF16))
    if not keep_h:
        out_specs += [stat, stat]
        out_shape += [jax.ShapeDtypeStruct((b, t, LANES), F32)] * 2
    out = list(pl.pallas_call(
        functools.partial(_ln_kernel, with_next=nxt is not None, keep_h=keep_h),
        grid=(b, t // tt),
        in_specs=in_specs, out_specs=out_specs, out_shape=out_shape,
        compiler_params=_params("parallel", "parallel"),
        name="deepnorm_residual",
    )(*args))
    h = out.pop(0) if keep_h else None
    a = out.pop(0) if nxt is not None else None
    stream = dict(h=h) if keep_h else dict(z=z, mu=out[0], rstd=out[1], ln=(layer, ln_idx))
    return stream, a


def _mm_kernel(*refs, n_a, alpha, normed):
    a_refs, w_ref, o_ref = refs[:n_a], refs[n_a], refs[-1]
    acc = None
    k0 = 0
    for a_ref in a_refs:
        k = a_ref.shape[1]
        part = jnp.dot(a_ref[...], w_ref[k0:k0 + k, :], preferred_element_type=F32)
        acc = part if acc is None else acc + part
        k0 += k
    if alpha is not None:
        if normed:
            z_ref, mu_ref, rstd_ref, g_ref, b_ref, gate_ref = refs[n_a + 1:-1]
            acc = _ln_apply(z_ref[...], mu_ref[:, 0:1], rstd_ref[:, 0:1], alpha * g_ref[...], alpha * b_ref[...]) \
                + gate_ref[...] * acc
        else:
            h_ref, gate_ref = refs[n_a + 1:-1]
            acc = alpha * h_ref[...] + gate_ref[...] * acc
    o_ref[...] = acc.astype(o_ref.dtype)


def _matmul_tiles(m, k, n, out_bytes, max_rows, resid):
    budget = VMEM_LIMIT_BYTES - 2 * 1024 * 1024
    best = None
    for wbuf in ((2, 1) if resid else (2,)):
        for tm in (1024, 512, 256, 128, 64, 32, 16):
            if m % tm or tm > max_rows:
                continue
            for tn in (2048, 1792, 1536, 1280, 1024, 768, 512, 256, 128):
                if n % tn:
                    continue
                need = 2 * tm * k * 2 + wbuf * k * tn * 2 + 2 * tm * tn * out_bytes + tm * tn * 4
                if resid:
                    need += 3 * tm * tn * 4 + 4 * tm * LANES * 4
                if need <= budget and (best is None or tm * tn > best[0] * best[1]):
                    best = (tm, tn, wbuf)
    assert best is not None, (m, k, n)
    return best


def _matmul(a_parts, w, layer, out_dtype=F32, resid=None):
    lead = a_parts[0].shape[:-1]
    a2 = [a.reshape(-1, a.shape[-1]) for a in a_parts]
    m = a2[0].shape[0]
    _, k, n = w.shape
    assert sum(a.shape[1] for a in a2) == k
    seq = lead[-1]
    tm, tn, wbuf = _matmul_tiles(m, k, n, jnp.dtype(out_dtype).itemsize,
                                 seq if resid is not None else m, resid is not None)
    in_specs = [pl.BlockSpec((tm, a.shape[1]), lambda j, i: (i, 0)) for a in a2]
    w_mode = dict(pipeline_mode=pl.Buffered(1)) if wbuf == 1 else {}
    in_specs.append(pl.BlockSpec((None, k, tn), lambda j, i: (layer, 0, j), **w_mode))
    args = a2 + [w]
    alpha, normed = None, False
    if resid is not None:
        stream, lng4, lnb4, mod4, mod_layer, gate_part, row, alpha = resid
        per_seq = seq // tm
        tile = pl.BlockSpec((tm, tn), lambda j, i: (i, j))
        normed = "z" in stream
        if normed:
            ln_layer, ln_idx = stream["ln"]
            stat = pl.BlockSpec((tm, LANES), lambda j, i: (i, 0))
            ln_vec = pl.BlockSpec((None, None, 1, tn), lambda j, i: (ln_layer, ln_idx, 0, j))
            in_specs += [tile, stat, stat, ln_vec, ln_vec]
            args += [stream["z"].reshape(m, n), stream["mu"].reshape(m, LANES), stream["rstd"].reshape(m, LANES),
                     lng4, lnb4]
        else:
            in_specs.append(tile)
            args.append(stream["h"].reshape(m, n))
        in_specs.append(pl.BlockSpec((None, None, 1, tn),
                                     lambda j, i: (mod_layer, row(i // per_seq), 0, gate_part * (n // tn) + j)))
        args.append(mod4)
    if resid is None:
        n_a = len(a2)
        inner = pltpu.emit_pipeline(
            functools.partial(_mm_kernel, n_a=n_a, alpha=None, normed=False),
            grid=(n // tn, m // tm),
            in_specs=[pl.BlockSpec((tm, a.shape[1]), lambda j, i: (i, 0)) for a in a2]
            + [pl.BlockSpec((k, tn), lambda j, i: (0, j))],
            out_specs=[pl.BlockSpec((tm, tn), lambda j, i: (i, j))])

        def outer(*refs):
            inner(*refs[:n_a], refs[n_a].at[layer], refs[n_a + 1])

        any_spec = pl.BlockSpec(memory_space=pl.ANY)
        out = pl.pallas_call(
            outer,
            in_specs=[any_spec] * (n_a + 1),
            out_specs=any_spec,
            out_shape=jax.ShapeDtypeStruct((m, n), out_dtype),
            compiler_params=pltpu.CompilerParams(vmem_limit_bytes=VMEM_LIMIT_BYTES),
            name="projection",
        )(*args)
        return out.reshape(lead + (n,))
    out = pl.pallas_call(
        functools.partial(_mm_kernel, n_a=len(a2), alpha=alpha, normed=normed),
        grid=(n // tn, m // tm),
        in_specs=in_specs,
        out_specs=pl.BlockSpec((tm, tn), lambda j, i: (i, j)),
        out_shape=jax.ShapeDtypeStruct((m, n), out_dtype),
        compiler_params=_params("parallel", "parallel"),
        name="projection",
    )(*args)
    return out.reshape(lead + (n,))


def _head_sum(x):
    li = (lax.broadcasted_iota(jnp.int32, (2 * LANES, LANES), 0) % LANES) // HEAD
    lj = lax.broadcasted_iota(jnp.int32, (2 * LANES, LANES), 1) // HEAD
    ones = (li == lj).astype(BF16)
    hi = x.astype(BF16)
    lo = (x - hi.astype(F32)).astype(BF16)
    return jnp.dot(jnp.concatenate([hi, lo], axis=1), ones, preferred_element_type=F32)


def _prep_kernel(p_ref, pp_ref, pn_ref, shw_ref, w0_ref, w2_ref, a0_ref, a2_ref, g2_ref,
                 kkw_ref, ka_ref, rk_ref,
                 r_o, v_o, kk_o, lwf_o, lwb_o, kf_o, kb_o, bf_o, bb_o, g_o, bv_o,
                 *, width, rp, rgp):
    i = pl.program_id(1)
    last = pl.num_programs(1) - 1
    tt = p_ref.shape[0]
    row = lax.broadcasted_iota(jnp.int32, (tt, 1), 0)

    def shifted(c0, c1):
        x = p_ref[:, c0:c1].astype(F32)
        hs = pp_ref.shape[0]
        prev_row = jnp.where(i > 0, pp_ref[:, c0:c1].astype(F32)[hs - 1:hs], 0.0)
        next_row = jnp.where(i < last, pn_ref[:, c0:c1].astype(F32)[0:1], 0.0)
        xp = jnp.where(row == 0, prev_row, pltpu.roll(x, 1, 0))
        xn = jnp.where(row == tt - 1, next_row, pltpu.roll(x, tt - 1, 0))
        return shw_ref[0:1, c0:c1] * xp + shw_ref[1:2, c0:c1] * x + shw_ref[2:3, c0:c1] * xn

    hp_count = width // LANES
    o = 3 * width
    kkw = kkw_ref[...]
    ka = ka_ref[...]
    rk = rk_ref[...]
    g = jnp.dot(_sigmoid(shifted(o + 4 * rp, o + 4 * rp + rgp)).astype(BF16), g2_ref[...],
                preferred_element_type=F32)
    g_o[...] = g.astype(g_o.dtype)
    rates = []
    for d in range(2):
        wd = shifted(o + d * rp, o + (d + 1) * rp)
        ad = shifted(o + (2 + d) * rp, o + (3 + d) * rp)
        wl = w0_ref[d:d + 1, :] + jnp.dot(jnp.tanh(wd).astype(BF16), w2_ref[d], preferred_element_type=F32)
        lw = -math.exp(-0.5) * _sigmoid(wl)
        ar = _sigmoid(a0_ref[d:d + 1, :] + jnp.dot(ad.astype(BF16), a2_ref[d], preferred_element_type=F32))
        rates.append(ar)
        lw_o = lwf_o if d == 0 else lwb_o
        for hp in range(hp_count):
            lw_o[hp] = lw[:, hp * LANES:(hp + 1) * LANES]
    for hp in range(hp_count):
        c0, c1 = hp * LANES, (hp + 1) * LANES
        r = shifted(c0, c1)
        k = shifted(width + c0, width + c1)
        v = shifted(2 * width + c0, 2 * width + c1)
        kkr = k * kkw[:, c0:c1]
        kk = kkr * lax.rsqrt(jnp.maximum(_head_sum(kkr * kkr), 1e-24))
        kd0 = k * (1.0 + (rates[0][:, c0:c1] - 1.0) * ka[:, c0:c1])
        kd1 = k * (1.0 + (rates[1][:, c0:c1] - 1.0) * ka[:, c0:c1])
        r_o[hp] = r.astype(r_o.dtype)
        v_o[hp] = v.astype(v_o.dtype)
        kk_o[hp] = kk.astype(kk_o.dtype)
        kf_o[hp] = kd0.astype(kf_o.dtype)
        kb_o[hp] = kd1.astype(kb_o.dtype)
        bf_o[hp] = (kk * rates[0][:, c0:c1]).astype(bf_o.dtype)
        bb_o[hp] = (kk * rates[1][:, c0:c1]).astype(bb_o.dtype)
        bonus = _head_sum(r * (0.5 * (kd0 + kd1)) * rk[:, c0:c1])
        bv_o[:, c0:c1] = (bonus * v).astype(bv_o.dtype)


def _rwkv_prep(pa, lay, rw):
    b, t, na = pa.shape
    width, rp, rgp = lay["width"], lay["rp"], lay["rgp"]
    hp = width // LANES
    tt = min(256, t)
    hs = SUBLANES * (4 // pa.dtype.itemsize)
    nb = t // hs
    tok = lambda w: pl.BlockSpec((None, tt, w), lambda bi, ti: (bi, ti, 0))
    halo_prev = pl.BlockSpec((None, hs, na), lambda bi, ti: (bi, jnp.maximum(ti * (tt // hs) - 1, 0), 0))
    halo_next = pl.BlockSpec((None, hs, na), lambda bi, ti: (bi, jnp.minimum((ti + 1) * (tt // hs), nb - 1), 0))
    full = lambda a: pl.BlockSpec(a.shape, lambda bi, ti: (0,) * a.ndim)
    pair = pl.BlockSpec((None, hp, tt, LANES), lambda bi, ti: (bi, 0, ti, 0))
    pair_shapes = [jax.ShapeDtypeStruct((b, hp, t, LANES), dt) for dt in (BF16, BF16, BF16, F32, F32, BF16, BF16, BF16, BF16)]
    consts = [rw["shift"], rw["w0"], rw["w2"], rw["a0"], rw["a2"], rw["g2"], rw["kk"], rw["ka"], rw["rk"]]
    outs = pl.pallas_call(
        functools.partial(_prep_kernel, width=width, rp=rp, rgp=rgp),
        grid=(b, t // tt),
        in_specs=[tok(na), halo_prev, halo_next] + [full(a) for a in consts],
        out_specs=[pair] * 9 + [tok(width), tok(width)],
        out_shape=pair_shapes + [jax.ShapeDtypeStruct((b, t, width), BF16)] * 2,
        compiler_params=_params("parallel", "parallel"),
        name="rwkv_streams",
    )(pa, pa, pa, *consts)
    names = ("r", "v", "kk", "lwf", "lwb", "kf", "kb", "bf", "bb", "g", "bv")
    return dict(zip(names, outs))


def _bdot(a, b):
    return jnp.dot(a.astype(BF16), b.astype(BF16), preferred_element_type=F32)


def _bdot_nt(a, b):
    return lax.dot_general(a.astype(BF16), b.astype(BF16), (((1,), (1,)), ((), ())), preferred_element_type=F32)


def _bdot_tn(a, b):
    return lax.dot_general(a.astype(BF16), b.astype(BF16), (((0,), (0,)), ((), ())), preferred_element_type=F32)


def _running_sum(upto, x):
    tri = upto.astype(BF16)
    hi = x.astype(BF16)
    rest = x - hi.astype(F32)
    mid = rest.astype(BF16)
    lo = (rest - mid.astype(F32)).astype(BF16)
    dot = lambda p: jnp.dot(tri, p, preferred_element_type=F32)
    return dot(hi) + dot(mid) + dot(lo)


def _wkv_kernel(r_ref, v_ref, kk_ref, lw_ref, k_ref, b_ref, s0_ref, y_ref, s_ref, *, reverse, ch):
    c = pl.program_id(1)

    @pl.when(c == 0)
    def _():
        s_ref[...] = s0_ref[...]

    hp_count, rows, _ = r_ref.shape
    nsub = rows // ch
    pairs = range(hp_count)
    order = list(range(nsub - 1, -1, -1)) if reverse else list(range(nsub))
    chains = [(p, s) for s in order for p in pairs]
    ti = lax.broadcasted_iota(jnp.int32, (ch, ch), 0)
    tj = lax.broadcasted_iota(jnp.int32, (ch, ch), 1)
    upto = ((tj >= ti) if reverse else (tj <= ti)).astype(F32)
    ei = lax.broadcasted_iota(jnp.int32, (ch, LANES), 0)
    ej = lax.broadcasted_iota(jnp.int32, (ch, LANES), 1)
    eye2 = (ei == ej % HEAD).astype(F32)
    gi = lax.broadcasted_iota(jnp.int32, (2 * ch, 2 * LANES), 0)
    gj = lax.broadcasted_iota(jnp.int32, (2 * ch, 2 * LANES), 1) % ch
    it = gi % ch
    earlier = (gj > it) if reverse else (gj < it)
    gmask = earlier | ((gj == it) & (gi >= ch))
    first = lax.broadcasted_iota(jnp.int32, (1, LANES), 1) < HEAD
    si = lax.broadcasted_iota(jnp.int32, (LANES, LANES), 0) // HEAD
    sj = lax.broadcasted_iota(jnp.int32, (LANES, LANES), 1) // HEAD
    same_head = si == sj
    doublings = int(math.log2(ch)) - 1
    zero = jnp.zeros((), BF16)

    def split(x):
        x = x.astype(BF16)
        return jnp.concatenate([jnp.where(first, x, zero), jnp.where(first, zero, x)], axis=0)

    ar, brows, bke, vs, vsplit, wcs = {}, {}, {}, {}, {}, {}
    for p, s in chains:
        sl = slice(s * ch, (s + 1) * ch)
        lw = lw_ref[p, sl, :]
        cl = _running_sum(upto, lw)
        tot = jnp.sum(lw, axis=0, keepdims=True)
        e_in = jnp.exp(cl)
        e_out = jnp.exp(-cl)
        e_end = jnp.exp(tot - cl)
        a_t = -kk_ref[p, sl, :].astype(F32) * jnp.exp(cl - lw)
        b = b_ref[p, sl, :].astype(F32)
        k = k_ref[p, sl, :].astype(F32)
        ar[p, s] = jnp.concatenate([a_t, r_ref[p, sl, :].astype(F32) * e_in], axis=0).astype(BF16)
        brows[p, s] = jnp.concatenate([split(b * e_out), split(k * e_out)], axis=0)
        bke[p, s] = jnp.concatenate([b * e_end, k * e_end], axis=0).astype(BF16)
        vs[p, s] = v_ref[p, sl, :]
        vsplit[p, s] = split(vs[p, s])
        wcs[p, s] = jnp.exp(tot)
    g = {c: jnp.where(gmask, _bdot_nt(ar[c], brows[c]), 0.0) for c in chains}
    lakv = {c: _bdot(g[c][:ch, LANES:], vsplit[c]) for c in chains}
    inv = {c: eye2 + g[c][:ch, :LANES] for c in chains}
    pw = {c: _bdot(g[c][:ch, :LANES], split(g[c][:ch, :LANES])) for c in chains}
    for _ in range(doublings - 1):
        prod = {c: _bdot(pw[c], jnp.concatenate([split(inv[c]), split(pw[c])], axis=1)) for c in chains}
        inv = {c: inv[c] + prod[c][:, :LANES] for c in chains}
        pw = {c: prod[c][:, LANES:] for c in chains}
    inv = {c: inv[c] + _bdot(pw[c], split(inv[c])) for c in chains}
    state = [s_ref[p] for p in pairs]
    for s in order:
        ars = [_bdot_nt(ar[p, s], state[p]) for p in pairs]
        u = [_bdot(inv[p, s], split(ars[p][:ch] + lakv[p, s])) for p in pairs]
        ys = [ars[p][ch:] + _bdot(g[p, s][ch:, :], jnp.concatenate([split(u[p]), vsplit[p, s]], axis=0))
              for p in pairs]
        upd = [_bdot_tn(jnp.concatenate([u[p].astype(BF16), vs[p, s].astype(BF16)], axis=0), bke[p, s])
               for p in pairs]
        state = [state[p] * wcs[p, s] + jnp.where(same_head, upd[p], 0.0) for p in pairs]
        for p in pairs:
            y_ref[p, s * ch:(s + 1) * ch, :] = ys[p]
    for p in pairs:
        s_ref[p] = state[p]


def _wkv_scan(st, d, s0, reverse):
    r = st["r"]
    b, hp, t, _ = r.shape
    ch = min(SCAN_CHUNK, t)
    rows = min(SCAN_CHUNKS_PER_STEP * ch, t)
    nc = t // rows
    cidx = (lambda c: nc - 1 - c) if reverse else (lambda c: c)
    blk = pl.BlockSpec((None, hp, rows, LANES), lambda bi, c: (bi, 0, cidx(c), 0))
    st_spec = pl.BlockSpec((None, hp, LANES, LANES), lambda bi, c: (bi, 0, 0, 0))
    lw, k, bb = (st["lwf"], st["kf"], st["bf"]) if d == 0 else (st["lwb"], st["kb"], st["bb"])
    y, s_last = pl.pallas_call(
        functools.partial(_wkv_kernel, reverse=reverse, ch=ch),
        grid=(b, nc),
        in_specs=[blk] * 6 + [st_spec],
        out_specs=[blk, st_spec],
        out_shape=[jax.ShapeDtypeStruct((b, hp, t, LANES), F32),
                   jax.ShapeDtypeStruct((b, hp, LANES, LANES), F32)],
        compiler_params=_params("parallel", "arbitrary"),
        name="wkv_scan",
    )(r, st["v"], st["kk"], lw, k, bb, s0)
    return y, s_last


def _rwkv_out_kernel(yf_ref, yb_ref, bv_ref, g_ref, lng_ref, lnb_ref, o_ref):
    hp_count = yf_ref.shape[0]
    for hp in range(hp_count):
        c0, c1 = hp * LANES, (hp + 1) * LANES
        y = yf_ref[hp] + yb_ref[hp]
        mu = _head_sum(y) * (1.0 / HEAD)
        yc = y - mu
        var = _head_sum(yc * yc) * (1.0 / HEAD)
        yn = yc * lax.rsqrt(var + GN_EPS) * lng_ref[:, c0:c1] + lnb_ref[:, c0:c1]
        o_ref[:, c0:c1] = ((yn + bv_ref[:, c0:c1].astype(F32)) * g_ref[:, c0:c1].astype(F32)).astype(o_ref.dtype)


def _rwkv_out(yf, yb, st, lnx_g, lnx_b):
    b, hp, t, _ = yf.shape
    width = hp * LANES
    tt = min(512, t)
    pair = pl.BlockSpec((None, hp, tt, LANES), lambda bi, ti: (bi, 0, ti, 0))
    tok = pl.BlockSpec((None, tt, width), lambda bi, ti: (bi, ti, 0))
    vec = pl.BlockSpec((1, width), lambda bi, ti: (0, 0))
    return pl.pallas_call(
        _rwkv_out_kernel,
        grid=(b, t // tt),
        in_specs=[pair, pair, tok, tok, vec, vec],
        out_specs=tok,
        out_shape=jax.ShapeDtypeStruct((b, t, width), BF16),
        compiler_params=_params("parallel", "parallel"),
        name="rwkv_out",
    )(yf, yb, st["bv"], st["g"], lnx_g, lnx_b)


def _gelu(z):
    return 0.5 * z * (1.0 + lax.erf(z * (2.0 ** -0.5)))


def _sgu_kernel(u_ref, v_ref, lng_ref, lnb_ref, ws_ref, bs_ref, o_ref):
    heads, chunk, _ = ws_ref.shape
    tt, width = u_ref.shape
    hw = width // heads
    v = _gelu(v_ref[...].astype(F32))
    mu = jnp.mean(v, axis=-1, keepdims=True)
    vc = v - mu
    var = jnp.mean(vc * vc, axis=-1, keepdims=True)
    vn = (vc * lax.rsqrt(var + LN_EPS) * lng_ref[...] + lnb_ref[...]).astype(BF16)
    for r0 in range(0, tt, chunk):
        rows = slice(r0, r0 + chunk)
        for h in range(heads):
            c0, c1 = h * hw, (h + 1) * hw
            s = jnp.dot(ws_ref[h], vn[rows, c0:c1], preferred_element_type=F32) + bs_ref[:, h:h + 1]
            o_ref[rows, c0:c1] = (_gelu(u_ref[rows, c0:c1].astype(F32)) * s).astype(o_ref.dtype)


def _sgu(pb, ln_g, ln_b, ws, bs_t):
    b, t, w2 = pb.shape
    width = w2 // 2
    heads, chunk, _ = ws.shape
    tt = min(4 * chunk, t)
    tok_u = pl.BlockSpec((None, tt, width), lambda bi, ti: (bi, ti, 0))
    tok_v = pl.BlockSpec((None, tt, width), lambda bi, ti: (bi, ti, 1))
    full = lambda a: pl.BlockSpec(a.shape, lambda bi, ti: (0,) * a.ndim)
    return pl.pallas_call(
        _sgu_kernel,
        grid=(b, t // tt),
        in_specs=[tok_u, tok_v, full(ln_g), full(ln_b), full(ws), full(bs_t)],
        out_specs=tok_u,
        out_shape=jax.ShapeDtypeStruct((b, t, width), BF16),
        compiler_params=_params("parallel", "parallel"),
        name="spatial_gating",
    )(pb, pb, ln_g, ln_b, ws, bs_t)


def _odd_kernel(xc_ref, bg_ref, cg_ref, xd_ref, pw_ref, ps_ref, cw_ref, o_ref, *, seg):
    tt, width = xc_ref.shape
    groups = pw_ref.shape[0]
    gc = width // groups
    ti = lax.broadcasted_iota(jnp.int32, (tt, tt), 0)
    tj = lax.broadcasted_iota(jnp.int32, (tt, tt), 1)
    same_seg = (ti // seg) == (tj // seg)
    pos = lax.broadcasted_iota(jnp.int32, (tt, 1), 0) % seg
    cols = [slice(gi * gc, (gi + 1) * gc) for gi in range(groups)]
    sums, inv_count = [], []
    for gi in range(groups):
        win = POOL_WINDOWS[gi]
        lo = win // 2
        hi = win - 1 - lo
        band = (same_seg & (tj >= ti - lo) & (tj <= ti + hi)).astype(BF16)
        count = (jnp.minimum(pos + hi, seg - 1) - jnp.maximum(pos - lo, 0) + 1).astype(F32)
        inv_count.append(1.0 / count)
        sums.append(jnp.dot(band, xc_ref[:, cols[gi]], preferred_element_type=F32))
    p = [(sums[gi] * inv_count[gi] - xc_ref[:, cols[gi]].astype(F32)).astype(BF16) for gi in range(groups)]
    ys = [jnp.dot(p[gi], pw_ref[gi], preferred_element_type=F32) for gi in range(groups)]
    for gi in range(groups):
        o_ref[:, cols[gi]] = (ys[gi] * ps_ref[:, cols[gi]]).astype(o_ref.dtype)
    z = cg_ref[...].astype(F32) * xd_ref[...].astype(F32)
    zp = jnp.where(pos == 0, 0.0, pltpu.roll(z, 1, 0))
    zn = jnp.where(pos == seg - 1, 0.0, pltpu.roll(z, tt - 1, 0))
    conv = cw_ref[0:1, :] * zp + cw_ref[1:2, :] * z + cw_ref[2:3, :] * zn
    o_ref[:, width:] = (bg_ref[...].astype(F32) * conv).astype(o_ref.dtype)


def _odd_mix(p, pool_w, pool_scale, sconv_w, seg):
    b, t, w4 = p.shape
    width = w4 // 4
    tt = min(max(256, seg), t)
    assert tt % seg == 0 and t % tt == 0
    col = lambda ci: pl.BlockSpec((None, tt, width), lambda bi, ti: (bi, ti, ci))
    full = lambda a: pl.BlockSpec(a.shape, lambda bi, ti: (0,) * a.ndim)
    return pl.pallas_call(
        functools.partial(_odd_kernel, seg=seg),
        grid=(b, t // tt),
        in_specs=[col(0), col(1), col(2), col(3), full(pool_w), full(pool_scale), full(sconv_w)],
        out_specs=pl.BlockSpec((None, tt, 2 * width), lambda bi, ti: (bi, ti, 0)),
        out_shape=jax.ShapeDtypeStruct((b, t, 2 * width), BF16),
        compiler_params=_params("parallel", "parallel"),
        name="pool_shortconv",
    )(p, p, p, p, pool_w, pool_scale, sconv_w)


def _ffn_mid_kernel(g_ref, gp_ref, gn_ref, val_ref, cw_ref, o_ref, *, shift):
    i = pl.program_id(1)
    last = pl.num_programs(1) - 1
    tt = g_ref.shape[0]
    hs = gp_ref.shape[0]
    g = g_ref[...].astype(F32)
    if hs == shift:
        prev_blk = jnp.where(i > 0, gp_ref[...].astype(F32), 0.0)
        next_blk = jnp.where(i < last, gn_ref[...].astype(F32), 0.0)
        if tt > shift:
            gp = jnp.concatenate([prev_blk, g[:tt - shift]], axis=0)
            gn = jnp.concatenate([g[shift:], next_blk], axis=0)
        else:
            gp, gn = prev_blk, next_blk
    else:
        row = lax.broadcasted_iota(jnp.int32, (tt, 1), 0)
        prev_row = jnp.where(i > 0, gp_ref[...].astype(F32)[hs - 1:hs, :], 0.0)
        next_row = jnp.where(i < last, gn_ref[...].astype(F32)[0:1, :], 0.0)
        gp = jnp.where(row == 0, prev_row, pltpu.roll(g, 1, 0))
        gn = jnp.where(row == tt - 1, next_row, pltpu.roll(g, tt - 1, 0))
    hw = 0.5 * cw_ref[...]
    half = hw[0:1, :] * gp + hw[1:2, :] * g + hw[2:3, :] * gn
    o_ref[...] = (half * (1.0 + jnp.tanh(half)) * val_ref[...].astype(F32)).astype(o_ref.dtype)


def _ffn_mid(up, conv_w, shift):
    b, t, f2 = up.shape
    f = f2 // 2
    tt = min(512, t)
    tc = f
    for cand in (2560, 1280, 1024, 512, 256, 128):
        if f % cand == 0:
            tc = cand
            break
    ncol = f // tc
    row_tile = SUBLANES * (4 // up.dtype.itemsize)
    hs = shift if shift % row_tile == 0 else row_tile
    assert shift == 1 or (hs == shift and tt % shift == 0)
    nh = t // hs
    main = pl.BlockSpec((None, tt, tc), lambda bi, ti, ci: (bi, ti, ci))
    prev = pl.BlockSpec((None, hs, tc), lambda bi, ti, ci: (bi, jnp.maximum(ti * (tt // hs) - 1, 0), ci))
    nxt = pl.BlockSpec((None, hs, tc), lambda bi, ti, ci: (bi, jnp.minimum((ti + 1) * (tt // hs), nh - 1), ci))
    val = pl.BlockSpec((None, tt, tc), lambda bi, ti, ci: (bi, ti, ci + ncol))
    cw = pl.BlockSpec((3, tc), lambda bi, ti, ci: (0, ci))
    return pl.pallas_call(
        functools.partial(_ffn_mid_kernel, shift=shift),
        grid=(b, t // tt, ncol),
        in_specs=[main, prev, nxt, val, cw],
        out_specs=main,
        out_shape=jax.ShapeDtypeStruct((b, t, f), BF16),
        compiler_params=_params("parallel", "parallel", "parallel"),
        name="convffn_gate",
    )(up, up, up, up, conv_w)


def _pad_cols(w, n):
    return jnp.pad(w, ((0, 0),) * (w.ndim - 1) + ((0, n - w.shape[-1]),))


def _even_layout(width, r_decay, r_aaa, r_gate):
    rp = _round_up(max(r_decay, r_aaa), LANES)
    rgp = _round_up(r_gate, LANES)
    na = _round_up(3 * width + 4 * rp + rgp, 1024)
    return dict(width=width, rp=rp, rgp=rgp, na=na)


def _permute_rwkv_cols(w, lay, r_decay, r_aaa, r_gate):
    width, rp, rgp, na = lay["width"], lay["rp"], lay["rgp"], lay["na"]
    o = 3 * width
    parts = [w[..., :o]]
    for seg in (r_decay, r_decay, r_aaa, r_aaa):
        parts.append(_pad_cols(w[..., o:o + seg], rp))
        o += seg
    parts.append(_pad_cols(w[..., o:o + r_gate], rgp))
    return _pad_cols(jnp.concatenate(parts, axis=-1), na)


def _pad_rows(w, n):
    pad = [(0, 0)] * w.ndim
    pad[-2] = (0, n - w.shape[-2])
    return jnp.pad(w, pad)


def kernel(x, c, ctx, c_ctx, ada_w, ada_b, ln_g, ln_b, ffn_w_up, ffn_conv, ffn_w_down, ev_w_in, ev_w_out, ev_shift, rwkv_w0, rwkv_w2, rwkv_a0, rwkv_a2, rwkv_g2, rwkv_kk, rwkv_ka, rwkv_rk, rwkv_lnx_g, rwkv_lnx_b, sgu_ln_g, sgu_ln_b, sgu_w, sgu_b, od_w_in, od_w_out, pool_w, pool_scale, sconv_w):
    batch, seq, d = x.shape
    depth = ada_w.shape[0]
    alpha = (2 * depth) ** 0.25
    last_cross = 2 * ((depth - 1) // 2)
    heads = rwkv_rk.shape[1]
    width_a = heads * HEAD
    r_decay, r_aaa, r_gate = rwkv_w2.shape[2], rwkv_a2.shape[2], rwkv_g2.shape[1]
    in_a = 3 * width_a + 2 * r_decay + 2 * r_aaa + r_gate
    lay = _even_layout(width_a, r_decay, r_aaa, r_gate)

    rows = _round_up(batch + 1, SUBLANES)
    cc = jnp.zeros((rows, d), F32).at[:batch].set(c).at[batch].set(c_ctx)
    mod4 = _modulation_all(cc, ada_w, ada_b).reshape(depth, rows, 1, 6 * d)
    lng4 = ln_g.reshape(depth, 2, 1, d)
    lnb4 = ln_b.reshape(depth, 2, 1, d)
    lat_row = lambda b: b
    ctx_row = lambda b: batch

    ev_in = ev_w_in.astype(BF16)
    w_a = _permute_rwkv_cols(ev_in[..., :in_a], lay, r_decay, r_aaa, r_gate)
    w_b = ev_in[..., in_a:]
    ev_out = ev_w_out.astype(BF16)
    od_in = od_w_in.astype(BF16)
    od_out = od_w_out.astype(BF16)
    w_up = ffn_w_up.astype(BF16)
    w_down = ffn_w_down.astype(BF16)
    shift_all = _permute_rwkv_cols(ev_shift, lay, r_decay, r_aaa, r_gate)

    h_lat, h_ctx = dict(h=x), dict(h=ctx)
    a_lat = _modulate(x, mod4, 0, lat_row, 1, 0)
    a_ctx = _modulate(ctx, mod4, 0, ctx_row, 1, 0)
    for i in range(depth):
        ctx_out = i < last_cross
        j = i // 2
        mix_ctx = None
        if i % 2 == 0:
            rw = dict(
                shift=shift_all[j],
                w0=rwkv_w0[j], a0=rwkv_a0[j],
                w2=_pad_rows(rwkv_w2[j], lay["rp"]).astype(BF16),
                a2=_pad_rows(rwkv_a2[j], lay["rp"]).astype(BF16),
                g2=_pad_rows(rwkv_g2[j], lay["rgp"]).astype(BF16),
                kk=rwkv_kk[j].reshape(1, width_a), ka=rwkv_ka[j].reshape(1, width_a),
                rk=rwkv_rk[j].reshape(1, width_a))
            lnx_g = rwkv_lnx_g[j].reshape(1, width_a)
            lnx_b = rwkv_lnx_b[j].reshape(1, width_a)
            sgu_g = sgu_ln_g[j].reshape(1, -1)
            sgu_bb = sgu_ln_b[j].reshape(1, -1)
            sgu_ws = sgu_w[j].astype(BF16)
            sgu_bt = sgu_b[j].T

            st_lat = _rwkv_prep(_matmul([a_lat], w_a, j, BF16), lay, rw)
            st_ctx = _rwkv_prep(_matmul([a_ctx], w_a, j, BF16), lay, rw)
            s0 = jnp.zeros((batch, width_a // LANES, LANES, LANES), F32)
            yc_f, sc_f = _wkv_scan(st_ctx, 0, s0, False)
            yl_f, _ = _wkv_scan(st_lat, 0, sc_f, False)
            yc_b, sc_b = _wkv_scan(st_ctx, 1, s0, True)
            yl_b, _ = _wkv_scan(st_lat, 1, sc_b, True)
            ya_lat = _rwkv_out(yl_f, yl_b, st_lat, lnx_g, lnx_b)
            yb_lat = _sgu(_matmul([a_lat], w_b, j, BF16), sgu_g, sgu_bb, sgu_ws, sgu_bt)
            mix_lat, w_mix = [ya_lat, yb_lat], ev_out
            if ctx_out:
                ya_ctx = _rwkv_out(yc_f, yc_b, st_ctx, lnx_g, lnx_b)
                yb_ctx = _sgu(_matmul([a_ctx], w_b, j, BF16), sgu_g, sgu_bb, sgu_ws, sgu_bt)
                mix_ctx = [ya_ctx, yb_ctx]
        else:
            pw = pool_w[j].astype(BF16)
            ps = pool_scale[j].reshape(1, -1)
            mix_lat, w_mix = [_odd_mix(_matmul([a_lat], od_in, j, BF16), pw, ps, sconv_w[j], GRID_W)], od_out
            if ctx_out:
                mix_ctx = [_odd_mix(_matmul([a_ctx], od_in, j, BF16), pw, ps, sconv_w[j], ctx.shape[1])]

        nxt = (i + 1, 1, 0) if i + 1 < depth else None
        res = lambda stream, part, row: (stream, lng4, lnb4, mod4, i, part, row, alpha)
        z = _matmul(mix_lat, w_mix, j, resid=res(h_lat, 2, lat_row))
        h_lat, a2 = _ln_residual(z, mod4, lng4, lnb4, lat_row, i, 0, (i, 4, 3), False)
        gated = _ffn_mid(_matmul([a2], w_up, i, BF16), ffn_conv[i], GRID_W)
        z = _matmul([gated], w_down, i, resid=res(h_lat, 5, lat_row))
        h_lat, a_lat = _ln_residual(z, mod4, lng4, lnb4, lat_row, i, 1, nxt, nxt is None)
        if ctx_out:
            z = _matmul(mix_ctx, w_mix, j, resid=res(h_ctx, 2, ctx_row))
            h_ctx, a2 = _ln_residual(z, mod4, lng4, lnb4, ctx_row, i, 0, (i, 4, 3), False)
            gated = _ffn_mid(_matmul([a2], w_up, i, BF16), ffn_conv[i], 1)
            z = _matmul([gated], w_down, i, resid=res(h_ctx, 5, ctx_row))
            h_ctx, a_ctx = _ln_residual(z, mod4, lng4, lnb4, ctx_row, i, 1, nxt, False)
    return h_lat["h"]
```
